```python
import math
import jax, jax.numpy as jnp
from jax import lax
import numpy as np

D_MODEL = 1024
BATCH = 8
SEQ = 2048
DEPTH = 4
DEC_BATCH = 128
DEC_SEQ = 4
PAST_LEN = 8192
PAGE_SIZE = 128

N_META = 16
HEAD_DIM = 64
N_Q_HEADS = 8
N_KV_HEADS = 2
GQA_GROUP = N_Q_HEADS // N_KV_HEADS
ATTN_W = N_Q_HEADS * HEAD_DIM
KV_W = N_KV_HEADS * HEAD_DIM
LRU_W = D_MODEL - ATTN_W
LRU_BLOCKS = 8
LRU_BLOCK_W = LRU_W // LRU_BLOCKS
CONV_W = 4
LRU_C = 8.0
WINDOW = 128
BLOCK = 128
ROPE_THETA = 10000.0
D_FF = -(-8 * D_MODEL // (3 * 256)) * 256
IN_W = ATTN_W + 2 * KV_W + 2 * LRU_W
MIX_W = ATTN_W + LRU_W
EPS = 1e-6

kernel_name = "hymba_griffin_swa_sink_step"


def rms_norm(x, g):
    xf = x.astype(jnp.float32)
    y = xf * lax.rsqrt(jnp.mean(xf * xf, axis=-1, keepdims=True) + EPS)
    return (y * g.astype(jnp.float32)).astype(x.dtype)


def rope(x, pos):
    half = HEAD_DIM // 2
    inv = ROPE_THETA ** (-jnp.arange(half, dtype=jnp.float32) / half)
    ang = pos.astype(jnp.float32)[:, None] * inv[None, :]
    cos = jnp.cos(ang)[:, None, :]
    sin = jnp.sin(ang)[:, None, :]
    xf = x.astype(jnp.float32)
    x1, x2 = xf[..., :half], xf[..., half:]
    out = jnp.concatenate([x1 * cos - x2 * sin, x2 * cos + x1 * sin], axis=-1)
    return out.astype(x.dtype)


def sink_softmax(s, sink):
    m = jnp.maximum(jnp.max(s, axis=-1, keepdims=True), sink)
    p = jnp.exp(s - m)
    return p / (jnp.sum(p, axis=-1, keepdims=True) + jnp.exp(sink - m))


def swa_prompt(q, k, v, sinks):
    B, L = q.shape[0], q.shape[1]
    pad = (-L) % BLOCK
    padw = ((0, 0), (pad, 0), (0, 0), (0, 0))
    qp, kp, vp = jnp.pad(q, padw), jnp.pad(k, padw), jnp.pad(v, padw)
    nb = (L + pad) // BLOCK
    qb = qp.reshape(B, nb, BLOCK, N_KV_HEADS, GQA_GROUP, HEAD_DIM)
    kb = kp.reshape(B, nb, BLOCK, N_KV_HEADS, HEAD_DIM)
    vb = vp.reshape(B, nb, BLOCK, N_KV_HEADS, HEAD_DIM)
    shift = ((0, 0), (1, 0), (0, 0), (0, 0), (0, 0))
    kk = jnp.concatenate([jnp.pad(kb, shift)[:, :-1], kb], axis=2)
    vv = jnp.concatenate([jnp.pad(vb, shift)[:, :-1], vb], axis=2)
    s = jnp.einsum('bnqkgd,bnskd->bnkgqs', qb, kk,
                   preferred_element_type=jnp.float32) * (HEAD_DIM ** -0.5)
    i = jnp.arange(BLOCK)[:, None]
    j = jnp.arange(2 * BLOCK)[None, :]
    n = jnp.arange(nb)[:, None, None]
    kidx = (n - 1) * BLOCK + j - pad
    rel = BLOCK + i - j
    mask = (kidx >= 0) & (rel >= 0) & (rel < WINDOW)
    s = jnp.where(mask[None, :, None, None], s, -jnp.inf)
    sink = sinks.astype(jnp.float32).reshape(N_KV_HEADS, GQA_GROUP)[:, :, None, None]
    p = sink_softmax(s, sink)
    o = jnp.einsum('bnkgqs,bnskd->bnqkgd', p.astype(v.dtype), vv)
    return o.reshape(B, L + pad, ATTN_W)[:, pad:]


def swa_sample(q, k_new, v_new, k_buf, v_buf, sinks, pos):
    B, S = q.shape[0], q.shape[1]
    kk = jnp.concatenate([k_buf.astype(k_new.dtype), k_new], axis=1)
    vv = jnp.concatenate([v_buf.astype(v_new.dtype), v_new], axis=1)
    kpos = jnp.concatenate([pos[0] - WINDOW + jnp.arange(WINDOW, dtype=pos.dtype), pos])
    rel = pos[:, None] - kpos[None, :]
    mask = (kpos[None, :] >= 0) & (rel >= 0) & (rel < WINDOW)
    qg = q.reshape(B, S, N_KV_HEADS, GQA_GROUP, HEAD_DIM)
    s = jnp.einsum('bqkgd,bskd->bkgqs', qg, kk,
                   preferred_element_type=jnp.float32) * (HEAD_DIM ** -0.5)
    s = jnp.where(mask, s, -jnp.inf)
    sink = sinks.astype(jnp.float32).reshape(N_KV_HEADS, GQA_GROUP)[:, :, None, None]
    p = sink_softmax(s, sink)
    o = jnp.einsum('bkgqs,bskd->bqkgd', p.astype(vv.dtype), vv)
    return o.reshape(B, S, ATTN_W), kk[:, -WINDOW:], vv[:, -WINDOW:]


def rg_lru(xb, conv_prefix, h0, conv_w, conv_b, w_a, b_a, w_i, b_i, lam):
    B, L = xb.shape[0], xb.shape[1]
    xc = jnp.concatenate([conv_prefix.astype(xb.dtype), xb], axis=1)
    u = conv_b.astype(xb.dtype)
    for j in range(CONV_W):
        u = u + xc[:, j:j + L] * conv_w[j].astype(xb.dtype)
    new_conv = xc[:, -(CONV_W - 1):]
    ub = u.reshape(B, L, LRU_BLOCKS, LRU_BLOCK_W)
    r = jax.nn.sigmoid((jnp.einsum('blhi,hij->blhj', ub, w_a).reshape(B, L, LRU_W)
                        + b_a).astype(jnp.float32))
    ig = jax.nn.sigmoid((jnp.einsum('blhi,hij->blhj', ub, w_i).reshape(B, L, LRU_W)
                         + b_i).astype(jnp.float32))
    log_a = LRU_C * r * jax.nn.log_sigmoid(lam.astype(jnp.float32))
    a = jnp.exp(log_a)
    bt = jnp.sqrt(-jnp.expm1(2.0 * log_a)) * ig * u.astype(jnp.float32)

    def step(h, ab):
        a_t, b_t = ab
        h = a_t * h + b_t
        return h, h

    hT, hs = lax.scan(step, h0.astype(jnp.float32),
                      (jnp.swapaxes(a, 0, 1), jnp.swapaxes(bt, 0, 1)))
    return jnp.swapaxes(hs, 0, 1).astype(xb.dtype), hT, new_conv


def decoder_stack(x, pos, k_bufs, v_bufs, h0s, conv0s,
                  pre_mix_norm, w_in, sinks, conv_w, conv_b, w_a, b_a, w_i, b_i, lam,
                  attn_out_norm, lru_out_norm, w_out, post_mix_norm,
                  pre_ffn_norm, w_gate, w_up, w_down, post_ffn_norm):
    B, L = x.shape[0], x.shape[1]
    splits = [ATTN_W, ATTN_W + KV_W, ATTN_W + 2 * KV_W, ATTN_W + 2 * KV_W + LRU_W]
    nks, nvs, nhs, ncs = [], [], [], []
    for l in range(DEPTH):
        hn = rms_norm(x, pre_mix_norm[l])
        z = hn @ w_in[l]
        q, k, v, xb, gate = jnp.split(z, splits, axis=-1)
        q = rope(q.reshape(B, L, N_Q_HEADS, HEAD_DIM), pos)
        k = rope(k.reshape(B, L, N_KV_HEADS, HEAD_DIM), pos)
        v = v.reshape(B, L, N_KV_HEADS, HEAD_DIM)
        if k_bufs is None:
            o_att = swa_prompt(q, k, v, sinks[l])
            nk, nv = k[:, -WINDOW:], v[:, -WINDOW:]
        else:
            o_att, nk, nv = swa_sample(q, k, v, k_bufs[l], v_bufs[l], sinks[l], pos)
        h_seq, hT, nconv = rg_lru(xb, conv0s[l], h0s[l], conv_w[l], conv_b[l],
                                  w_a[l], b_a[l], w_i[l], b_i[l], lam[l])
        o_lru = h_seq * jax.nn.gelu(gate, approximate=True)
        merged = jnp.concatenate([rms_norm(o_att, attn_out_norm[l]),
                                  rms_norm(o_lru, lru_out_norm[l])], axis=-1)
        x = x + rms_norm(merged @ w_out[l], post_mix_norm[l])
        hf = rms_norm(x, pre_ffn_norm[l])
        f = (jax.nn.silu(hf @ w_gate[l]) * (hf @ w_up[l])) @ w_down[l]
        x = x + rms_norm(f, post_ffn_norm[l])
        nks.append(nk); nvs.append(nv); nhs.append(hT); ncs.append(nconv)
    return x, jnp.stack(nks), jnp.stack(nvs), jnp.stack(nhs), jnp.stack(ncs)


def setup_inputs(seed: int = 0) -> dict:
    key = jax.random.key(seed)
    ks = jax.random.split(key, 32)
    f32 = jnp.float32
    nrm = lambda k, shape, s: jax.random.normal(k, shape, f32) * s
    gain = lambda k, shape: 1.0 + 0.01 * jax.random.normal(k, shape, f32)
    u = jax.random.uniform(ks[0], (DEPTH, LRU_W), f32, 0.9, 0.999)
    a0 = u ** (1.0 / LRU_C)
    lam = jnp.log(a0) - jnp.log1p(-a0)
    return {
        "x_prompt": nrm(ks[1], (BATCH, SEQ, D_MODEL), 1.0),
        "x_sample": nrm(ks[2], (DEC_BATCH, DEC_SEQ, D_MODEL), 1.0),
        "cache_k": nrm(ks[3], (DEPTH, DEC_BATCH, WINDOW, N_KV_HEADS, HEAD_DIM), 1.0),
        "cache_v": nrm(ks[4], (DEPTH, DEC_BATCH, WINDOW, N_KV_HEADS, HEAD_DIM), 1.0),
        "state_h": nrm(ks[5], (DEPTH, DEC_BATCH, LRU_W), 0.5),
        "state_conv": nrm(ks[6], (DEPTH, DEC_BATCH, CONV_W - 1, LRU_W), 1.0),
        "meta_tokens": nrm(ks[7], (N_META, D_MODEL), 1.0),
        "pre_mix_norm": gain(ks[8], (DEPTH, D_MODEL)),
        "w_in": nrm(ks[9], (DEPTH, D_MODEL, IN_W), D_MODEL ** -0.5),
        "sinks": nrm(ks[10], (DEPTH, N_Q_HEADS), 0.5),
        "conv_w": nrm(ks[11], (DEPTH, CONV_W, LRU_W), CONV_W ** -0.5),
        "conv_b": nrm(ks[12], (DEPTH, LRU_W), 0.01),
        "w_a": nrm(ks[13], (DEPTH, LRU_BLOCKS, LRU_BLOCK_W, LRU_BLOCK_W), LRU_BLOCK_W ** -0.5),
        "b_a": nrm(ks[14], (DEPTH, LRU_W), 0.01),
        "w_i": nrm(ks[15], (DEPTH, LRU_BLOCKS, LRU_BLOCK_W, LRU_BLOCK_W), LRU_BLOCK_W ** -0.5),
        "b_i": nrm(ks[16], (DEPTH, LRU_W), 0.01),
        "lam": lam,
        "attn_out_norm": gain(ks[17], (DEPTH, ATTN_W)),
        "lru_out_norm": gain(ks[18], (DEPTH, LRU_W)),
        "w_out": nrm(ks[19], (DEPTH, MIX_W, D_MODEL), MIX_W ** -0.5),
        "post_mix_norm": gain(ks[20], (DEPTH, D_MODEL)),
        "pre_ffn_norm": gain(ks[21], (DEPTH, D_MODEL)),
        "w_gate": nrm(ks[22], (DEPTH, D_MODEL, D_FF), D_MODEL ** -0.5),
        "w_up": nrm(ks[23], (DEPTH, D_MODEL, D_FF), D_MODEL ** -0.5),
        "w_down": nrm(ks[24], (DEPTH, D_FF, D_MODEL), D_FF ** -0.5),
        "post_ffn_norm": gain(ks[25], (DEPTH, D_MODEL)),
    }


def reference(x_prompt, x_sample, cache_k, cache_v, state_h, state_conv, meta_tokens,
              pre_mix_norm, w_in, sinks, conv_w, conv_b, w_a, b_a, w_i, b_i, lam,
              attn_out_norm, lru_out_norm, w_out, post_mix_norm,
              pre_ffn_norm, w_gate, w_up, w_down, post_ffn_norm):
    weights = (pre_mix_norm, w_in, sinks, conv_w, conv_b, w_a, b_a, w_i, b_i, lam,
               attn_out_norm, lru_out_norm, w_out, post_mix_norm,
               pre_ffn_norm, w_gate, w_up, w_down, post_ffn_norm)
    B = x_prompt.shape[0]
    meta = jnp.broadcast_to(meta_tokens.astype(x_prompt.dtype)[None], (B, N_META, D_MODEL))
    xp = jnp.concatenate([meta, x_prompt], axis=1)
    pos_p = jnp.arange(xp.shape[1], dtype=jnp.int32)
    h0 = jnp.zeros((DEPTH, B, LRU_W), jnp.float32)
    c0 = jnp.zeros((DEPTH, B, CONV_W - 1, LRU_W), x_prompt.dtype)
    yp, prompt_k, prompt_v, prompt_h, prompt_conv = decoder_stack(
        xp, pos_p, None, None, h0, c0, *weights)
    y_prompt = yp[:, N_META:]
    pos_s = PAST_LEN + jnp.arange(x_sample.shape[1], dtype=jnp.int32)
    y_sample, sample_k, sample_v, sample_h, sample_conv = decoder_stack(
        x_sample, pos_s, cache_k, cache_v, state_h, state_conv, *weights)
    return (y_prompt, y_sample, prompt_k, prompt_v, prompt_h, prompt_conv,
            sample_k, sample_v, sample_h, sample_conv)
```

```python
import functools

import jax
import jax.numpy as jnp
from jax import lax
from jax.experimental import pallas as pl
from jax.experimental.pallas import tpu as pltpu

D_MODEL = 1024
DEPTH = 4
PAST_LEN = 8192
N_META = 16
HEAD_DIM = 64
N_Q_HEADS = 8
N_KV_HEADS = 2
ATTN_W = N_Q_HEADS * HEAD_DIM
KV_W = N_KV_HEADS * HEAD_DIM
LRU_W = D_MODEL - ATTN_W
LRU_BLOCKS = 8
LRU_BLOCK_W = LRU_W // LRU_BLOCKS
CONV_W = 4
LRU_C = 8.0
WINDOW = 128
ROPE_THETA = 10000.0
D_FF = 2816
IN_W = ATTN_W + 2 * KV_W + 2 * LRU_W
EPS = 1e-6

LANES = 128
SUBLANES = 8
N_SLABS = LRU_W // LANES
N_PAIRS = ATTN_W // LANES
PAIRS_PER_KV = N_PAIRS // N_KV_HEADS
VMEM_LIMIT = 56 * 1024 * 1024

BF16 = jnp.bfloat16
F32 = jnp.float32


def _rms(x, g):
    ms = jnp.mean(x * x, axis=-1, keepdims=True)
    return x * lax.rsqrt(ms + EPS) * g


def _dot(a, b):
    return jnp.dot(a, b, preferred_element_type=F32)


def _dot_nt(a, b):
    return lax.dot_general(a, b, (((1,), (1,)), ((), ())), preferred_element_type=F32)


def _const_spec(shape):
    zeros = (0,) * len(shape)
    return pl.BlockSpec(shape, lambda *_: zeros)


def _inproj_kernel(x_ref, g_ref, w_ref, cos_ref, sin_ref,
                   q_ref, k_ref, v_ref, xb_ref, gate_ref):
    hn = _rms(x_ref[...], g_ref[...]).astype(BF16)
    z = _dot(hn, w_ref[...])
    cos = cos_ref[...]
    sin = sin_ref[...]
    lane = lax.broadcasted_iota(jnp.int32, cos.shape, 1)
    first_half = (lane & (HEAD_DIM // 2)) == 0

    def rope(t):
        swapped = jnp.where(first_half,
                            pltpu.roll(t, LANES - HEAD_DIM // 2, 1),
                            pltpu.roll(t, HEAD_DIM // 2, 1))
        return t * cos + swapped * sin

    for p in range(N_PAIRS):
        qp = rope(z[:, p * LANES:(p + 1) * LANES]) * (HEAD_DIM ** -0.5)
        q_ref[:, p * LANES:(p + 1) * LANES] = qp.astype(q_ref.dtype)
    k_ref[...] = rope(z[:, ATTN_W:ATTN_W + KV_W])
    v_ref[...] = z[:, ATTN_W + KV_W:ATTN_W + 2 * KV_W]
    xb_ref[...] = z[:, ATTN_W + 2 * KV_W:ATTN_W + 2 * KV_W + LRU_W]
    gate_ref[...] = z[:, ATTN_W + 2 * KV_W + LRU_W:]


def _inproj(x, g, w, cos, sin, tm, tiles_per_seq):
    t = x.shape[0]
    n_tiles = t // tm
    tok = lambda w_: pl.BlockSpec((tm, w_), lambda i: (i, 0))
    tab = pl.BlockSpec((tm, LANES), lambda i: (i % tiles_per_seq, 0))
    return pl.pallas_call(
        _inproj_kernel,
        grid=(n_tiles,),
        in_specs=[tok(D_MODEL), _const_spec((1, D_MODEL)), _const_spec((D_MODEL, IN_W)), tab, tab],
        out_specs=[tok(ATTN_W), tok(KV_W), tok(KV_W), tok(LRU_W), tok(LRU_W)],
        out_shape=[jax.ShapeDtypeStruct((t, ATTN_W), BF16),
                   jax.ShapeDtypeStruct((t, KV_W), F32),
                   jax.ShapeDtypeStruct((t, KV_W), F32),
                   jax.ShapeDtypeStruct((t, LRU_W), F32),
                   jax.ShapeDtypeStruct((t, LRU_W), F32)],
        compiler_params=pltpu.CompilerParams(
            dimension_semantics=("arbitrary",), vmem_limit_bytes=VMEM_LIMIT),
        name="inproj",
    )(x, g, w, cos, sin)


def _ffn_kernel(x_ref, att_ref, lru_ref, wo_ref, gpm_ref, gpf_ref,
                wg_ref, wu_ref, wd_ref, gpo_ref, o_ref):
    m = _dot(att_ref[...], wo_ref[0:ATTN_W, :]) + _dot(lru_ref[...], wo_ref[ATTN_W:, :])
    x1 = x_ref[...] + _rms(m, gpm_ref[...])
    hf = _rms(x1, gpf_ref[...]).astype(BF16)
    gt = _dot(hf, wg_ref[...])
    up = _dot(hf, wu_ref[...])
    hid = (gt * jax.nn.sigmoid(gt) * up).astype(BF16)
    f = _dot(hid, wd_ref[...])
    o_ref[...] = x1 + _rms(f, gpo_ref[...])


def _ffn(x, att, lru, wo, gpm, gpf, wg, wu, wd, gpo, tm):
    t = x.shape[0]
    tok = lambda w_: pl.BlockSpec((tm, w_), lambda i: (i, 0))
    once = lambda shape: pl.BlockSpec(shape, lambda i: (0,) * len(shape),
                                      pipeline_mode=pl.Buffered(1))
    return pl.pallas_call(
        _ffn_kernel,
        grid=(t // tm,),
        in_specs=[tok(D_MODEL), tok(ATTN_W), tok(LRU_W),
                  once((D_MODEL, D_MODEL)), _const_spec((1, D_MODEL)), _const_spec((1, D_MODEL)),
                  once((D_MODEL, D_FF)), once((D_MODEL, D_FF)), once((D_FF, D_MODEL)),
                  _const_spec((1, D_MODEL))],
        out_specs=tok(D_MODEL),
        out_shape=jax.ShapeDtypeStruct((t, D_MODEL), F32),
        compiler_params=pltpu.CompilerParams(
            dimension_semantics=("arbitrary",), vmem_limit_bytes=VMEM_LIMIT),
        name="outffn",
    )(x, att, lru, wo, gpm, gpf, wg, wu, wd, gpo)


def _kv_variants(k, v):
    lane = lax.broadcasted_iota(jnp.int32, k.shape, 1)
    low = lane < HEAD_DIM
    kr = pltpu.roll(k, HEAD_DIM, 1)
    vr = pltpu.roll(v, HEAD_DIM, 1)
    zero = jnp.zeros_like(k)
    ka = (jnp.where(low, k, zero), jnp.where(low, kr, zero))
    kb = (jnp.where(low, zero, kr), jnp.where(low, zero, k))
    vv = (jnp.where(low, v, vr), jnp.where(low, vr, v))
    cast = lambda pair: tuple(a.astype(BF16) for a in pair)
    return cast(ka), cast(kb), cast(vv)


def _sink_softmax_pv(s, mask, sink, vv):
    s = jnp.where(mask, s, -jnp.inf)
    mx = jnp.maximum(jnp.max(s, axis=1, keepdims=True), sink)
    p = jnp.exp(s - mx)
    den = jnp.sum(p, axis=1, keepdims=True) + jnp.exp(sink - mx)
    return _dot(p.astype(BF16), vv) / den


def _attn_prompt_kernel(sink_ref, q_ref, k_ref, v_ref, g_ref, o_ref,
                        ka_ref, kb_ref, vv_ref, *, seq, chunk):
    for c in range(seq // chunk):
        rows = slice(c * chunk, (c + 1) * chunk)
        ka, kb, vv = _kv_variants(k_ref[0, rows, :], v_ref[0, rows, :])
        for h in range(N_KV_HEADS):
            ka_ref[h, rows, :] = ka[h]
            kb_ref[h, rows, :] = kb[h]
            vv_ref[h, rows, :] = vv[h]

    g = g_ref[...]

    def tile(q0, ks, nq, nk):
        row = lax.broadcasted_iota(jnp.int32, (nq, nk), 0)
        col = lax.broadcasted_iota(jnp.int32, (nq, nk), 1)
        rel = row - col + (q0 - ks)
        mask = (rel >= 0) & (rel < WINDOW)
        low = lax.broadcasted_iota(jnp.int32, (nq, LANES), 1) < HEAD_DIM
        outs = []
        ss = jnp.zeros((nq, 1), F32)
        for p in range(N_PAIRS):
            h = p // PAIRS_PER_KV
            qp = q_ref[0, pl.ds(q0, nq), p * LANES:(p + 1) * LANES]
            vv = vv_ref[h, pl.ds(ks, nk), :]
            oa = _sink_softmax_pv(_dot_nt(qp, ka_ref[h, pl.ds(ks, nk), :]), mask,
                                  sink_ref[2 * p], vv)
            ob = _sink_softmax_pv(_dot_nt(qp, kb_ref[h, pl.ds(ks, nk), :]), mask,
                                  sink_ref[2 * p + 1], vv)
            op = jnp.where(low, oa, ob)
            ss = ss + jnp.sum(op * op, axis=1, keepdims=True)
            outs.append(op)
        inv = lax.rsqrt(ss * (1.0 / ATTN_W) + EPS)
        for p in range(N_PAIRS):
            cols = slice(p * LANES, (p + 1) * LANES)
            o_ref[0, pl.ds(q0, nq), cols] = (outs[p] * inv * g[:, cols]).astype(o_ref.dtype)

    n_full = seq // WINDOW
    tail = seq - n_full * WINDOW

    def body(i, carry):
        q0 = pl.multiple_of(i * WINDOW, WINDOW)
        ks = pl.multiple_of(jnp.maximum(q0 - WINDOW, 0), WINDOW)
        tile(q0, ks, WINDOW, 2 * WINDOW)
        return carry

    lax.fori_loop(0, n_full, body, 0)
    if tail:
        q0 = n_full * WINDOW
        tile(q0, q0 - WINDOW, tail, WINDOW + tail)


def _attn_prompt(sinks, q, k, v, g, chunk):
    b, seq, _ = q.shape
    kern = functools.partial(_attn_prompt_kernel, seq=seq, chunk=chunk)
    seq_spec = lambda w_: pl.BlockSpec((1, seq, w_), lambda i: (i, 0, 0))
    return pl.pallas_call(
        kern,
        grid=(b,),
        in_specs=[pl.BlockSpec(memory_space=pltpu.SMEM),
                  seq_spec(ATTN_W), seq_spec(KV_W), seq_spec(KV_W), _const_spec((1, ATTN_W))],
        out_specs=seq_spec(ATTN_W),
        out_shape=jax.ShapeDtypeStruct((b, seq, ATTN_W), BF16),
        scratch_shapes=[pltpu.VMEM((N_KV_HEADS, seq, LANES), BF16),
                        pltpu.VMEM((N_KV_HEADS, seq, LANES), BF16),
                        pltpu.VMEM((N_KV_HEADS, seq, LANES), BF16)],
        compiler_params=pltpu.CompilerParams(
            dimension_semantics=("arbitrary",), vmem_limit_bytes=VMEM_LIMIT),
        name="attn_prompt",
    )(sinks, q, k, v, g)


def _gelu_tanh(x):
    return 0.5 * x * (1.0 + jnp.tanh(0.7978845608028654 * (x + 0.044715 * (x * x * x))))


def _lru_gates(u, wab, bab, logsig):
    ga = _dot(u.astype(BF16), wab) + bab
    r = jax.nn.sigmoid(ga[:, :LRU_W])
    ig = jax.nn.sigmoid(ga[:, LRU_W:])
    log_a = LRU_C * r * logsig
    a = jnp.exp(log_a)
    b = jnp.sqrt(-jnp.tanh(log_a) * (a * a + 1.0)) * ig * u
    return a, b


def _lru_prompt_kernel(xb_ref, gate_ref, conv0_ref, h0_ref, cw_ref, cb_ref, wab_ref, bab_ref,
                       lam_ref, g_ref, o_ref, ht_ref, xpad_ref, a_ref, b_ref, *, seq, chunk):
    n_chunks = seq // chunk
    pad = SUBLANES
    xpad_ref[0:pad, :] = conv0_ref[0]
    for c in range(n_chunks):
        xpad_ref[pad + c * chunk:pad + (c + 1) * chunk, :] = xb_ref[0, c * chunk:(c + 1) * chunk, :]

    logsig = jax.nn.log_sigmoid(lam_ref[...])
    wab = wab_ref[...]
    bab = bab_ref[...]
    for c in range(n_chunks):
        r0 = c * chunk
        u = cb_ref[...]
        for j in range(CONV_W):
            start = pad + r0 - (CONV_W - 1) + j
            u = u + xpad_ref[start:start + chunk, :] * cw_ref[j:j + 1, :]
        a, b = _lru_gates(u, wab, bab, logsig)
        for s in range(N_SLABS):
            a_ref[s, r0:r0 + chunk, :] = a[:, s * LANES:(s + 1) * LANES]
            b_ref[s, r0:r0 + chunk, :] = b[:, s * LANES:(s + 1) * LANES]

    piece = seq // SUBLANES
    gather = lambda ref, s, i: ref[s, pl.ds(i, SUBLANES, stride=piece), :]

    def local_scan(i, carry):
        hs, ps = carry
        nh, npr = [], []
        for s in range(N_SLABS):
            a = gather(a_ref, s, i)
            h = a * hs[s] + gather(b_ref, s, i)
            p = a * ps[s]
            b_ref[s, pl.ds(i, SUBLANES, stride=piece), :] = h
            a_ref[s, pl.ds(i, SUBLANES, stride=piece), :] = p
            nh.append(h)
            npr.append(p)
        return tuple(nh), tuple(npr)

    zeros = tuple(jnp.zeros((SUBLANES, LANES), F32) for _ in range(N_SLABS))
    ones = tuple(jnp.ones((SUBLANES, LANES), F32) for _ in range(N_SLABS))
    h_end, p_end = lax.fori_loop(0, piece, local_scan, (zeros, ones))

    sub = lax.broadcasted_iota(jnp.int32, (SUBLANES, LANES), 0)
    h0 = h0_ref[0]
    carry_in, h_last = [], []
    for s in range(N_SLABS):
        cin = jnp.broadcast_to(h0[:, s * LANES:(s + 1) * LANES], (SUBLANES, LANES))
        acc = jnp.zeros((SUBLANES, LANES), F32)
        for c in range(SUBLANES):
            acc = jnp.where(sub == c, cin, acc)
            end = h_end[s] + p_end[s] * cin
            cin = jnp.broadcast_to(end[c:c + 1, :], (SUBLANES, LANES))
        carry_in.append(acc)
        h_last.append(cin[0:1, :])
    ht_ref[0] = jnp.concatenate(h_last, axis=1)

    def apply_carry(i, carry):
        for s in range(N_SLABS):
            h = gather(b_ref, s, i) + gather(a_ref, s, i) * carry_in[s]
            b_ref[s, pl.ds(i, SUBLANES, stride=piece), :] = h
        return carry

    lax.fori_loop(0, piece, apply_carry, 0)

    g = g_ref[...]
    for c in range(n_chunks):
        rows = slice(c * chunk, (c + 1) * chunk)
        hseq = jnp.concatenate([b_ref[s, rows, :] for s in range(N_SLABS)], axis=1)
        o = hseq * _gelu_tanh(gate_ref[0, rows, :])
        o_ref[0, rows, :] = _rms(o, g).astype(o_ref.dtype)


def _lru_prompt(xb, gate, conv0, h0, cw, cb, wab, bab, lam, g, chunk):
    b, seq, _ = xb.shape
    kern = functools.partial(_lru_prompt_kernel, seq=seq, chunk=chunk)
    seq_spec = pl.BlockSpec((1, seq, LRU_W), lambda i: (i, 0, 0))
    return pl.pallas_call(
        kern,
        grid=(b,),
        in_specs=[seq_spec, seq_spec,
                  pl.BlockSpec((1, SUBLANES, LRU_W), lambda i: (i, 0, 0)),
                  pl.BlockSpec((1, 1, LRU_W), lambda i: (i, 0, 0)),
                  _const_spec((CONV_W, LRU_W)), _const_spec((1, LRU_W)),
                  _const_spec((LRU_W, 2 * LRU_W)), _const_spec((1, 2 * LRU_W)),
                  _const_spec((1, LRU_W)), _const_spec((1, LRU_W))],
        out_specs=[seq_spec, pl.BlockSpec((1, 1, LRU_W), lambda i: (i, 0, 0))],
        out_shape=[jax.ShapeDtypeStruct((b, seq, LRU_W), BF16),
                   jax.ShapeDtypeStruct((b, 1, LRU_W), F32)],
        scratch_shapes=[pltpu.VMEM((seq + SUBLANES, LRU_W), F32),
                        pltpu.VMEM((N_SLABS, seq, LANES), F32),
                        pltpu.VMEM((N_SLABS, seq, LANES), F32)],
        compiler_params=pltpu.CompilerParams(
            dimension_semantics=("arbitrary",), vmem_limit_bytes=VMEM_LIMIT),
        name="lru_prompt",
    )(xb, gate, conv0, h0, cw, cb, wab, bab, lam, g)


def _attn_sample_kernel(sink_ref, q_ref, kn_ref, vn_ref, ck_ref, cv_ref, o_ref, *, group, n_new):
    nq = N_PAIRS * n_new
    new_pad = kn_ref.shape[1]
    nk = WINDOW + new_pad
    row = lax.broadcasted_iota(jnp.int32, (nq, nk), 0)
    col = lax.broadcasted_iota(jnp.int32, (nq, nk), 1)
    pair = sum((row >= p * n_new).astype(jnp.int32) for p in range(1, N_PAIRS))
    qpos = row - n_new * pair
    rel = jnp.where(col < WINDOW, qpos + WINDOW - col, qpos - (col - WINDOW))
    mask = (rel >= 0) & (rel < WINDOW) & (col < WINDOW + n_new)
    if PAST_LEN < WINDOW:
        mask = mask & (col >= WINDOW - PAST_LEN)
    kv0_rows = row[:, 0:1] < PAIRS_PER_KV * n_new
    low = lax.broadcasted_iota(jnp.int32, (nq, LANES), 1) < HEAD_DIM
    pair1 = pair[:, 0:1]

    def sink_col(first):
        out = jnp.zeros((nq, 1), F32)
        for p in range(N_PAIRS):
            out = jnp.where(pair1 == p, sink_ref[2 * p + first], out)
        return out

    sink_a, sink_b = sink_col(0), sink_col(1)

    def body(b, carry):
        kfull = jnp.concatenate([ck_ref[b], kn_ref[b]], axis=0)
        vfull = jnp.concatenate([cv_ref[b], vn_ref[b]], axis=0)
        ka, kb, vv = _kv_variants(kfull, vfull)
        qb = q_ref[b]
        halves = []
        for kvar, sink in ((ka, sink_a), (kb, sink_b)):
            s = jnp.where(kv0_rows, _dot_nt(qb, kvar[0]), _dot_nt(qb, kvar[1]))
            s = jnp.where(mask, s, -jnp.inf)
            mx = jnp.maximum(jnp.max(s, axis=1, keepdims=True), sink)
            p = jnp.exp(s - mx)
            den = jnp.sum(p, axis=1, keepdims=True) + jnp.exp(sink - mx)
            pb = p.astype(BF16)
            halves.append(jnp.where(kv0_rows, _dot(pb, vv[0]), _dot(pb, vv[1])) / den)
        o_ref[b] = jnp.where(low, halves[0], halves[1])
        return carry

    lax.fori_loop(0, group, body, 0)


def _attn_sample(sinks, q16, kn, vn, ck, cv, group, n_new):
    b = q16.shape[0]
    kern = functools.partial(_attn_sample_kernel, group=group, n_new=n_new)
    spec = lambda r, w_: pl.BlockSpec((group, r, w_), lambda i: (i, 0, 0))
    return pl.pallas_call(
        kern,
        grid=(b // group,),
        in_specs=[pl.BlockSpec(memory_space=pltpu.SMEM),
                  spec(q16.shape[1], LANES), spec(kn.shape[1], KV_W), spec(vn.shape[1], KV_W),
                  spec(WINDOW, KV_W), spec(WINDOW, KV_W)],
        out_specs=spec(q16.shape[1], LANES),
        out_shape=jax.ShapeDtypeStruct(q16.shape, F32),
        compiler_params=pltpu.CompilerParams(
            dimension_semantics=("arbitrary",), vmem_limit_bytes=VMEM_LIMIT),
        name="attn_sample",
    )(sinks, q16, kn, vn, ck, cv)


def _mix_sample_kernel(att_ref, xb_ref, gate_ref, conv0_ref, h0_ref, cw_ref, cb_ref, wab_ref,
                       bab_ref, lam_ref, ga_ref, gl_ref, att_o_ref, lru_o_ref, ht_ref, *, n_new):
    att_o_ref[...] = _rms(att_ref[...], ga_ref[...]).astype(att_o_ref.dtype)
    logsig = jax.nn.log_sigmoid(lam_ref[...])
    wab = wab_ref[...]
    bab = bab_ref[...]
    xs = [conv0_ref[j] for j in range(CONV_W - 1)] + [xb_ref[t] for t in range(n_new)]
    h = h0_ref[...]
    for t in range(n_new):
        u = cb_ref[...]
        for j in range(CONV_W):
            u = u + xs[t + j] * cw_ref[j:j + 1, :]
        a, b = _lru_gates(u, wab, bab, logsig)
        h = a * h + b
        o = h * _gelu_tanh(gate_ref[t])
        lru_o_ref[t] = _rms(o, gl_ref[...]).astype(lru_o_ref.dtype)
    ht_ref[...] = h


def _mix_sample(att, xb_t, gate_t, conv0_t, h0, cw, cb, wab, bab, lam, ga, gl):
    n_new, b, _ = xb_t.shape
    kern = functools.partial(_mix_sample_kernel, n_new=n_new)
    args = (att, xb_t, gate_t, conv0_t, h0, cw, cb, wab, bab, lam, ga, gl)
    return pl.pallas_call(
        kern,
        grid=(1,),
        in_specs=[_const_spec(a.shape) for a in args],
        out_specs=[_const_spec(att.shape), _const_spec(xb_t.shape), _const_spec(h0.shape)],
        out_shape=[jax.ShapeDtypeStruct(att.shape, BF16),
                   jax.ShapeDtypeStruct(xb_t.shape, BF16),
                   jax.ShapeDtypeStruct(h0.shape, F32)],
        compiler_params=pltpu.CompilerParams(
            dimension_semantics=("arbitrary",), vmem_limit_bytes=VMEM_LIMIT),
        name="mix_sample",
    )(*args)


def _rope_tables(pos):
    half = HEAD_DIM // 2
    inv = ROPE_THETA ** (-jnp.arange(half, dtype=F32) / half)
    ang = pos.astype(F32)[:, None] * inv[None, :]
    cos, sin = jnp.cos(ang), jnp.sin(ang)
    reps = LANES // HEAD_DIM
    return (jnp.concatenate([cos, cos] * reps, axis=1),
            jnp.concatenate([-sin, sin] * reps, axis=1))


def _block_diag(w):
    eye = jnp.eye(LRU_BLOCKS, dtype=w.dtype)
    full = w[:, :, None, :] * eye[:, None, :, None]
    return full.reshape(LRU_W, LRU_W)


def _pick_tile(n, candidates):
    for c in candidates:
        if n % c == 0:
            return c
    raise ValueError(f"no tile for {n} tokens")


def kernel(x_prompt, x_sample, cache_k, cache_v, state_h, state_conv, meta_tokens, pre_mix_norm,
           w_in, sinks, conv_w, conv_b, w_a, b_a, w_i, b_i, lam, attn_out_norm, lru_out_norm,
           w_out, post_mix_norm, pre_ffn_norm, w_gate, w_up, w_down, post_ffn_norm):
    bp, seq_in, _ = x_prompt.shape
    seq = seq_in + N_META
    bs, n_new, _ = x_sample.shape
    assert seq % SUBLANES == 0 and (seq % WINDOW) % 16 == 0 and n_new >= CONV_W - 1

    row = lambda a: a.reshape(1, -1)
    w_in_b = w_in.astype(BF16)
    w_out_b = w_out.astype(BF16)
    w_gate_b = w_gate.astype(BF16)
    w_up_b = w_up.astype(BF16)
    w_down_b = w_down.astype(BF16)
    wab = [jnp.concatenate([_block_diag(w_a[l]), _block_diag(w_i[l])], axis=1).astype(BF16)
           for l in range(DEPTH)]
    bab = [jnp.concatenate([b_a[l], b_i[l]]).reshape(1, -1) for l in range(DEPTH)]

    meta = jnp.broadcast_to(meta_tokens.astype(x_prompt.dtype)[None], (bp, N_META, D_MODEL))
    xp = jnp.concatenate([meta, x_prompt], axis=1).reshape(bp * seq, D_MODEL)
    xs = x_sample.reshape(bs * n_new, D_MODEL)

    cos_p, sin_p = _rope_tables(jnp.arange(seq, dtype=jnp.int32))
    pos_s = PAST_LEN + jnp.arange(n_new, dtype=jnp.int32)
    cos_s, sin_s = _rope_tables(jnp.tile(pos_s, bs))

    tm_p = _pick_tile(seq, (688, 512, 256, 128, 16))
    tm_s = bs * n_new
    tm_f = _pick_tile(bp * seq, (384, 256, 128, 16))
    conv0_p = jnp.zeros((bp, SUBLANES, LRU_W), F32)
    h0_p = jnp.zeros((bp, 1, LRU_W), F32)
    new_pad = 16
    group = 16

    pk, pv, ph, pc, sk, sv, sh, sc = [], [], [], [], [], [], [], []
    for l in range(DEPTH):
        lw = dict(cw=conv_w[l], cb=row(conv_b[l]), wab=wab[l], bab=bab[l], lam=row(lam[l]))

        q, k, v, xb, gate = _inproj(xp, row(pre_mix_norm[l]), w_in_b[l], cos_p, sin_p,
                                    tm_p, seq // tm_p)
        k3 = k.reshape(bp, seq, KV_W)
        v3 = v.reshape(bp, seq, KV_W)
        xb3 = xb.reshape(bp, seq, LRU_W)
        att = _attn_prompt(sinks[l], q.reshape(bp, seq, ATTN_W), k3, v3,
                           row(attn_out_norm[l]), tm_p)
        lru, ht = _lru_prompt(xb3, gate.reshape(bp, seq, LRU_W), conv0_p, h0_p,
                              lw["cw"], lw["cb"], lw["wab"], lw["bab"], lw["lam"],
                              row(lru_out_norm[l]), tm_p)
        xp = _ffn(xp, att.reshape(bp * seq, ATTN_W), lru.reshape(bp * seq, LRU_W), w_out_b[l],
                  row(post_mix_norm[l]), row(pre_ffn_norm[l]), w_gate_b[l], w_up_b[l],
                  w_down_b[l], row(post_ffn_norm[l]), tm_f)
        pk.append(k3[:, -WINDOW:].reshape(bp, WINDOW, N_KV_HEADS, HEAD_DIM))
        pv.append(v3[:, -WINDOW:].reshape(bp, WINDOW, N_KV_HEADS, HEAD_DIM))
        ph.append(ht.reshape(bp, LRU_W))
        pc.append(xb3[:, -(CONV_W - 1):])

        q, k, v, xb, gate = _inproj(xs, row(pre_mix_norm[l]), w_in_b[l], cos_s, sin_s, tm_s, 1)
        k3 = k.reshape(bs, n_new, KV_W)
        v3 = v.reshape(bs, n_new, KV_W)
        ck = cache_k[l].reshape(bs, WINDOW, KV_W)
        cv = cache_v[l].reshape(bs, WINDOW, KV_W)
        q16 = q.reshape(bs, n_new, N_PAIRS, LANES).transpose(0, 2, 1, 3).reshape(
            bs, N_PAIRS * n_new, LANES)
        padn = ((0, 0), (0, new_pad - n_new), (0, 0))
        o16 = _attn_sample(sinks[l], q16, jnp.pad(k3, padn), jnp.pad(v3, padn), ck, cv,
                           group, n_new)
        att_raw = o16.reshape(bs, N_PAIRS, n_new, LANES).transpose(0, 2, 1, 3).reshape(
            bs * n_new, ATTN_W)
        xb3 = xb.reshape(bs, n_new, LRU_W)
        tmaj = lambda a: a.transpose(1, 0, 2)
        att, lru_t, ht = _mix_sample(att_raw, tmaj(xb3), tmaj(gate.reshape(bs, n_new, LRU_W)),
                                     tmaj(state_conv[l]), state_h[l],
                                     lw["cw"], lw["cb"], lw["wab"], lw["bab"], lw["lam"],
                                     row(attn_out_norm[l]), row(lru_out_norm[l]))
        xs = _ffn(xs, att, tmaj(lru_t).reshape(bs * n_new, LRU_W), w_out_b[l],
                  row(post_mix_norm[l]), row(pre_ffn_norm[l]), w_gate_b[l], w_up_b[l],
                  w_down_b[l], row(post_ffn_norm[l]), tm_s)
        keep = WINDOW - n_new
        sk.append(jnp.concatenate([ck[:, -keep:], k3], axis=1).reshape(
            bs, WINDOW, N_KV_HEADS, HEAD_DIM))
        sv.append(jnp.concatenate([cv[:, -keep:], v3], axis=1).reshape(
            bs, WINDOW, N_KV_HEADS, HEAD_DIM))
        sh.append(ht)
        sc.append(jnp.concatenate([state_conv[l], xb3], axis=1)[:, -(CONV_W - 1):])

    y_prompt = xp.reshape(bp, seq, D_MODEL)[:, N_META:]
    y_sample = xs.reshape(bs, n_new, D_MODEL)
    return (y_prompt, y_sample, jnp.stack(pk), jnp.stack(pv), jnp.stack(ph), jnp.stack(pc),
            jnp.stack(sk), jnp.stack(sv), jnp.stack(sh), jnp.stack(sc))
```

```python
import functools

import jax
import jax.numpy as jnp
from jax import lax
from jax.experimental import pallas as pl
from jax.experimental.pallas import tpu as pltpu

D_MODEL = 1024
DEPTH = 4
PAST_LEN = 8192
N_META = 16
HEAD_DIM = 64
N_Q_HEADS = 8
N_KV_HEADS = 2
ATTN_W = N_Q_HEADS * HEAD_DIM
KV_W = N_KV_HEADS * HEAD_DIM
LRU_W = D_MODEL - ATTN_W
LRU_BLOCKS = 8
LRU_BLOCK_W = LRU_W // LRU_BLOCKS
CONV_W = 4
LRU_C = 8.0
WINDOW = 128
ROPE_THETA = 10000.0
D_FF = 2816
IN_W = ATTN_W + 2 * KV_W + 2 * LRU_W
EPS = 1e-6

LANES = 128
SUBLANES = 8
N_SLABS = LRU_W // LANES
N_PAIRS = ATTN_W // LANES
PAIRS_PER_KV = N_PAIRS // N_KV_HEADS
VMEM_LIMIT = 56 * 1024 * 1024

BF16 = jnp.bfloat16
F32 = jnp.float32


def _rms(x, g):
    ms = jnp.mean(x * x, axis=-1, keepdims=True)
    return x * lax.rsqrt(ms + EPS) * g


def _dot(a, b):
    return jnp.dot(a, b, preferred_element_type=F32)


def _dot_nt(a, b):
    return lax.dot_general(a, b, (((1,), (1,)), ((), ())), preferred_element_type=F32)


def _const_spec(shape):
    zeros = (0,) * len(shape)
    return pl.BlockSpec(shape, lambda *_: zeros)


def _layer_spec(arr, layer, single_buffer=False):
    tail = (0,) * (arr.ndim - 1)
    mode = dict(pipeline_mode=pl.Buffered(1)) if single_buffer else {}
    return pl.BlockSpec((None,) + arr.shape[1:], lambda *_: (layer,) + tail, **mode)


def _params(n_axes=1):
    return pltpu.CompilerParams(dimension_semantics=("arbitrary",) * n_axes,
                                vmem_limit_bytes=VMEM_LIMIT)


def _inproj_kernel(x_ref, g_ref, w_ref, cos_ref, sin_ref,
                   q_ref, k_ref, v_ref, xb_ref, gate_ref):
    hn = _rms(x_ref[...], g_ref[...]).astype(BF16)
    z = _dot(hn, w_ref[...])
    cos = cos_ref[...]
    sin = sin_ref[...]
    lane = lax.broadcasted_iota(jnp.int32, cos.shape, 1)
    first_half = (lane & (HEAD_DIM // 2)) == 0

    def rope(t):
        swapped = jnp.where(first_half,
                            pltpu.roll(t, LANES - HEAD_DIM // 2, 1),
                            pltpu.roll(t, HEAD_DIM // 2, 1))
        return t * cos + swapped * sin

    for p in range(N_PAIRS):
        qp = rope(z[:, p * LANES:(p + 1) * LANES]) * (HEAD_DIM ** -0.5)
        q_ref[:, p * LANES:(p + 1) * LANES] = qp.astype(q_ref.dtype)
    k_ref[...] = rope(z[:, ATTN_W:ATTN_W + KV_W])
    v_ref[...] = z[:, ATTN_W + KV_W:ATTN_W + 2 * KV_W]
    xb_ref[...] = z[:, ATTN_W + 2 * KV_W:ATTN_W + 2 * KV_W + LRU_W]
    gate_ref[...] = z[:, ATTN_W + 2 * KV_W + LRU_W:]


def _inproj(layer, x, g, w, cos, sin, tm, tiles_per_seq, q_dtype):
    t = x.shape[0]
    tok = lambda w_: pl.BlockSpec((tm, w_), lambda i: (i, 0))
    tab = pl.BlockSpec((tm, LANES), lambda i: (i % tiles_per_seq, 0))
    return pl.pallas_call(
        _inproj_kernel,
        grid=(t // tm,),
        in_specs=[tok(D_MODEL), _layer_spec(g, layer), _layer_spec(w, layer), tab, tab],
        out_specs=[tok(ATTN_W), tok(KV_W), tok(KV_W), tok(LRU_W), tok(LRU_W)],
        out_shape=[jax.ShapeDtypeStruct((t, ATTN_W), q_dtype),
                   jax.ShapeDtypeStruct((t, KV_W), F32),
                   jax.ShapeDtypeStruct((t, KV_W), F32),
                   jax.ShapeDtypeStruct((t, LRU_W), F32),
                   jax.ShapeDtypeStruct((t, LRU_W), F32)],
        compiler_params=_params(),
        name="inproj",
    )(x, g, w, cos, sin)


def _ffn_kernel(x_ref, att_ref, lru_ref, wo_ref, gpm_ref, gpf_ref,
                wg_ref, wu_ref, wd_ref, gpo_ref, o_ref):
    m = _dot(att_ref[...], wo_ref[0:ATTN_W, :]) + _dot(lru_ref[...], wo_ref[ATTN_W:, :])
    x1 = x_ref[...] + _rms(m, gpm_ref[...])
    hf = _rms(x1, gpf_ref[...]).astype(BF16)
    gt = _dot(hf, wg_ref[...])
    up = _dot(hf, wu_ref[...])
    hid = (gt * jax.nn.sigmoid(gt) * up).astype(BF16)
    f = _dot(hid, wd_ref[...])
    o_ref[...] = x1 + _rms(f, gpo_ref[...])


def _ffn(layer, x, att, lru, wo, gpm, gpf, wg, wu, wd, gpo, tm):
    t = x.shape[0]
    tok = lambda w_: pl.BlockSpec((tm, w_), lambda i: (i, 0))
    big = lambda a: _layer_spec(a, layer, single_buffer=True)
    return pl.pallas_call(
        _ffn_kernel,
        grid=(t // tm,),
        in_specs=[tok(D_MODEL), tok(ATTN_W), tok(LRU_W),
                  big(wo), _layer_spec(gpm, layer), _layer_spec(gpf, layer),
                  big(wg), big(wu), big(wd), _layer_spec(gpo, layer)],
        out_specs=tok(D_MODEL),
        out_shape=jax.ShapeDtypeStruct((t, D_MODEL), F32),
        compiler_params=_params(),
        name="outffn",
    )(x, att, lru, wo, gpm, gpf, wg, wu, wd, gpo)


def _kv_variants(k, v):
    lane = lax.broadcasted_iota(jnp.int32, k.shape, 1)
    low = lane < HEAD_DIM
    kr = pltpu.roll(k, HEAD_DIM, 1)
    vr = pltpu.roll(v, HEAD_DIM, 1)
    zero = jnp.zeros_like(k)
    ka = (jnp.where(low, k, zero), jnp.where(low, kr, zero))
    kb = (jnp.where(low, zero, kr), jnp.where(low, zero, k))
    vv = (jnp.where(low, v, vr), jnp.where(low, vr, v))
    cast = lambda pair: tuple(a.astype(BF16) for a in pair)
    return cast(ka), cast(kb), cast(vv)


def _sink_softmax_pv(s, mask, sink, vv):
    s = jnp.where(mask, s, -jnp.inf)
    mx = jnp.maximum(jnp.max(s, axis=1, keepdims=True), sink)
    p = jnp.exp(s - mx)
    den = jnp.sum(p, axis=1, keepdims=True) + jnp.exp(sink - mx)
    return _dot(p.astype(BF16), vv) / den


def _attn_prompt_kernel(sink_ref, q_ref, k_ref, v_ref, g_ref, o_ref,
                        ka_ref, kb_ref, vv_ref, *, layer, seq, chunk):
    for c in range(seq // chunk):
        rows = slice(c * chunk, (c + 1) * chunk)
        ka, kb, vv = _kv_variants(k_ref[0, rows, :], v_ref[0, rows, :])
        for h in range(N_KV_HEADS):
            ka_ref[h, rows, :] = ka[h]
            kb_ref[h, rows, :] = kb[h]
            vv_ref[h, rows, :] = vv[h]

    g = g_ref[...]

    def tile(q0, ks, nq, nk):
        row = lax.broadcasted_iota(jnp.int32, (nq, nk), 0)
        col = lax.broadcasted_iota(jnp.int32, (nq, nk), 1)
        rel = row - col + (q0 - ks)
        mask = (rel >= 0) & (rel < WINDOW)
        low = lax.broadcasted_iota(jnp.int32, (nq, LANES), 1) < HEAD_DIM
        outs = []
        ss = jnp.zeros((nq, 1), F32)
        for p in range(N_PAIRS):
            h = p // PAIRS_PER_KV
            qp = q_ref[0, pl.ds(q0, nq), p * LANES:(p + 1) * LANES]
            vv = vv_ref[h, pl.ds(ks, nk), :]
            oa = _sink_softmax_pv(_dot_nt(qp, ka_ref[h, pl.ds(ks, nk), :]), mask,
                                  sink_ref[layer,2 * p], vv)
            ob = _sink_softmax_pv(_dot_nt(qp, kb_ref[h, pl.ds(ks, nk), :]), mask,
                                  sink_ref[layer,2 * p + 1], vv)
            op = jnp.where(low, oa, ob)
            ss = ss + jnp.sum(op * op, axis=1, keepdims=True)
            outs.append(op)
        inv = lax.rsqrt(ss * (1.0 / ATTN_W) + EPS)
        for p in range(N_PAIRS):
            cols = slice(p * LANES, (p + 1) * LANES)
            o_ref[0, pl.ds(q0, nq), cols] = (outs[p] * inv * g[:, cols]).astype(o_ref.dtype)

    n_full = seq // WINDOW
    tail = seq - n_full * WINDOW

    def body(i, carry):
        q0 = pl.multiple_of(i * WINDOW, WINDOW)
        ks = pl.multiple_of(jnp.maximum(q0 - WINDOW, 0), WINDOW)
        tile(q0, ks, WINDOW, 2 * WINDOW)
        return carry

    lax.fori_loop(0, n_full, body, 0)
    if tail:
        q0 = n_full * WINDOW
        tile(q0, q0 - WINDOW, tail, WINDOW + tail)


def _attn_prompt(layer, sinks, q, k, v, g, chunk):
    b, seq, _ = q.shape
    kern = functools.partial(_attn_prompt_kernel, layer=layer, seq=seq, chunk=chunk)
    seq_spec = lambda w_: pl.BlockSpec((1, seq, w_), lambda i: (i, 0, 0))
    return pl.pallas_call(
        kern,
        grid=(b,),
        in_specs=[pl.BlockSpec(memory_space=pltpu.SMEM),
                  seq_spec(ATTN_W), seq_spec(KV_W), seq_spec(KV_W), _layer_spec(g, layer)],
        out_specs=seq_spec(ATTN_W),
        out_shape=jax.ShapeDtypeStruct((b, seq, ATTN_W), BF16),
        scratch_shapes=[pltpu.VMEM((N_KV_HEADS, seq, LANES), BF16),
                        pltpu.VMEM((N_KV_HEADS, seq, LANES), BF16),
                        pltpu.VMEM((N_KV_HEADS, seq, LANES), BF16)],
        compiler_params=_params(),
        name="attn_prompt",
    )(sinks, q, k, v, g)


def _gelu_tanh(x):
    return 0.5 * x * (1.0 + jnp.tanh(0.7978845608028654 * (x + 0.044715 * (x * x * x))))


def _lru_gates(u, wab, bab, logsig):
    ga = _dot(u.astype(BF16), wab) + bab
    r = jax.nn.sigmoid(ga[:, :LRU_W])
    ig = jax.nn.sigmoid(ga[:, LRU_W:])
    log_a = LRU_C * r * logsig
    a = jnp.exp(log_a)
    b = jnp.sqrt(-jnp.tanh(log_a) * (a * a + 1.0)) * ig * u
    return a, b


def _lru_prompt_kernel(xb_ref, gate_ref, conv0_ref, h0_ref, cw_ref, cb_ref, wab_ref, bab_ref,
                       lam_ref, g_ref, o_ref, ht_ref, xpad_ref, a_ref, b_ref, *, seq, chunk):
    n_chunks = seq // chunk
    pad = SUBLANES
    xpad_ref[0:pad, :] = conv0_ref[0]
    for c in range(n_chunks):
        xpad_ref[pad + c * chunk:pad + (c + 1) * chunk, :] = xb_ref[0, c * chunk:(c + 1) * chunk, :]

    logsig = jax.nn.log_sigmoid(lam_ref[...])
    wab = wab_ref[...]
    bab = bab_ref[...]
    for c in range(n_chunks):
        r0 = c * chunk
        u = cb_ref[...]
        for j in range(CONV_W):
            start = pad + r0 - (CONV_W - 1) + j
            u = u + xpad_ref[start:start + chunk, :] * cw_ref[j:j + 1, :]
        a, b = _lru_gates(u, wab, bab, logsig)
        for s in range(N_SLABS):
            a_ref[s, r0:r0 + chunk, :] = a[:, s * LANES:(s + 1) * LANES]
            b_ref[s, r0:r0 + chunk, :] = b[:, s * LANES:(s + 1) * LANES]

    piece = seq // SUBLANES
    gather = lambda ref, s, i: ref[s, pl.ds(i, SUBLANES, stride=piece), :]

    def local_scan(i, carry):
        hs, ps = carry
        nh, npr = [], []
        for s in range(N_SLABS):
            a = gather(a_ref, s, i)
            h = a * hs[s] + gather(b_ref, s, i)
            p = a * ps[s]
            b_ref[s, pl.ds(i, SUBLANES, stride=piece), :] = h
            a_ref[s, pl.ds(i, SUBLANES, stride=piece), :] = p
            nh.append(h)
            npr.append(p)
        return tuple(nh), tuple(npr)

    zeros = tuple(jnp.zeros((SUBLANES, LANES), F32) for _ in range(N_SLABS))
    ones = tuple(jnp.ones((SUBLANES, LANES), F32) for _ in range(N_SLABS))
    h_end, p_end = lax.fori_loop(0, piece, local_scan, (zeros, ones))

    sub = lax.broadcasted_iota(jnp.int32, (SUBLANES, LANES), 0)
    h0 = h0_ref[0]
    carry_in, h_last = [], []
    for s in range(N_SLABS):
        cin = jnp.broadcast_to(h0[:, s * LANES:(s + 1) * LANES], (SUBLANES, LANES))
        acc = jnp.zeros((SUBLANES, LANES), F32)
        for c in range(SUBLANES):
            acc = jnp.where(sub == c, cin, acc)
            end = h_end[s] + p_end[s] * cin
            cin = jnp.broadcast_to(end[c:c + 1, :], (SUBLANES, LANES))
        carry_in.append(acc)
        h_last.append(cin[0:1, :])
    ht_ref[0] = jnp.concatenate(h_last, axis=1)

    def apply_carry(i, carry):
        for s in range(N_SLABS):
            h = gather(b_ref, s, i) + gather(a_ref, s, i) * carry_in[s]
            b_ref[s, pl.ds(i, SUBLANES, stride=piece), :] = h
        return carry

    lax.fori_loop(0, piece, apply_carry, 0)

    g = g_ref[...]
    for c in range(n_chunks):
        rows = slice(c * chunk, (c + 1) * chunk)
        hseq = jnp.concatenate([b_ref[s, rows, :] for s in range(N_SLABS)], axis=1)
        o = hseq * _gelu_tanh(gate_ref[0, rows, :])
        o_ref[0, rows, :] = _rms(o, g).astype(o_ref.dtype)


def _lru_prompt(layer, xb, gate, conv0, h0, cw, cb, wab, bab, lam, g, chunk):
    b, seq, _ = xb.shape
    kern = functools.partial(_lru_prompt_kernel, seq=seq, chunk=chunk)
    seq_spec = pl.BlockSpec((1, seq, LRU_W), lambda i: (i, 0, 0))
    lyr = lambda a: _layer_spec(a, layer)
    return pl.pallas_call(
        kern,
        grid=(b,),
        in_specs=[seq_spec, seq_spec,
                  pl.BlockSpec((1, SUBLANES, LRU_W), lambda i: (i, 0, 0)),
                  pl.BlockSpec((1, 1, LRU_W), lambda i: (i, 0, 0)),
                  lyr(cw), lyr(cb), lyr(wab), lyr(bab), lyr(lam), lyr(g)],
        out_specs=[seq_spec, pl.BlockSpec((1, 1, LRU_W), lambda i: (i, 0, 0))],
        out_shape=[jax.ShapeDtypeStruct((b, seq, LRU_W), BF16),
                   jax.ShapeDtypeStruct((b, 1, LRU_W), F32)],
        scratch_shapes=[pltpu.VMEM((seq + SUBLANES, LRU_W), F32),
                        pltpu.VMEM((N_SLABS, seq, LANES), F32),
                        pltpu.VMEM((N_SLABS, seq, LANES), F32)],
        compiler_params=_params(),
        name="lru_prompt",
    )(xb, gate, conv0, h0, cw, cb, wab, bab, lam, g)


def _div_pow2(x, n):
    assert n & (n - 1) == 0
    return x >> (n.bit_length() - 1)


def _mod_pow2(x, n):
    assert n & (n - 1) == 0
    return x & (n - 1)


def _attn_sample_kernel(sink_ref, q_ref, kn_ref, vn_ref, ck_ref, cv_ref,
                        o_ref, nk_ref, nv_ref, *, layer, group, n_new):
    rpb = PAIRS_PER_KV * n_new
    new_pad = kn_ref.shape[1]
    rows = group * rpb
    ncols = group * new_pad
    row = lax.broadcasted_iota(jnp.int32, (rows, 1), 0)
    qpos = _mod_pow2(row, n_new)
    row_seq = _div_pow2(row, rpb)
    second_pair = _mod_pow2(_div_pow2(row, n_new), PAIRS_PER_KV) == 1
    col = lax.broadcasted_iota(jnp.int32, (1, WINDOW), 1)
    mask_c = col > qpos
    if PAST_LEN < WINDOW:
        mask_c = mask_c & (col >= WINDOW - PAST_LEN)
    ncol = lax.broadcasted_iota(jnp.int32, (1, ncols), 1)
    mask_n = (_div_pow2(ncol, new_pad) == row_seq) & (_mod_pow2(ncol, new_pad) <= qpos)
    low = lax.broadcasted_iota(jnp.int32, (rpb, LANES), 1) < HEAD_DIM
    zero_half = jnp.zeros((HEAD_DIM, LANES), BF16)

    kn_all = kn_ref[...].reshape(ncols, KV_W)
    vn_all = vn_ref[...].reshape(ncols, KV_W)
    knt_all = kn_all.T
    vnt_all = vn_all.T
    vn_lane = lax.broadcasted_iota(jnp.int32, vn_all.shape, 1) < HEAD_DIM
    vn_roll = pltpu.roll(vn_all, HEAD_DIM, 1)

    keep = WINDOW - n_new
    cache_lane = lax.broadcasted_iota(jnp.int32, (HEAD_DIM, WINDOW), 1) < keep

    for h in range(N_KV_HEADS):
        hd = slice(h * HEAD_DIM, (h + 1) * HEAD_DIM)
        q_all = q_ref[:, h].reshape(rows, LANES).astype(BF16)
        q_seq = [q_ref[b, h].astype(BF16) for b in range(group)]
        knt_h = knt_all[hd, :].astype(BF16)
        vvn = (jnp.where(vn_lane, vn_all, vn_roll) if h == 0
               else jnp.where(vn_lane, vn_roll, vn_all)).astype(BF16)
        kt, vt2 = [], []
        for b in range(group):
            kt.append(ck_ref[b, h].astype(BF16))
            vt = cv_ref[b, h].astype(BF16)
            vt2.append(jnp.concatenate([vt, vt], axis=0))
        probs, dens = [], []
        for first in (True, False):
            embed = lambda t: jnp.concatenate([t, zero_half] if first else [zero_half, t], axis=0)
            s_new = jnp.where(mask_n, _dot(q_all, embed(knt_h)), -jnp.inf)
            s_c = jnp.concatenate(
                [_dot(q_seq[b], embed(kt[b])) for b in range(group)], axis=0)
            s_c = jnp.where(mask_c, s_c, -jnp.inf)
            pair0 = 2 * h * PAIRS_PER_KV + (0 if first else 1)
            sink = jnp.where(second_pair, sink_ref[layer,pair0 + 2], sink_ref[layer,pair0])
            mx = jnp.maximum(jnp.maximum(jnp.max(s_c, axis=1, keepdims=True),
                                         jnp.max(s_new, axis=1, keepdims=True)), sink)
            p_c = jnp.exp(s_c - mx)
            p_n = jnp.exp(s_new - mx)
            dens.append(jnp.sum(p_c, axis=1, keepdims=True) + jnp.sum(p_n, axis=1, keepdims=True)
                        + jnp.exp(sink - mx))
            probs.append((p_c, _dot(p_n.astype(BF16), vvn)))
        for b in range(group):
            rs = slice(b * rpb, (b + 1) * rpb)
            pc = jnp.concatenate([probs[0][0][rs], probs[1][0][rs]], axis=0).astype(BF16)
            oc = _dot_nt(pc, vt2[b])
            o_first = (oc[:rpb] + probs[0][1][rs]) / dens[0][rs]
            o_second = (oc[rpb:] + probs[1][1][rs]) / dens[1][rs]
            o_ref[b, h] = jnp.where(low, o_first, o_second)
            shift = (keep - b * new_pad) % LANES
            nk_ref[b, h] = jnp.where(cache_lane, pltpu.roll(ck_ref[b, h], keep, 1),
                                     pltpu.roll(knt_all[hd, :], shift, 1))
            nv_ref[b, h] = jnp.where(cache_lane, pltpu.roll(cv_ref[b, h], keep, 1),
                                     pltpu.roll(vnt_all[hd, :], shift, 1))


def _attn_sample(layer, sinks, q, kn, vn, ck, cv, group, n_new):
    bs = q.shape[0]
    kern = functools.partial(_attn_sample_kernel, layer=layer, group=group, n_new=n_new)
    seq4 = lambda a: pl.BlockSpec((group,) + a.shape[1:], lambda i: (i, 0, 0, 0))
    seq3 = lambda a: pl.BlockSpec((group,) + a.shape[1:], lambda i: (i, 0, 0))
    cache = pl.BlockSpec((None, group) + ck.shape[2:], lambda i: (layer, i, 0, 0, 0))
    cache_out = pl.BlockSpec((group,) + ck.shape[2:], lambda i: (i, 0, 0, 0))
    return pl.pallas_call(
        kern,
        grid=(bs // group,),
        in_specs=[pl.BlockSpec(memory_space=pltpu.SMEM),
                  seq4(q), seq3(kn), seq3(vn), cache, cache],
        out_specs=[seq4(q), cache_out, cache_out],
        out_shape=[jax.ShapeDtypeStruct(q.shape, F32),
                   jax.ShapeDtypeStruct(ck.shape[1:], F32),
                   jax.ShapeDtypeStruct(cv.shape[1:], F32)],
        compiler_params=_params(),
        name="attn_sample",
    )(sinks, q, kn, vn, ck, cv)


def _mix_sample_kernel(att_ref, xb_ref, gate_ref, conv0_ref, h0_ref, cw_ref, cb_ref, wab_ref,
                       bab_ref, lam_ref, ga_ref, gl_ref, att_o_ref, lru_o_ref, ht_ref, *, n_new):
    att_o_ref[...] = _rms(att_ref[...], ga_ref[...]).astype(att_o_ref.dtype)
    logsig = jax.nn.log_sigmoid(lam_ref[...])
    wab = wab_ref[...]
    bab = bab_ref[...]
    xs = [conv0_ref[j] for j in range(CONV_W - 1)] + [xb_ref[t] for t in range(n_new)]
    h = h0_ref[...]
    for t in range(n_new):
        u = cb_ref[...]
        for j in range(CONV_W):
            u = u + xs[t + j] * cw_ref[j:j + 1, :]
        a, b = _lru_gates(u, wab, bab, logsig)
        h = a * h + b
        o = h * _gelu_tanh(gate_ref[t])
        lru_o_ref[t] = _rms(o, gl_ref[...]).astype(lru_o_ref.dtype)
    ht_ref[...] = h


def _mix_sample(layer, att, xb_t, gate_t, conv0_t, h0, cw, cb, wab, bab, lam, ga, gl):
    n_new = xb_t.shape[0]
    kern = functools.partial(_mix_sample_kernel, n_new=n_new)
    lyr = lambda a: _layer_spec(a, layer)
    return pl.pallas_call(
        kern,
        grid=(1,),
        in_specs=[_const_spec(att.shape), _const_spec(xb_t.shape), _const_spec(gate_t.shape),
                  lyr(conv0_t), lyr(h0), lyr(cw), lyr(cb), lyr(wab), lyr(bab), lyr(lam),
                  lyr(ga), lyr(gl)],
        out_specs=[_const_spec(att.shape), _const_spec(xb_t.shape), _const_spec(h0.shape[1:])],
        out_shape=[jax.ShapeDtypeStruct(att.shape, BF16),
                   jax.ShapeDtypeStruct(xb_t.shape, BF16),
                   jax.ShapeDtypeStruct(h0.shape[1:], F32)],
        compiler_params=_params(),
        name="mix_sample",
    )(att, xb_t, gate_t, conv0_t, h0, cw, cb, wab, bab, lam, ga, gl)


def _rope_tables(pos):
    half = HEAD_DIM // 2
    inv = ROPE_THETA ** (-jnp.arange(half, dtype=F32) / half)
    ang = pos.astype(F32)[:, None] * inv[None, :]
    cos, sin = jnp.cos(ang), jnp.sin(ang)
    reps = LANES // HEAD_DIM
    return (jnp.concatenate([cos, cos] * reps, axis=1),
            jnp.concatenate([-sin, sin] * reps, axis=1))


def _block_diag(w):
    eye = jnp.eye(LRU_BLOCKS, dtype=w.dtype)
    full = w[:, :, :, None, :] * eye[None, :, None, :, None]
    return full.reshape(w.shape[0], LRU_W, LRU_W)


def _pick_tile(n, candidates):
    for c in candidates:
        if n % c == 0:
            return c
    raise ValueError(f"no tile for {n} tokens")


def kernel(x_prompt, x_sample, cache_k, cache_v, state_h, state_conv, meta_tokens, pre_mix_norm,
           w_in, sinks, conv_w, conv_b, w_a, b_a, w_i, b_i, lam, attn_out_norm, lru_out_norm,
           w_out, post_mix_norm, pre_ffn_norm, w_gate, w_up, w_down, post_ffn_norm):
    bp, seq_in, _ = x_prompt.shape
    seq = seq_in + N_META
    bs, n_new, _ = x_sample.shape
    assert seq % SUBLANES == 0 and (seq % WINDOW) % 16 == 0 and n_new >= CONV_W - 1

    vec = lambda a: a.reshape(DEPTH, 1, -1)
    w_in_b = w_in.astype(BF16)
    w_out_b = w_out.astype(BF16)
    w_gate_b = w_gate.astype(BF16)
    w_up_b = w_up.astype(BF16)
    w_down_b = w_down.astype(BF16)
    wab = jnp.concatenate([_block_diag(w_a), _block_diag(w_i)], axis=2).astype(BF16)
    bab = vec(jnp.concatenate([b_a, b_i], axis=1))
    g_pre, g_att, g_lru = vec(pre_mix_norm), vec(attn_out_norm), vec(lru_out_norm)
    g_pm, g_pf, g_po = vec(post_mix_norm), vec(pre_ffn_norm), vec(post_ffn_norm)
    cb, lam_v = vec(conv_b), vec(lam)

    meta = jnp.broadcast_to(meta_tokens.astype(x_prompt.dtype)[None], (bp, N_META, D_MODEL))
    xp = jnp.concatenate([meta, x_prompt], axis=1).reshape(bp * seq, D_MODEL)
    xs = x_sample.reshape(bs * n_new, D_MODEL)

    cos_p, sin_p = _rope_tables(jnp.arange(seq, dtype=jnp.int32))
    pos_s = PAST_LEN + jnp.arange(n_new, dtype=jnp.int32)
    cos_s, sin_s = _rope_tables(jnp.tile(pos_s, bs))

    tm_p = _pick_tile(seq, (688, 512, 256, 128, 16))
    tm_s = bs * n_new
    tm_f = _pick_tile(bp * seq, (384, 256, 128, 16))
    conv0_p = jnp.zeros((bp, SUBLANES, LRU_W), F32)
    h0_p = jnp.zeros((bp, 1, LRU_W), F32)
    new_pad = SUBLANES
    group = LANES // new_pad
    assert n_new <= new_pad and bs % group == 0 and WINDOW == LANES

    ck_t = cache_k.transpose(0, 1, 3, 4, 2)
    cv_t = cache_v.transpose(0, 1, 3, 4, 2)
    conv_s = state_conv.transpose(0, 2, 1, 3)

    pk, pv, ph, pc, sk, sv, sh, sc = [], [], [], [], [], [], [], []
    for l in range(DEPTH):
        q, k, v, xb, gate = _inproj(l, xp, g_pre, w_in_b, cos_p, sin_p, tm_p, seq // tm_p, BF16)
        k3 = k.reshape(bp, seq, KV_W)
        v3 = v.reshape(bp, seq, KV_W)
        xb3 = xb.reshape(bp, seq, LRU_W)
        att = _attn_prompt(l, sinks, q.reshape(bp, seq, ATTN_W), k3, v3, g_att, tm_p)
        lru, ht = _lru_prompt(l, xb3, gate.reshape(bp, seq, LRU_W), conv0_p, h0_p,
                              conv_w, cb, wab, bab, lam_v, g_lru, tm_p)
        xp = _ffn(l, xp, att.reshape(bp * seq, ATTN_W), lru.reshape(bp * seq, LRU_W), w_out_b,
                  g_pm, g_pf, w_gate_b, w_up_b, w_down_b, g_po, tm_f)
        pk.append(k3[:, -WINDOW:].reshape(bp, WINDOW, N_KV_HEADS, HEAD_DIM))
        pv.append(v3[:, -WINDOW:].reshape(bp, WINDOW, N_KV_HEADS, HEAD_DIM))
        ph.append(ht.reshape(bp, LRU_W))
        pc.append(xb3[:, -(CONV_W - 1):])

        q, k, v, xb, gate = _inproj(l, xs, g_pre, w_in_b, cos_s, sin_s, tm_s, 1, F32)
        q4 = q.reshape(bs, n_new, N_KV_HEADS, PAIRS_PER_KV, LANES).transpose(0, 2, 3, 1, 4).reshape(
            bs, N_KV_HEADS, PAIRS_PER_KV * n_new, LANES)
        padn = ((0, 0), (0, new_pad - n_new), (0, 0))
        o4, nk, nv = _attn_sample(l, sinks, q4,
                                  jnp.pad(k.reshape(bs, n_new, KV_W), padn),
                                  jnp.pad(v.reshape(bs, n_new, KV_W), padn),
                                  ck_t, cv_t, group, n_new)
        att_raw = o4.reshape(bs, N_KV_HEADS, PAIRS_PER_KV, n_new, LANES).transpose(
            0, 3, 1, 2, 4).reshape(bs * n_new, ATTN_W)
        xb3 = xb.reshape(bs, n_new, LRU_W)
        tmaj = lambda a: a.transpose(1, 0, 2)
        att, lru_t, ht = _mix_sample(l, att_raw, tmaj(xb3), tmaj(gate.reshape(bs, n_new, LRU_W)),
                                     conv_s, state_h, conv_w, cb, wab, bab, lam_v, g_att, g_lru)
        xs = _ffn(l, xs, att, tmaj(lru_t).reshape(bs * n_new, LRU_W), w_out_b,
                  g_pm, g_pf, w_gate_b, w_up_b, w_down_b, g_po, tm_s)
        sk.append(nk.transpose(0, 3, 1, 2))
        sv.append(nv.transpose(0, 3, 1, 2))
        sh.append(ht)
        sc.append(jnp.concatenate([state_conv[l], xb3], axis=1)[:, -(CONV_W - 1):])

    y_prompt = xp.reshape(bp, seq, D_MODEL)[:, N_META:]
    y_sample = xs.reshape(bs, n_new, D_MODEL)
    return (y_prompt, y_sample, jnp.stack(pk), jnp.stack(pv), jnp.stack(ph), jnp.stack(pc),
            jnp.stack(sk), jnp.stack(sv), jnp.stack(sh), jnp.stack(sc))
```

```python
import functools

import jax
import jax.numpy as jnp
from jax import lax
from jax.experimental import pallas as pl
from jax.experimental.pallas import tpu as pltpu

D_MODEL = 1024
DEPTH = 4
PAST_LEN = 8192
N_META = 16
HEAD_DIM = 64
N_Q_HEADS = 8
N_KV_HEADS = 2
ATTN_W = N_Q_HEADS * HEAD_DIM
KV_W = N_KV_HEADS * HEAD_DIM
LRU_W = D_MODEL - ATTN_W
LRU_BLOCKS = 8
LRU_BLOCK_W = LRU_W // LRU_BLOCKS
CONV_W = 4
LRU_C = 8.0
WINDOW = 128
ROPE_THETA = 10000.0
D_FF = 2816
IN_W = ATTN_W + 2 * KV_W + 2 * LRU_W
EPS = 1e-6

LANES = 128
SUBLANES = 8
N_SLABS = LRU_W // LANES
N_PAIRS = ATTN_W // LANES
PAIRS_PER_KV = N_PAIRS // N_KV_HEADS
VMEM_LIMIT = 56 * 1024 * 1024

BF16 = jnp.bfloat16
F32 = jnp.float32


def _rms(x, g):
    ms = jnp.mean(x * x, axis=-1, keepdims=True)
    return x * lax.rsqrt(ms + EPS) * g


def _dot(a, b):
    return jnp.dot(a, b, preferred_element_type=F32)


def _dot_nt(a, b):
    return lax.dot_general(a, b, (((1,), (1,)), ((), ())), preferred_element_type=F32)


def _const_spec(shape):
    zeros = (0,) * len(shape)
    return pl.BlockSpec(shape, lambda *_: zeros)


def _layer_spec(arr, layer, single_buffer=False):
    tail = (0,) * (arr.ndim - 1)
    mode = dict(pipeline_mode=pl.Buffered(1)) if single_buffer else {}
    return pl.BlockSpec((None,) + arr.shape[1:], lambda *_: (layer,) + tail, **mode)


def _params(n_axes=1):
    return pltpu.CompilerParams(dimension_semantics=("arbitrary",) * n_axes,
                                vmem_limit_bytes=VMEM_LIMIT)


def _inproj_kernel(x_ref, g_ref, w_ref, cos_ref, sin_ref,
                   q_ref, k_ref, v_ref, xb_ref, gate_ref):
    hn = _rms(x_ref[...], g_ref[...]).astype(BF16)
    z = _dot(hn, w_ref[...])
    cos = cos_ref[...]
    sin = sin_ref[...]
    lane = lax.broadcasted_iota(jnp.int32, cos.shape, 1)
    first_half = (lane & (HEAD_DIM // 2)) == 0

    def rope(t):
        swapped = jnp.where(first_half,
                            pltpu.roll(t, LANES - HEAD_DIM // 2, 1),
                            pltpu.roll(t, HEAD_DIM // 2, 1))
        return t * cos + swapped * sin

    for p in range(N_PAIRS):
        qp = rope(z[:, p * LANES:(p + 1) * LANES]) * (HEAD_DIM ** -0.5)
        q_ref[:, p * LANES:(p + 1) * LANES] = qp.astype(q_ref.dtype)
    k_ref[...] = rope(z[:, ATTN_W:ATTN_W + KV_W])
    v_ref[...] = z[:, ATTN_W + KV_W:ATTN_W + 2 * KV_W]
    xb_ref[...] = z[:, ATTN_W + 2 * KV_W:ATTN_W + 2 * KV_W + LRU_W]
    gate_ref[...] = z[:, ATTN_W + 2 * KV_W + LRU_W:]


def _inproj(layer, x, g, w, cos, sin, tm, tiles_per_seq, q_dtype):
    t = x.shape[0]
    tok = lambda w_: pl.BlockSpec((tm, w_), lambda i: (i, 0))
    tab = pl.BlockSpec((tm, LANES), lambda i: (i % tiles_per_seq, 0))
    return pl.pallas_call(
        _inproj_kernel,
        grid=(t // tm,),
        in_specs=[tok(D_MODEL), _layer_spec(g, layer), _layer_spec(w, layer), tab, tab],
        out_specs=[tok(ATTN_W), tok(KV_W), tok(KV_W), tok(LRU_W), tok(LRU_W)],
        out_shape=[jax.ShapeDtypeStruct((t, ATTN_W), q_dtype),
                   jax.ShapeDtypeStruct((t, KV_W), F32),
                   jax.ShapeDtypeStruct((t, KV_W), F32),
                   jax.ShapeDtypeStruct((t, LRU_W), F32),
                   jax.ShapeDtypeStruct((t, LRU_W), F32)],
        compiler_params=_params(),
        name="inproj",
    )(x, g, w, cos, sin)


def _ffn_kernel(x_ref, att_ref, lru_ref, ga_ref, gl_ref, wo_ref, gpm_ref, gpf_ref,
                wg_ref, wu_ref, wd_ref, gpo_ref, o_ref):
    att = _rms(att_ref[...], ga_ref[...]).astype(BF16)
    lru = _rms(lru_ref[...], gl_ref[...]).astype(BF16)
    m = _dot(att, wo_ref[0:ATTN_W, :]) + _dot(lru, wo_ref[ATTN_W:, :])
    x1 = x_ref[...] + _rms(m, gpm_ref[...])
    hf = _rms(x1, gpf_ref[...]).astype(BF16)
    gt = _dot(hf, wg_ref[...])
    up = _dot(hf, wu_ref[...])
    hid = (gt * jax.nn.sigmoid(gt) * up).astype(BF16)
    f = _dot(hid, wd_ref[...])
    o_ref[...] = x1 + _rms(f, gpo_ref[...])


def _ffn(layer, x, att, lru, ga, gl, wo, gpm, gpf, wg, wu, wd, gpo, tm):
    t = x.shape[0]
    tok = lambda w_: pl.BlockSpec((tm, w_), lambda i: (i, 0))
    big = lambda a: _layer_spec(a, layer, single_buffer=True)
    lyr = lambda a: _layer_spec(a, layer)
    return pl.pallas_call(
        _ffn_kernel,
        grid=(t // tm,),
        in_specs=[tok(D_MODEL), tok(ATTN_W), tok(LRU_W), lyr(ga), lyr(gl),
                  big(wo), lyr(gpm), lyr(gpf), big(wg), big(wu), big(wd), lyr(gpo)],
        out_specs=tok(D_MODEL),
        out_shape=jax.ShapeDtypeStruct((t, D_MODEL), F32),
        compiler_params=_params(),
        name="outffn",
    )(x, att, lru, ga, gl, wo, gpm, gpf, wg, wu, wd, gpo)


def _kv_variants(k, v):
    lane = lax.broadcasted_iota(jnp.int32, k.shape, 1)
    low = lane < HEAD_DIM
    kr = pltpu.roll(k, HEAD_DIM, 1)
    vr = pltpu.roll(v, HEAD_DIM, 1)
    zero = jnp.zeros_like(k)
    ka = (jnp.where(low, k, zero), jnp.where(low, kr, zero))
    kb = (jnp.where(low, zero, kr), jnp.where(low, zero, k))
    vv = (jnp.where(low, v, vr), jnp.where(low, vr, v))
    cast = lambda pair: tuple(a.astype(BF16) for a in pair)
    return cast(ka), cast(kb), cast(vv)


def _attn_prompt_kernel(sink_ref, q_ref, k_ref, v_ref, o_ref,
                        ka_ref, kb_ref, vv_ref, *, layer, seq, chunk):
    for c in range(seq // chunk):
        rows = slice(c * chunk, (c + 1) * chunk)
        ka, kb, vv = _kv_variants(k_ref[0, rows, :], v_ref[0, rows, :])
        ones = jnp.ones((chunk, LANES), BF16)
        for h in range(N_KV_HEADS):
            ka_ref[h, rows, :] = ka[h]
            kb_ref[h, rows, :] = kb[h]
            vv_ref[h, rows, :] = jnp.concatenate([vv[h], ones], axis=1)

    def tile(q0, ks, nq, nk, sink_col):
        row = lax.broadcasted_iota(jnp.int32, (nq, nk), 0)
        col = lax.broadcasted_iota(jnp.int32, (nq, nk), 1)
        rel = row - col + (q0 - ks)
        mask = (rel >= 0) & (rel < WINDOW)
        col_row = lax.broadcasted_iota(jnp.int32, (1, nk), 1)
        key_row = lax.broadcasted_iota(jnp.int32, (nk, 2 * LANES), 0)
        val_lane = lax.broadcasted_iota(jnp.int32, (nk, 2 * LANES), 1)
        zero_v = (key_row == sink_col) & (val_lane < LANES)
        vvw = [jnp.where(zero_v, jnp.zeros((), BF16), vv_ref[h, pl.ds(ks, nk), :])
               for h in range(N_KV_HEADS)]
        low = lax.broadcasted_iota(jnp.int32, (nq, LANES), 1) < HEAD_DIM
        for p in range(N_PAIRS):
            h = p // PAIRS_PER_KV
            qp = q_ref[0, pl.ds(q0, nq), p * LANES:(p + 1) * LANES]
            halves = []
            for k_ref_, head in ((ka_ref, 2 * p), (kb_ref, 2 * p + 1)):
                s = _dot_nt(qp, k_ref_[h, pl.ds(ks, nk), :])
                fill = jnp.where(col_row == sink_col, sink_ref[layer, head], -jnp.inf)
                s = jnp.where(mask, s, fill)
                e = jnp.exp(s - jnp.max(s, axis=1, keepdims=True))
                pv = _dot(e.astype(BF16), vvw[h])
                halves.append(pv[:, :LANES] / pv[:, LANES:])
            o_ref[0, pl.ds(q0, nq), p * LANES:(p + 1) * LANES] = jnp.where(low, halves[0], halves[1])

    n_full = seq // WINDOW
    tail = seq - n_full * WINDOW

    def body(i, carry):
        q0 = pl.multiple_of(i * WINDOW, WINDOW)
        ks = pl.multiple_of(jnp.maximum(q0 - WINDOW, 0), WINDOW)
        tile(q0, ks, WINDOW, 2 * WINDOW, jnp.where(i == 0, 2 * WINDOW - 1, 0))
        return carry

    lax.fori_loop(0, n_full, body, 0)
    if tail:
        q0 = n_full * WINDOW
        tile(q0, q0 - WINDOW, tail, WINDOW + tail, 0)


def _attn_prompt(layer, sinks, q, k, v, chunk):
    b, seq, _ = q.shape
    assert seq >= 2 * WINDOW
    kern = functools.partial(_attn_prompt_kernel, layer=layer, seq=seq, chunk=chunk)
    seq_spec = lambda w_: pl.BlockSpec((1, seq, w_), lambda i: (i, 0, 0))
    return pl.pallas_call(
        kern,
        grid=(b,),
        in_specs=[pl.BlockSpec(memory_space=pltpu.SMEM),
                  seq_spec(ATTN_W), seq_spec(KV_W), seq_spec(KV_W)],
        out_specs=seq_spec(ATTN_W),
        out_shape=jax.ShapeDtypeStruct((b, seq, ATTN_W), F32),
        scratch_shapes=[pltpu.VMEM((N_KV_HEADS, seq, LANES), BF16),
                        pltpu.VMEM((N_KV_HEADS, seq, LANES), BF16),
                        pltpu.VMEM((N_KV_HEADS, seq, 2 * LANES), BF16)],
        compiler_params=_params(),
        name="attn_prompt",
    )(sinks, q, k, v)


def _gelu_tanh(x):
    return 0.5 * x * (1.0 + jnp.tanh(0.7978845608028654 * (x + 0.044715 * (x * x * x))))


def _lru_gates(u, wab, bab, logsig):
    ga = _dot(u.astype(BF16), wab) + bab
    r = jax.nn.sigmoid(ga[:, :LRU_W])
    ig = jax.nn.sigmoid(ga[:, LRU_W:])
    log_a = LRU_C * r * logsig
    a = jnp.exp(log_a)
    b = jnp.sqrt(-jnp.tanh(log_a) * (a * a + 1.0)) * ig * u
    return a, b


def _lru_prompt_kernel(xb_ref, gate_ref, conv0_ref, h0_ref, cw_ref, cb_ref, wab_ref, bab_ref,
                       lam_ref, o_ref, ht_ref, xpad_ref, a_ref, b_ref, *, seq, chunk):
    n_chunks = seq // chunk
    pad = SUBLANES
    xpad_ref[0:pad, :] = conv0_ref[0]
    for c in range(n_chunks):
        xpad_ref[pad + c * chunk:pad + (c + 1) * chunk, :] = xb_ref[0, c * chunk:(c + 1) * chunk, :]

    logsig = jax.nn.log_sigmoid(lam_ref[...])
    wab = wab_ref[...]
    bab = bab_ref[...]
    for c in range(n_chunks):
        r0 = c * chunk
        u = cb_ref[...]
        for j in range(CONV_W):
            start = pad + r0 - (CONV_W - 1) + j
            u = u + xpad_ref[start:start + chunk, :] * cw_ref[j:j + 1, :]
        a, b = _lru_gates(u, wab, bab, logsig)
        for s in range(N_SLABS):
            a_ref[s, r0:r0 + chunk, :] = a[:, s * LANES:(s + 1) * LANES]
            b_ref[s, r0:r0 + chunk, :] = b[:, s * LANES:(s + 1) * LANES]

    piece = seq // SUBLANES
    gather = lambda ref, s, i: ref[s, pl.ds(i, SUBLANES, stride=piece), :]

    def local_scan(i, carry):
        hs, ps = carry
        nh, npr = [], []
        for s in range(N_SLABS):
            a = gather(a_ref, s, i)
            h = a * hs[s] + gather(b_ref, s, i)
            p = a * ps[s]
            b_ref[s, pl.ds(i, SUBLANES, stride=piece), :] = h
            a_ref[s, pl.ds(i, SUBLANES, stride=piece), :] = p
            nh.append(h)
            npr.append(p)
        return tuple(nh), tuple(npr)

    zeros = tuple(jnp.zeros((SUBLANES, LANES), F32) for _ in range(N_SLABS))
    ones = tuple(jnp.ones((SUBLANES, LANES), F32) for _ in range(N_SLABS))
    h_end, p_end = lax.fori_loop(0, piece, local_scan, (zeros, ones))

    sub = lax.broadcasted_iota(jnp.int32, (SUBLANES, LANES), 0)
    h0 = h0_ref[0]
    carry_in, h_last = [], []
    for s in range(N_SLABS):
        cin = jnp.broadcast_to(h0[:, s * LANES:(s + 1) * LANES], (SUBLANES, LANES))
        acc = jnp.zeros((SUBLANES, LANES), F32)
        for c in range(SUBLANES):
            acc = jnp.where(sub == c, cin, acc)
            end = h_end[s] + p_end[s] * cin
            cin = jnp.broadcast_to(end[c:c + 1, :], (SUBLANES, LANES))
        carry_in.append(acc)
        h_last.append(cin[0:1, :])
    ht_ref[0] = jnp.concatenate(h_last, axis=1)

    def apply_carry(i, carry):
        for s in range(N_SLABS):
            h = gather(b_ref, s, i) + gather(a_ref, s, i) * carry_in[s]
            b_ref[s, pl.ds(i, SUBLANES, stride=piece), :] = h
        return carry

    lax.fori_loop(0, piece, apply_carry, 0)

    for c in range(n_chunks):
        rows = slice(c * chunk, (c + 1) * chunk)
        for s in range(N_SLABS):
            cols = slice(s * LANES, (s + 1) * LANES)
            o_ref[0, rows, cols] = b_ref[s, rows, :] * _gelu_tanh(gate_ref[0, rows, cols])


def _lru_prompt(layer, xb, gate, conv0, h0, cw, cb, wab, bab, lam, chunk):
    b, seq, _ = xb.shape
    kern = functools.partial(_lru_prompt_kernel, seq=seq, chunk=chunk)
    seq_spec = pl.BlockSpec((1, seq, LRU_W), lambda i: (i, 0, 0))
    lyr = lambda a: _layer_spec(a, layer)
    return pl.pallas_call(
        kern,
        grid=(b,),
        in_specs=[seq_spec, seq_spec,
                  pl.BlockSpec((1, SUBLANES, LRU_W), lambda i: (i, 0, 0)),
                  pl.BlockSpec((1, 1, LRU_W), lambda i: (i, 0, 0)),
                  lyr(cw), lyr(cb), lyr(wab), lyr(bab), lyr(lam)],
        out_specs=[seq_spec, pl.BlockSpec((1, 1, LRU_W), lambda i: (i, 0, 0))],
        out_shape=[jax.ShapeDtypeStruct((b, seq, LRU_W), F32),
                   jax.ShapeDtypeStruct((b, 1, LRU_W), F32)],
        scratch_shapes=[pltpu.VMEM((seq + SUBLANES, LRU_W), F32),
                        pltpu.VMEM((N_SLABS, seq, LANES), F32),
                        pltpu.VMEM((N_SLABS, seq, LANES), F32)],
        compiler_params=_params(),
        name="lru_prompt",
    )(xb, gate, conv0, h0, cw, cb, wab, bab, lam)


def _div_pow2(x, n):
    assert n & (n - 1) == 0
    return x >> (n.bit_length() - 1)


def _mod_pow2(x, n):
    assert n & (n - 1) == 0
    return x & (n - 1)


def _attn_sample_kernel(sink_ref, q_ref, kn_ref, vn_ref, ck_ref, cv_ref,
                        o_ref, nk_ref, nv_ref, *, layer, group, n_new):
    rpb = PAIRS_PER_KV * n_new
    new_pad = kn_ref.shape[1]
    rows = group * rpb
    ncols = group * new_pad
    row = lax.broadcasted_iota(jnp.int32, (rows, 1), 0)
    qpos = _mod_pow2(row, n_new)
    row_seq = _div_pow2(row, rpb)
    second_pair = _mod_pow2(_div_pow2(row, n_new), PAIRS_PER_KV) == 1
    col = lax.broadcasted_iota(jnp.int32, (1, WINDOW), 1)
    mask_c = col > qpos
    if PAST_LEN < WINDOW:
        mask_c = mask_c & (col >= WINDOW - PAST_LEN)
    ncol = lax.broadcasted_iota(jnp.int32, (1, ncols), 1)
    mask_n = (_div_pow2(ncol, new_pad) == row_seq) & (_mod_pow2(ncol, new_pad) <= qpos)
    low = lax.broadcasted_iota(jnp.int32, (rpb, LANES), 1) < HEAD_DIM
    zero_half = jnp.zeros((HEAD_DIM, LANES), BF16)

    kn_all = kn_ref[...].reshape(ncols, KV_W)
    vn_all = vn_ref[...].reshape(ncols, KV_W)
    knt_all = kn_all.T
    vnt_all = vn_all.T
    vn_lane = lax.broadcasted_iota(jnp.int32, vn_all.shape, 1) < HEAD_DIM
    vn_roll = pltpu.roll(vn_all, HEAD_DIM, 1)

    keep = WINDOW - n_new
    cache_lane = lax.broadcasted_iota(jnp.int32, (HEAD_DIM, WINDOW), 1) < keep

    for h in range(N_KV_HEADS):
        hd = slice(h * HEAD_DIM, (h + 1) * HEAD_DIM)
        q_all = q_ref[:, h].reshape(rows, LANES).astype(BF16)
        q_seq = [q_ref[b, h].astype(BF16) for b in range(group)]
        knt_h = knt_all[hd, :].astype(BF16)
        vvn = (jnp.where(vn_lane, vn_all, vn_roll) if h == 0
               else jnp.where(vn_lane, vn_roll, vn_all)).astype(BF16)
        kt, vt2 = [], []
        for b in range(group):
            kt.append(ck_ref[b, h].astype(BF16))
            vt = cv_ref[b, h].astype(BF16)
            vt2.append(jnp.concatenate([vt, vt], axis=0))
        probs, dens = [], []
        for first in (True, False):
            embed = lambda t: jnp.concatenate([t, zero_half] if first else [zero_half, t], axis=0)
            s_new = jnp.where(mask_n, _dot(q_all, embed(knt_h)), -jnp.inf)
            s_c = jnp.concatenate(
                [_dot(q_seq[b], embed(kt[b])) for b in range(group)], axis=0)
            s_c = jnp.where(mask_c, s_c, -jnp.inf)
            pair0 = 2 * h * PAIRS_PER_KV + (0 if first else 1)
            sink = jnp.where(second_pair, sink_ref[layer,pair0 + 2], sink_ref[layer,pair0])
            mx = jnp.maximum(jnp.maximum(jnp.max(s_c, axis=1, keepdims=True),
                                         jnp.max(s_new, axis=1, keepdims=True)), sink)
            p_c = jnp.exp(s_c - mx)
            p_n = jnp.exp(s_new - mx)
            dens.append(jnp.sum(p_c, axis=1, keepdims=True) + jnp.sum(p_n, axis=1, keepdims=True)
                        + jnp.exp(sink - mx))
            probs.append((p_c, _dot(p_n.astype(BF16), vvn)))
        for b in range(group):
            rs = slice(b * rpb, (b + 1) * rpb)
            pc = jnp.concatenate([probs[0][0][rs], probs[1][0][rs]], axis=0).astype(BF16)
            oc = _dot_nt(pc, vt2[b])
            o_first = (oc[:rpb] + probs[0][1][rs]) / dens[0][rs]
            o_second = (oc[rpb:] + probs[1][1][rs]) / dens[1][rs]
            o_ref[b, h] = jnp.where(low, o_first, o_second)
            shift = (keep - b * new_pad) % LANES
            nk_ref[b, h] = jnp.where(cache_lane, pltpu.roll(ck_ref[b, h], keep, 1),
                                     pltpu.roll(knt_all[hd, :], shift, 1))
            nv_ref[b, h] = jnp.where(cache_lane, pltpu.roll(cv_ref[b, h], keep, 1),
                                     pltpu.roll(vnt_all[hd, :], shift, 1))


def _attn_sample(layer, sinks, q, kn, vn, ck, cv, group, n_new):
    bs = q.shape[0]
    kern = functools.partial(_attn_sample_kernel, layer=layer, group=group, n_new=n_new)
    seq4 = lambda a: pl.BlockSpec((group,) + a.shape[1:], lambda i: (i, 0, 0, 0))
    seq3 = lambda a: pl.BlockSpec((group,) + a.shape[1:], lambda i: (i, 0, 0))
    cache = pl.BlockSpec((None, group) + ck.shape[2:], lambda i: (layer, i, 0, 0, 0))
    cache_out = pl.BlockSpec((group,) + ck.shape[2:], lambda i: (i, 0, 0, 0))
    return pl.pallas_call(
        kern,
        grid=(bs // group,),
        in_specs=[pl.BlockSpec(memory_space=pltpu.SMEM),
                  seq4(q), seq3(kn), seq3(vn), cache, cache],
        out_specs=[seq4(q), cache_out, cache_out],
        out_shape=[jax.ShapeDtypeStruct(q.shape, F32),
                   jax.ShapeDtypeStruct(ck.shape[1:], F32),
                   jax.ShapeDtypeStruct(cv.shape[1:], F32)],
        compiler_params=_params(),
        name="attn_sample",
    )(sinks, q, kn, vn, ck, cv)


def _lru_sample_kernel(xb_ref, gate_ref, conv0_ref, h0_ref, cw_ref, cb_ref, wab_ref,
                       bab_ref, lam_ref, lru_o_ref, ht_ref, *, n_new):
    logsig = jax.nn.log_sigmoid(lam_ref[...])
    wab = wab_ref[...]
    bab = bab_ref[...]
    xs = [conv0_ref[j] for j in range(CONV_W - 1)] + [xb_ref[t] for t in range(n_new)]
    h = h0_ref[...]
    for t in range(n_new):
        u = cb_ref[...]
        for j in range(CONV_W):
            u = u + xs[t + j] * cw_ref[j:j + 1, :]
        a, b = _lru_gates(u, wab, bab, logsig)
        h = a * h + b
        lru_o_ref[t] = h * _gelu_tanh(gate_ref[t])
    ht_ref[...] = h


def _lru_sample(layer, xb_t, gate_t, conv0_t, h0, cw, cb, wab, bab, lam):
    n_new = xb_t.shape[0]
    kern = functools.partial(_lru_sample_kernel, n_new=n_new)
    lyr = lambda a: _layer_spec(a, layer)
    return pl.pallas_call(
        kern,
        grid=(1,),
        in_specs=[_const_spec(xb_t.shape), _const_spec(gate_t.shape),
                  lyr(conv0_t), lyr(h0), lyr(cw), lyr(cb), lyr(wab), lyr(bab), lyr(lam)],
        out_specs=[_const_spec(xb_t.shape), _const_spec(h0.shape[1:])],
        out_shape=[jax.ShapeDtypeStruct(xb_t.shape, F32),
                   jax.ShapeDtypeStruct(h0.shape[1:], F32)],
        compiler_params=_params(),
        name="lru_sample",
    )(xb_t, gate_t, conv0_t, h0, cw, cb, wab, bab, lam)


def _rope_tables(pos):
    half = HEAD_DIM // 2
    inv = ROPE_THETA ** (-jnp.arange(half, dtype=F32) / half)
    ang = pos.astype(F32)[:, None] * inv[None, :]
    cos, sin = jnp.cos(ang), jnp.sin(ang)
    reps = LANES // HEAD_DIM
    return (jnp.concatenate([cos, cos] * reps, axis=1),
            jnp.concatenate([-sin, sin] * reps, axis=1))


def _block_diag(w):
    eye = jnp.eye(LRU_BLOCKS, dtype=w.dtype)
    full = w[:, :, :, None, :] * eye[None, :, None, :, None]
    return full.reshape(w.shape[0], LRU_W, LRU_W)


def _pick_tile(n, candidates):
    for c in candidates:
        if n % c == 0:
            return c
    raise ValueError(f"no tile for {n} tokens")


def kernel(x_prompt, x_sample, cache_k, cache_v, state_h, state_conv, meta_tokens, pre_mix_norm,
           w_in, sinks, conv_w, conv_b, w_a, b_a, w_i, b_i, lam, attn_out_norm, lru_out_norm,
           w_out, post_mix_norm, pre_ffn_norm, w_gate, w_up, w_down, post_ffn_norm):
    bp, seq_in, _ = x_prompt.shape
    seq = seq_in + N_META
    bs, n_new, _ = x_sample.shape
    assert seq % SUBLANES == 0 and (seq % WINDOW) % 16 == 0 and n_new >= CONV_W - 1

    vec = lambda a: a.reshape(DEPTH, 1, -1)
    w_in_b = w_in.astype(BF16)
    w_out_b = w_out.astype(BF16)
    w_gate_b = w_gate.astype(BF16)
    w_up_b = w_up.astype(BF16)
    w_down_b = w_down.astype(BF16)
    wab = jnp.concatenate([_block_diag(w_a), _block_diag(w_i)], axis=2).astype(BF16)
    bab = vec(jnp.concatenate([b_a, b_i], axis=1))
    g_pre, g_att, g_lru = vec(pre_mix_norm), vec(attn_out_norm), vec(lru_out_norm)
    g_pm, g_pf, g_po = vec(post_mix_norm), vec(pre_ffn_norm), vec(post_ffn_norm)
    cb, lam_v = vec(conv_b), vec(lam)

    meta = jnp.broadcast_to(meta_tokens.astype(x_prompt.dtype)[None], (bp, N_META, D_MODEL))
    xp = jnp.concatenate([meta, x_prompt], axis=1).reshape(bp * seq, D_MODEL)
    xs = x_sample.reshape(bs * n_new, D_MODEL)

    cos_p, sin_p = _rope_tables(jnp.arange(seq, dtype=jnp.int32))
    pos_s = PAST_LEN + jnp.arange(n_new, dtype=jnp.int32)
    cos_s, sin_s = _rope_tables(jnp.tile(pos_s, bs))

    tm_p = _pick_tile(seq, (688, 512, 256, 128, 16))
    tm_s = bs * n_new
    tm_f = _pick_tile(bp * seq, (384, 256, 128, 16))
    conv0_p = jnp.zeros((bp, SUBLANES, LRU_W), F32)
    h0_p = jnp.zeros((bp, 1, LRU_W), F32)
    new_pad = SUBLANES
    group = LANES // new_pad
    assert n_new <= new_pad and bs % group == 0 and WINDOW == LANES

    ck_t = cache_k.transpose(0, 1, 3, 4, 2)
    cv_t = cache_v.transpose(0, 1, 3, 4, 2)
    conv_s = state_conv.transpose(0, 2, 1, 3)

    pk, pv, ph, pc, sk, sv, sh, sc = [], [], [], [], [], [], [], []
    for l in range(DEPTH):
        q, k, v, xb, gate = _inproj(l, xp, g_pre, w_in_b, cos_p, sin_p, tm_p, seq // tm_p, BF16)
        k3 = k.reshape(bp, seq, KV_W)
        v3 = v.reshape(bp, seq, KV_W)
        xb3 = xb.reshape(bp, seq, LRU_W)
        att = _attn_prompt(l, sinks, q.reshape(bp, seq, ATTN_W), k3, v3, tm_p)
        lru, ht = _lru_prompt(l, xb3, gate.reshape(bp, seq, LRU_W), conv0_p, h0_p,
                              conv_w, cb, wab, bab, lam_v, tm_p)
        xp = _ffn(l, xp, att.reshape(bp * seq, ATTN_W), lru.reshape(bp * seq, LRU_W),
                  g_att, g_lru, w_out_b, g_pm, g_pf, w_gate_b, w_up_b, w_down_b, g_po, tm_f)
        pk.append(k3[:, -WINDOW:].reshape(bp, WINDOW, N_KV_HEADS, HEAD_DIM))
        pv.append(v3[:, -WINDOW:].reshape(bp, WINDOW, N_KV_HEADS, HEAD_DIM))
        ph.append(ht.reshape(bp, LRU_W))
        pc.append(xb3[:, -(CONV_W - 1):])

        q, k, v, xb, gate = _inproj(l, xs, g_pre, w_in_b, cos_s, sin_s, tm_s, 1, F32)
        q4 = q.reshape(bs, n_new, N_KV_HEADS, PAIRS_PER_KV, LANES).transpose(0, 2, 3, 1, 4).reshape(
            bs, N_KV_HEADS, PAIRS_PER_KV * n_new, LANES)
        padn = ((0, 0), (0, new_pad - n_new), (0, 0))
        o4, nk, nv = _attn_sample(l, sinks, q4,
                                  jnp.pad(k.reshape(bs, n_new, KV_W), padn),
                                  jnp.pad(v.reshape(bs, n_new, KV_W), padn),
                                  ck_t, cv_t, group, n_new)
        att_raw = o4.reshape(bs, N_KV_HEADS, PAIRS_PER_KV, n_new, LANES).transpose(
            0, 3, 1, 2, 4).reshape(bs * n_new, ATTN_W)
        xb3 = xb.reshape(bs, n_new, LRU_W)
        tmaj = lambda a: a.transpose(1, 0, 2)
        lru_t, ht = _lru_sample(l, tmaj(xb3), tmaj(gate.reshape(bs, n_new, LRU_W)),
                                conv_s, state_h, conv_w, cb, wab, bab, lam_v)
        xs = _ffn(l, xs, att_raw, tmaj(lru_t).reshape(bs * n_new, LRU_W),
                  g_att, g_lru, w_out_b, g_pm, g_pf, w_gate_b, w_up_b, w_down_b, g_po, tm_s)
        sk.append(nk.transpose(0, 3, 1, 2))
        sv.append(nv.transpose(0, 3, 1, 2))
        sh.append(ht)
        sc.append(jnp.concatenate([state_conv[l], xb3], axis=1)[:, -(CONV_W - 1):])

    y_prompt = xp.reshape(bp, seq, D_MODEL)[:, N_META:]
    y_sample = xs.reshape(bs, n_new, D_MODEL)
    return (y_prompt, y_sample, jnp.stack(pk), jnp.stack(pv), jnp.stack(ph), jnp.stack(pc),
            jnp.stack(sk), jnp.stack(sv), jnp.stack(sh), jnp.stack(sc))
```

```python
import functools

import jax
import jax.numpy as jnp
from jax import lax
from jax.experimental import pallas as pl
from jax.experimental.pallas import tpu as pltpu

D_MODEL = 1024
DEPTH = 4
PAST_LEN = 8192
N_META = 16
HEAD_DIM = 64
N_Q_HEADS = 8
N_KV_HEADS = 2
ATTN_W = N_Q_HEADS * HEAD_DIM
KV_W = N_KV_HEADS * HEAD_DIM
LRU_W = D_MODEL - ATTN_W
LRU_BLOCKS = 8
LRU_BLOCK_W = LRU_W // LRU_BLOCKS
CONV_W = 4
LRU_C = 8.0
WINDOW = 128
ROPE_THETA = 10000.0
D_FF = 2816
IN_W = ATTN_W + 2 * KV_W + 2 * LRU_W
EPS = 1e-6

LANES = 128
SUBLANES = 8
N_SLABS = LRU_W // LANES
N_PAIRS = ATTN_W // LANES
PAIRS_PER_KV = N_PAIRS // N_KV_HEADS
VMEM_LIMIT = 56 * 1024 * 1024

BF16 = jnp.bfloat16
F32 = jnp.float32


def _rms(x, g):
    ms = jnp.mean(x * x, axis=-1, keepdims=True)
    return x * lax.rsqrt(ms + EPS) * g


def _dot(a, b):
    return jnp.dot(a, b, preferred_element_type=F32)


def _dot_nt(a, b):
    return lax.dot_general(a, b, (((1,), (1,)), ((), ())), preferred_element_type=F32)


def _const_spec(shape):
    zeros = (0,) * len(shape)
    return pl.BlockSpec(shape, lambda *_: zeros)


def _layer_spec(arr, layer, single_buffer=False):
    tail = (0,) * (arr.ndim - 1)
    mode = dict(pipeline_mode=pl.Buffered(1)) if single_buffer else {}
    return pl.BlockSpec((None,) + arr.shape[1:], lambda *_: (layer,) + tail, **mode)


def _params(n_axes=1):
    return pltpu.CompilerParams(dimension_semantics=("arbitrary",) * n_axes,
                                vmem_limit_bytes=VMEM_LIMIT)


XB_COL = ATTN_W + 2 * KV_W
GATE_COL = XB_COL + LRU_W


def _store_qkv(z, cos_ref, sin_ref, q_ref, k_ref, v_ref):
    cos = cos_ref[...]
    sin = sin_ref[...]
    lane = lax.broadcasted_iota(jnp.int32, cos.shape, 1)
    first_half = (lane & (HEAD_DIM // 2)) == 0

    def rope(t):
        swapped = jnp.where(first_half,
                            pltpu.roll(t, LANES - HEAD_DIM // 2, 1),
                            pltpu.roll(t, HEAD_DIM // 2, 1))
        return t * cos + swapped * sin

    for p in range(N_PAIRS):
        qp = rope(z[:, p * LANES:(p + 1) * LANES]) * (HEAD_DIM ** -0.5)
        q_ref[:, p * LANES:(p + 1) * LANES] = qp.astype(q_ref.dtype)
    k_ref[...] = rope(z[:, ATTN_W:ATTN_W + KV_W])
    v_ref[...] = z[:, ATTN_W + KV_W:XB_COL]


def _inproj_kernel(x_ref, g_ref, w_ref, cos_ref, sin_ref,
                   q_ref, k_ref, v_ref, xb_ref, gate_ref):
    hn = _rms(x_ref[...], g_ref[...]).astype(BF16)
    z = _dot(hn, w_ref[...])
    _store_qkv(z[:, :XB_COL], cos_ref, sin_ref, q_ref, k_ref, v_ref)
    xb_ref[...] = z[:, XB_COL:GATE_COL]
    gate_ref[...] = z[:, GATE_COL:]


def _inproj(layer, x, g, w, cos, sin, tm, tiles_per_seq, q_dtype):
    t = x.shape[0]
    tok = lambda w_: pl.BlockSpec((tm, w_), lambda i: (i, 0))
    tab = pl.BlockSpec((tm, LANES), lambda i: (i % tiles_per_seq, 0))
    return pl.pallas_call(
        _inproj_kernel,
        grid=(t // tm,),
        in_specs=[tok(D_MODEL), _layer_spec(g, layer), _layer_spec(w, layer), tab, tab],
        out_specs=[tok(ATTN_W), tok(KV_W), tok(KV_W), tok(LRU_W), tok(LRU_W)],
        out_shape=[jax.ShapeDtypeStruct((t, ATTN_W), q_dtype),
                   jax.ShapeDtypeStruct((t, KV_W), F32),
                   jax.ShapeDtypeStruct((t, KV_W), F32),
                   jax.ShapeDtypeStruct((t, LRU_W), F32),
                   jax.ShapeDtypeStruct((t, LRU_W), F32)],
        compiler_params=_params(),
        name="inproj",
    )(x, g, w, cos, sin)


def _ffn_kernel(x_ref, att_ref, lru_ref, ga_ref, gl_ref, wo_ref, gpm_ref, gpf_ref,
                wg_ref, wu_ref, wd_ref, gpo_ref, o_ref):
    att = _rms(att_ref[...], ga_ref[...]).astype(BF16)
    lru = _rms(lru_ref[...], gl_ref[...]).astype(BF16)
    m = _dot(att, wo_ref[0:ATTN_W, :]) + _dot(lru, wo_ref[ATTN_W:, :])
    x1 = x_ref[...] + _rms(m, gpm_ref[...])
    hf = _rms(x1, gpf_ref[...]).astype(BF16)
    gt = _dot(hf, wg_ref[...])
    up = _dot(hf, wu_ref[...])
    hid = (gt * jax.nn.sigmoid(gt) * up).astype(BF16)
    f = _dot(hid, wd_ref[...])
    o_ref[...] = x1 + _rms(f, gpo_ref[...])


def _ffn(layer, x, att, lru, ga, gl, wo, gpm, gpf, wg, wu, wd, gpo, tm):
    t = x.shape[0]
    tok = lambda w_: pl.BlockSpec((tm, w_), lambda i: (i, 0))
    big = lambda a: _layer_spec(a, layer, single_buffer=True)
    lyr = lambda a: _layer_spec(a, layer)
    return pl.pallas_call(
        _ffn_kernel,
        grid=(t // tm,),
        in_specs=[tok(D_MODEL), tok(ATTN_W), tok(LRU_W), lyr(ga), lyr(gl),
                  big(wo), lyr(gpm), lyr(gpf), big(wg), big(wu), big(wd), lyr(gpo)],
        out_specs=tok(D_MODEL),
        out_shape=jax.ShapeDtypeStruct((t, D_MODEL), F32),
        compiler_params=_params(),
        name="outffn",
    )(x, att, lru, ga, gl, wo, gpm, gpf, wg, wu, wd, gpo)


def _kv_variants(k, v):
    lane = lax.broadcasted_iota(jnp.int32, k.shape, 1)
    low = lane < HEAD_DIM
    kr = pltpu.roll(k, HEAD_DIM, 1)
    vr = pltpu.roll(v, HEAD_DIM, 1)
    zero = jnp.zeros_like(k)
    ka = (jnp.where(low, k, zero), jnp.where(low, kr, zero))
    kb = (jnp.where(low, zero, kr), jnp.where(low, zero, k))
    vv = (jnp.where(low, v, vr), jnp.where(low, vr, v))
    cast = lambda pair: tuple(a.astype(BF16) for a in pair)
    return cast(ka), cast(kb), cast(vv)


def _attn_prompt_kernel(sink_ref, q_ref, k_ref, v_ref, o_ref,
                        ka_ref, kb_ref, vv_ref, *, layer, seq, chunk):
    for c in range(seq // chunk):
        rows = slice(c * chunk, (c + 1) * chunk)
        ka, kb, vv = _kv_variants(k_ref[0, rows, :], v_ref[0, rows, :])
        ones = jnp.ones((chunk, LANES), BF16)
        for h in range(N_KV_HEADS):
            ka_ref[h, rows, :] = ka[h]
            kb_ref[h, rows, :] = kb[h]
            vv_ref[h, rows, :] = jnp.concatenate([vv[h], ones], axis=1)

    def tile(q0, ks, nq, nk, sink_col):
        row = lax.broadcasted_iota(jnp.int32, (nq, nk), 0)
        col = lax.broadcasted_iota(jnp.int32, (nq, nk), 1)
        rel = row - col + (q0 - ks)
        mask = (rel >= 0) & (rel < WINDOW)
        col_row = lax.broadcasted_iota(jnp.int32, (1, nk), 1)
        key_row = lax.broadcasted_iota(jnp.int32, (nk, 2 * LANES), 0)
        val_lane = lax.broadcasted_iota(jnp.int32, (nk, 2 * LANES), 1)
        zero_v = (key_row == sink_col) & (val_lane < LANES)
        vvw = [jnp.where(zero_v, jnp.zeros((), BF16), vv_ref[h, pl.ds(ks, nk), :])
               for h in range(N_KV_HEADS)]
        low = lax.broadcasted_iota(jnp.int32, (nq, LANES), 1) < HEAD_DIM
        for p in range(N_PAIRS):
            h = p // PAIRS_PER_KV
            qp = q_ref[0, pl.ds(q0, nq), p * LANES:(p + 1) * LANES]
            halves = []
            for k_ref_, head in ((ka_ref, 2 * p), (kb_ref, 2 * p + 1)):
                s = _dot_nt(qp, k_ref_[h, pl.ds(ks, nk), :])
                fill = jnp.where(col_row == sink_col, sink_ref[layer, head], -jnp.inf)
                s = jnp.where(mask, s, fill)
                e = jnp.exp(s - jnp.max(s, axis=1, keepdims=True))
                pv = _dot(e.astype(BF16), vvw[h])
                halves.append(pv[:, :LANES] / pv[:, LANES:])
            o_ref[0, pl.ds(q0, nq), p * LANES:(p + 1) * LANES] = jnp.where(low, halves[0], halves[1])

    n_full = seq // WINDOW
    tail = seq - n_full * WINDOW

    def body(i, carry):
        q0 = pl.multiple_of(i * WINDOW, WINDOW)
        ks = pl.multiple_of(jnp.maximum(q0 - WINDOW, 0), WINDOW)
        tile(q0, ks, WINDOW, 2 * WINDOW, jnp.where(i == 0, 2 * WINDOW - 1, 0))
        return carry

    lax.fori_loop(0, n_full, body, 0)
    if tail:
        q0 = n_full * WINDOW
        tile(q0, q0 - WINDOW, tail, WINDOW + tail, 0)


def _attn_prompt(layer, sinks, q, k, v, chunk):
    b, seq, _ = q.shape
    assert seq >= 2 * WINDOW
    kern = functools.partial(_attn_prompt_kernel, layer=layer, seq=seq, chunk=chunk)
    seq_spec = lambda w_: pl.BlockSpec((1, seq, w_), lambda i: (i, 0, 0))
    return pl.pallas_call(
        kern,
        grid=(b,),
        in_specs=[pl.BlockSpec(memory_space=pltpu.SMEM),
                  seq_spec(ATTN_W), seq_spec(KV_W), seq_spec(KV_W)],
        out_specs=seq_spec(ATTN_W),
        out_shape=jax.ShapeDtypeStruct((b, seq, ATTN_W), F32),
        scratch_shapes=[pltpu.VMEM((N_KV_HEADS, seq, LANES), BF16),
                        pltpu.VMEM((N_KV_HEADS, seq, LANES), BF16),
                        pltpu.VMEM((N_KV_HEADS, seq, 2 * LANES), BF16)],
        compiler_params=_params(),
        name="attn_prompt",
    )(sinks, q, k, v)


def _gelu_tanh(x):
    return 0.5 * x * (1.0 + jnp.tanh(0.7978845608028654 * (x + 0.044715 * (x * x * x))))


def _lru_gates(u, wab, bab, logsig):
    ga = _dot(u.astype(BF16), wab) + bab
    r = jax.nn.sigmoid(ga[:, :LRU_W])
    ig = jax.nn.sigmoid(ga[:, LRU_W:])
    log_a = LRU_C * r * logsig
    a = jnp.exp(log_a)
    b = jnp.sqrt(-jnp.tanh(log_a) * (a * a + 1.0)) * ig * u
    return a, b


def _scan_tile(a_ref, b_ref, h0, rows):
    piece = rows // SUBLANES
    at = lambda i: pl.ds(i, SUBLANES, stride=piece)
    hs = [jnp.zeros((SUBLANES, LANES), F32) for _ in range(N_SLABS)]
    ps = [jnp.ones((SUBLANES, LANES), F32) for _ in range(N_SLABS)]
    for i in range(piece):
        for s in range(N_SLABS):
            a = a_ref[s, at(i), :]
            hs[s] = a * hs[s] + b_ref[s, at(i), :]
            ps[s] = a * ps[s]
            b_ref[s, at(i), :] = hs[s]
            a_ref[s, at(i), :] = ps[s]

    sub = lax.broadcasted_iota(jnp.int32, (SUBLANES, LANES), 0)
    carry_in, h_last = [], []
    for s in range(N_SLABS):
        cin = jnp.broadcast_to(h0[:, s * LANES:(s + 1) * LANES], (SUBLANES, LANES))
        acc = jnp.zeros((SUBLANES, LANES), F32)
        for c in range(SUBLANES):
            acc = jnp.where(sub == c, cin, acc)
            end = hs[s] + ps[s] * cin
            cin = jnp.broadcast_to(end[c:c + 1, :], (SUBLANES, LANES))
        carry_in.append(acc)
        h_last.append(cin[0:1, :])

    for i in range(piece):
        for s in range(N_SLABS):
            b_ref[s, at(i), :] = b_ref[s, at(i), :] + a_ref[s, at(i), :] * carry_in[s]
    return jnp.concatenate(h_last, axis=1)


def _inproj_lru_kernel(x_ref, g_ref, w_ref, cos_ref, sin_ref, cw_ref, cb_ref, wab_ref, bab_ref,
                       lam_ref, q_ref, k_ref, v_ref, o_ref, tail_ref, ht_ref,
                       xpad_ref, a_ref, b_ref, h_ref, *, tm, tiles_per_seq):
    pad = SUBLANES

    @pl.when(pl.program_id(0) % tiles_per_seq == 0)
    def _():
        xpad_ref[0:pad, :] = jnp.zeros((pad, LRU_W), F32)
        h_ref[...] = jnp.zeros((1, LRU_W), F32)

    hn = _rms(x_ref[...], g_ref[...]).astype(BF16)
    xb = _dot(hn, w_ref[:, XB_COL:GATE_COL])
    gate = _dot(hn, w_ref[:, GATE_COL:])

    xpad_ref[pad:pad + tm, :] = xb
    u = cb_ref[...]
    for j in range(CONV_W):
        start = pad - (CONV_W - 1) + j
        u = u + xpad_ref[start:start + tm, :] * cw_ref[j:j + 1, :]
    tail = xb[tm - pad:, :]
    tail_ref[0] = tail
    xpad_ref[0:pad, :] = tail

    a, b = _lru_gates(u, wab_ref[...], bab_ref[...], jax.nn.log_sigmoid(lam_ref[...]))
    for s in range(N_SLABS):
        a_ref[s] = a[:, s * LANES:(s + 1) * LANES]
        b_ref[s] = b[:, s * LANES:(s + 1) * LANES]
    h_last = _scan_tile(a_ref, b_ref, h_ref[...], tm)
    h_ref[...] = h_last
    ht_ref[0] = h_last
    for s in range(N_SLABS):
        cols = slice(s * LANES, (s + 1) * LANES)
        o_ref[:, cols] = b_ref[s] * _gelu_tanh(gate[:, cols])

    _store_qkv(_dot(hn, w_ref[:, :XB_COL]), cos_ref, sin_ref, q_ref, k_ref, v_ref)


def _inproj_lru(layer, x, g, w, cos, sin, cw, cb, wab, bab, lam, tm, tiles_per_seq):
    t = x.shape[0]
    n_seq = t // (tm * tiles_per_seq)
    assert tm % SUBLANES == 0 and tm >= SUBLANES
    kern = functools.partial(_inproj_lru_kernel, tm=tm, tiles_per_seq=tiles_per_seq)
    tok = lambda w_: pl.BlockSpec((tm, w_), lambda i: (i, 0))
    tab = pl.BlockSpec((tm, LANES), lambda i: (i % tiles_per_seq, 0))
    per_seq = lambda r: pl.BlockSpec((1, r, LRU_W), lambda i: (i // tiles_per_seq, 0, 0))
    lyr = lambda a: _layer_spec(a, layer)
    return pl.pallas_call(
        kern,
        grid=(t // tm,),
        in_specs=[tok(D_MODEL), lyr(g), lyr(w), tab, tab,
                  lyr(cw), lyr(cb), lyr(wab), lyr(bab), lyr(lam)],
        out_specs=[tok(ATTN_W), tok(KV_W), tok(KV_W), tok(LRU_W), per_seq(SUBLANES), per_seq(1)],
        out_shape=[jax.ShapeDtypeStruct((t, ATTN_W), BF16),
                   jax.ShapeDtypeStruct((t, KV_W), F32),
                   jax.ShapeDtypeStruct((t, KV_W), F32),
                   jax.ShapeDtypeStruct((t, LRU_W), F32),
                   jax.ShapeDtypeStruct((n_seq, SUBLANES, LRU_W), F32),
                   jax.ShapeDtypeStruct((n_seq, 1, LRU_W), F32)],
        scratch_shapes=[pltpu.VMEM((tm + SUBLANES, LRU_W), F32),
                        pltpu.VMEM((N_SLABS, tm, LANES), F32),
                        pltpu.VMEM((N_SLABS, tm, LANES), F32),
                        pltpu.VMEM((1, LRU_W), F32)],
        compiler_params=_params(),
        name="inproj_lru",
    )(x, g, w, cos, sin, cw, cb, wab, bab, lam)


def _div_pow2(x, n):
    assert n & (n - 1) == 0
    return x >> (n.bit_length() - 1)


def _mod_pow2(x, n):
    assert n & (n - 1) == 0
    return x & (n - 1)


def _attn_sample_kernel(sink_ref, q_ref, kn_ref, vn_ref, ck_ref, cv_ref,
                        o_ref, nk_ref, nv_ref, *, layer, group, n_new):
    rpb = PAIRS_PER_KV * n_new
    new_pad = kn_ref.shape[1]
    rows = group * rpb
    ncols = group * new_pad
    row = lax.broadcasted_iota(jnp.int32, (rows, 1), 0)
    qpos = _mod_pow2(row, n_new)
    row_seq = _div_pow2(row, rpb)
    second_pair = _mod_pow2(_div_pow2(row, n_new), PAIRS_PER_KV) == 1
    col = lax.broadcasted_iota(jnp.int32, (1, WINDOW), 1)
    mask_c = col > qpos
    if PAST_LEN < WINDOW:
        mask_c = mask_c & (col >= WINDOW - PAST_LEN)
    ncol = lax.broadcasted_iota(jnp.int32, (1, ncols), 1)
    mask_n = (_div_pow2(ncol, new_pad) == row_seq) & (_mod_pow2(ncol, new_pad) <= qpos)
    low = lax.broadcasted_iota(jnp.int32, (rpb, LANES), 1) < HEAD_DIM
    zero_half = jnp.zeros((HEAD_DIM, LANES), BF16)

    kn_all = kn_ref[...].reshape(ncols, KV_W)
    vn_all = vn_ref[...].reshape(ncols, KV_W)
    knt_all = kn_all.T
    vnt_all = vn_all.T
    vn_lane = lax.broadcasted_iota(jnp.int32, vn_all.shape, 1) < HEAD_DIM
    vn_roll = pltpu.roll(vn_all, HEAD_DIM, 1)

    keep = WINDOW - n_new
    cache_lane = lax.broadcasted_iota(jnp.int32, (HEAD_DIM, WINDOW), 1) < keep

    for h in range(N_KV_HEADS):
        hd = slice(h * HEAD_DIM, (h + 1) * HEAD_DIM)
        q_all = q_ref[:, h].reshape(rows, LANES).astype(BF16)
        q_seq = [q_ref[b, h].astype(BF16) for b in range(group)]
        knt_h = knt_all[hd, :].astype(BF16)
        vvn = (jnp.where(vn_lane, vn_all, vn_roll) if h == 0
               else jnp.where(vn_lane, vn_roll, vn_all)).astype(BF16)
        kt, vt2 = [], []
        for b in range(group):
            kt.append(ck_ref[b, h].astype(BF16))
            vt = cv_ref[b, h].astype(BF16)
            vt2.append(jnp.concatenate([vt, vt], axis=0))
        probs, dens = [], []
        for first in (True, False):
            embed = lambda t: jnp.concatenate([t, zero_half] if first else [zero_half, t], axis=0)
            s_new = jnp.where(mask_n, _dot(q_all, embed(knt_h)), -jnp.inf)
            s_c = jnp.concatenate(
                [_dot(q_seq[b], embed(kt[b])) for b in range(group)], axis=0)
            s_c = jnp.where(mask_c, s_c, -jnp.inf)
            pair0 = 2 * h * PAIRS_PER_KV + (0 if first else 1)
            sink = jnp.where(second_pair, sink_ref[layer,pair0 + 2], sink_ref[layer,pair0])
            mx = jnp.maximum(jnp.maximum(jnp.max(s_c, axis=1, keepdims=True),
                                         jnp.max(s_new, axis=1, keepdims=True)), sink)
            p_c = jnp.exp(s_c - mx)
            p_n = jnp.exp(s_new - mx)
            dens.append(jnp.sum(p_c, axis=1, keepdims=True) + jnp.sum(p_n, axis=1, keepdims=True)
                        + jnp.exp(sink - mx))
            probs.append((p_c, _dot(p_n.astype(BF16), vvn)))
        for b in range(group):
            rs = slice(b * rpb, (b + 1) * rpb)
            pc = jnp.concatenate([probs[0][0][rs], probs[1][0][rs]], axis=0).astype(BF16)
            oc = _dot_nt(pc, vt2[b])
            o_first = (oc[:rpb] + probs[0][1][rs]) / dens[0][rs]
            o_second = (oc[rpb:] + probs[1][1][rs]) / dens[1][rs]
            o_ref[b, h] = jnp.where(low, o_first, o_second)
            shift = (keep - b * new_pad) % LANES
            nk_ref[b, h] = jnp.where(cache_lane, pltpu.roll(ck_ref[b, h], keep, 1),
                                     pltpu.roll(knt_all[hd, :], shift, 1))
            nv_ref[b, h] = jnp.where(cache_lane, pltpu.roll(cv_ref[b, h], keep, 1),
                                     pltpu.roll(vnt_all[hd, :], shift, 1))


def _attn_sample(layer, sinks, q, kn, vn, ck, cv, group, n_new):
    bs = q.shape[0]
    kern = functools.partial(_attn_sample_kernel, layer=layer, group=group, n_new=n_new)
    seq4 = lambda a: pl.BlockSpec((group,) + a.shape[1:], lambda i: (i, 0, 0, 0))
    seq3 = lambda a: pl.BlockSpec((group,) + a.shape[1:], lambda i: (i, 0, 0))
    cache = pl.BlockSpec((None, group) + ck.shape[2:], lambda i: (layer, i, 0, 0, 0))
    cache_out = pl.BlockSpec((group,) + ck.shape[2:], lambda i: (i, 0, 0, 0))
    return pl.pallas_call(
        kern,
        grid=(bs // group,),
        in_specs=[pl.BlockSpec(memory_space=pltpu.SMEM),
                  seq4(q), seq3(kn), seq3(vn), cache, cache],
        out_specs=[seq4(q), cache_out, cache_out],
        out_shape=[jax.ShapeDtypeStruct(q.shape, F32),
                   jax.ShapeDtypeStruct(ck.shape[1:], F32),
                   jax.ShapeDtypeStruct(cv.shape[1:], F32)],
        compiler_params=_params(),
        name="attn_sample",
    )(sinks, q, kn, vn, ck, cv)


def _lru_sample_kernel(xb_ref, gate_ref, conv0_ref, h0_ref, cw_ref, cb_ref, wab_ref,
                       bab_ref, lam_ref, lru_o_ref, ht_ref, *, n_new):
    logsig = jax.nn.log_sigmoid(lam_ref[...])
    wab = wab_ref[...]
    bab = bab_ref[...]
    xs = [conv0_ref[j] for j in range(CONV_W - 1)] + [xb_ref[t] for t in range(n_new)]
    h = h0_ref[...]
    for t in range(n_new):
        u = cb_ref[...]
        for j in range(CONV_W):
            u = u + xs[t + j] * cw_ref[j:j + 1, :]
        a, b = _lru_gates(u, wab, bab, logsig)
        h = a * h + b
        lru_o_ref[t] = h * _gelu_tanh(gate_ref[t])
    ht_ref[...] = h


def _lru_sample(layer, xb_t, gate_t, conv0_t, h0, cw, cb, wab, bab, lam):
    n_new = xb_t.shape[0]
    kern = functools.partial(_lru_sample_kernel, n_new=n_new)
    lyr = lambda a: _layer_spec(a, layer)
    return pl.pallas_call(
        kern,
        grid=(1,),
        in_specs=[_const_spec(xb_t.shape), _const_spec(gate_t.shape),
                  lyr(conv0_t), lyr(h0), lyr(cw), lyr(cb), lyr(wab), lyr(bab), lyr(lam)],
        out_specs=[_const_spec(xb_t.shape), _const_spec(h0.shape[1:])],
        out_shape=[jax.ShapeDtypeStruct(xb_t.shape, F32),
                   jax.ShapeDtypeStruct(h0.shape[1:], F32)],
        compiler_params=_params(),
        name="lru_sample",
    )(xb_t, gate_t, conv0_t, h0, cw, cb, wab, bab, lam)


def _rope_tables(pos):
    half = HEAD_DIM // 2
    inv = ROPE_THETA ** (-jnp.arange(half, dtype=F32) / half)
    ang = pos.astype(F32)[:, None] * inv[None, :]
    cos, sin = jnp.cos(ang), jnp.sin(ang)
    reps = LANES // HEAD_DIM
    return (jnp.concatenate([cos, cos] * reps, axis=1),
            jnp.concatenate([-sin, sin] * reps, axis=1))


def _block_diag(w):
    eye = jnp.eye(LRU_BLOCKS, dtype=w.dtype)
    full = w[:, :, :, None, :] * eye[None, :, None, :, None]
    return full.reshape(w.shape[0], LRU_W, LRU_W)


def _pick_tile(n, candidates):
    for c in candidates:
        if n % c == 0:
            return c
    raise ValueError(f"no tile for {n} tokens")


def kernel(x_prompt, x_sample, cache_k, cache_v, state_h, state_conv, meta_tokens, pre_mix_norm,
           w_in, sinks, conv_w, conv_b, w_a, b_a, w_i, b_i, lam, attn_out_norm, lru_out_norm,
           w_out, post_mix_norm, pre_ffn_norm, w_gate, w_up, w_down, post_ffn_norm):
    bp, seq_in, _ = x_prompt.shape
    seq = seq_in + N_META
    bs, n_new, _ = x_sample.shape
    assert seq % SUBLANES == 0 and (seq % WINDOW) % 16 == 0 and n_new >= CONV_W - 1

    vec = lambda a: a.reshape(DEPTH, 1, -1)
    w_in_b = w_in.astype(BF16)
    w_out_b = w_out.astype(BF16)
    w_gate_b = w_gate.astype(BF16)
    w_up_b = w_up.astype(BF16)
    w_down_b = w_down.astype(BF16)
    wab = jnp.concatenate([_block_diag(w_a), _block_diag(w_i)], axis=2).astype(BF16)
    bab = vec(jnp.concatenate([b_a, b_i], axis=1))
    g_pre, g_att, g_lru = vec(pre_mix_norm), vec(attn_out_norm), vec(lru_out_norm)
    g_pm, g_pf, g_po = vec(post_mix_norm), vec(pre_ffn_norm), vec(post_ffn_norm)
    cb, lam_v = vec(conv_b), vec(lam)

    meta = jnp.broadcast_to(meta_tokens.astype(x_prompt.dtype)[None], (bp, N_META, D_MODEL))
    xp = jnp.concatenate([meta, x_prompt], axis=1).reshape(bp * seq, D_MODEL)
    xs = x_sample.reshape(bs * n_new, D_MODEL)

    cos_p, sin_p = _rope_tables(jnp.arange(seq, dtype=jnp.int32))
    pos_s = PAST_LEN + jnp.arange(n_new, dtype=jnp.int32)
    cos_s, sin_s = _rope_tables(jnp.tile(pos_s, bs))

    tm_p = _pick_tile(seq, (688, 512, 256, 128, 16))
    tm_s = bs * n_new
    tm_f = _pick_tile(bp * seq, (384, 256, 128, 16))
    conv0_p = jnp.zeros((bp, SUBLANES, LRU_W), F32)
    h0_p = jnp.zeros((bp, 1, LRU_W), F32)
    new_pad = SUBLANES
    group = LANES // new_pad
    assert n_new <= new_pad and bs % group == 0 and WINDOW == LANES

    ck_t = cache_k.transpose(0, 1, 3, 4, 2)
    cv_t = cache_v.transpose(0, 1, 3, 4, 2)
    conv_s = state_conv.transpose(0, 2, 1, 3)

    pk, pv, ph, pc, sk, sv, sh, sc = [], [], [], [], [], [], [], []
    for l in range(DEPTH):
        q, k, v, lru, xb_tail, ht = _inproj_lru(l, xp, g_pre, w_in_b, cos_p, sin_p,
                                                conv_w, cb, wab, bab, lam_v, tm_p, seq // tm_p)
        k3 = k.reshape(bp, seq, KV_W)
        v3 = v.reshape(bp, seq, KV_W)
        att = _attn_prompt(l, sinks, q.reshape(bp, seq, ATTN_W), k3, v3, tm_p)
        xp = _ffn(l, xp, att.reshape(bp * seq, ATTN_W), lru,
                  g_att, g_lru, w_out_b, g_pm, g_pf, w_gate_b, w_up_b, w_down_b, g_po, tm_f)
        pk.append(k3[:, -WINDOW:].reshape(bp, WINDOW, N_KV_HEADS, HEAD_DIM))
        pv.append(v3[:, -WINDOW:].reshape(bp, WINDOW, N_KV_HEADS, HEAD_DIM))
        ph.append(ht.reshape(bp, LRU_W))
        pc.append(xb_tail[:, -(CONV_W - 1):])

        q, k, v, xb, gate = _inproj(l, xs, g_pre, w_in_b, cos_s, sin_s, tm_s, 1, F32)
        q4 = q.reshape(bs, n_new, N_KV_HEADS, PAIRS_PER_KV, LANES).transpose(0, 2, 3, 1, 4).reshape(
            bs, N_KV_HEADS, PAIRS_PER_KV * n_new, LANES)
        padn = ((0, 0), (0, new_pad - n_new), (0, 0))
        o4, nk, nv = _attn_sample(l, sinks, q4,
                                  jnp.pad(k.reshape(bs, n_new, KV_W), padn),
                                  jnp.pad(v.reshape(bs, n_new, KV_W), padn),
                                  ck_t, cv_t, group, n_new)
        att_raw = o4.reshape(bs, N_KV_HEADS, PAIRS_PER_KV, n_new, LANES).transpose(
            0, 3, 1, 2, 4).reshape(bs * n_new, ATTN_W)
        xb3 = xb.reshape(bs, n_new, LRU_W)
        tmaj = lambda a: a.transpose(1, 0, 2)
        lru_t, ht = _lru_sample(l, tmaj(xb3), tmaj(gate.reshape(bs, n_new, LRU_W)),
                                conv_s, state_h, conv_w, cb, wab, bab, lam_v)
        xs = _ffn(l, xs, att_raw, tmaj(lru_t).reshape(bs * n_new, LRU_W),
                  g_att, g_lru, w_out_b, g_pm, g_pf, w_gate_b, w_up_b, w_down_b, g_po, tm_s)
        sk.append(nk.transpose(0, 3, 1, 2))
        sv.append(nv.transpose(0, 3, 1, 2))
        sh.append(ht)
        sc.append(jnp.concatenate([state_conv[l], xb3], axis=1)[:, -(CONV_W - 1):])

    y_prompt = xp.reshape(bp, seq, D_MODEL)[:, N_META:]
    y_sample = xs.reshape(bs, n_new, D_MODEL)
    return (y_prompt, y_sample, jnp.stack(pk), jnp.stack(pv), jnp.stack(ph), jnp.stack(pc),
            jnp.stack(sk), jnp.stack(sv), jnp.stack(sh), jnp.stack(sc))
```

```python
import functools

import jax
import jax.numpy as jnp
from jax import lax
from jax.experimental import pallas as pl
from jax.experimental.pallas import tpu as pltpu

D_MODEL = 1024
DEPTH = 4
PAST_LEN = 8192
N_META = 16
HEAD_DIM = 64
N_Q_HEADS = 8
N_KV_HEADS = 2
ATTN_W = N_Q_HEADS * HEAD_DIM
KV_W = N_KV_HEADS * HEAD_DIM
LRU_W = D_MODEL - ATTN_W
LRU_BLOCKS = 8
LRU_BLOCK_W = LRU_W // LRU_BLOCKS
CONV_W = 4
LRU_C = 8.0
WINDOW = 128
ROPE_THETA = 10000.0
D_FF = 2816
IN_W = ATTN_W + 2 * KV_W + 2 * LRU_W
EPS = 1e-6

LANES = 128
SUBLANES = 8
N_SLABS = LRU_W // LANES
N_PAIRS = ATTN_W // LANES
PAIRS_PER_KV = N_PAIRS // N_KV_HEADS
VMEM_LIMIT = 56 * 1024 * 1024

BF16 = jnp.bfloat16
F32 = jnp.float32


def _rms(x, g):
    ms = jnp.mean(x * x, axis=-1, keepdims=True)
    return x * lax.rsqrt(ms + EPS) * g


def _dot(a, b):
    return jnp.dot(a, b, preferred_element_type=F32)


def _dot_nt(a, b):
    return lax.dot_general(a, b, (((1,), (1,)), ((), ())), preferred_element_type=F32)


def _const_spec(shape):
    zeros = (0,) * len(shape)
    return pl.BlockSpec(shape, lambda *_: zeros)


def _layer_spec(arr, layer, single_buffer=False):
    tail = (0,) * (arr.ndim - 1)
    mode = dict(pipeline_mode=pl.Buffered(1)) if single_buffer else {}
    return pl.BlockSpec((None,) + arr.shape[1:], lambda *_: (layer,) + tail, **mode)


def _params(n_axes=1):
    return pltpu.CompilerParams(dimension_semantics=("arbitrary",) * n_axes,
                                vmem_limit_bytes=VMEM_LIMIT)


XB_COL = ATTN_W + 2 * KV_W
GATE_COL = XB_COL + LRU_W


def _store_qkv(z, cos_ref, sin_ref, q_ref, k_ref, v_ref):
    cos = cos_ref[...]
    sin = sin_ref[...]
    lane = lax.broadcasted_iota(jnp.int32, cos.shape, 1)
    first_half = (lane & (HEAD_DIM // 2)) == 0

    def rope(t):
        swapped = jnp.where(first_half,
                            pltpu.roll(t, LANES - HEAD_DIM // 2, 1),
                            pltpu.roll(t, HEAD_DIM // 2, 1))
        return t * cos + swapped * sin

    for p in range(N_PAIRS):
        qp = rope(z[:, p * LANES:(p + 1) * LANES]) * (HEAD_DIM ** -0.5)
        q_ref[:, p * LANES:(p + 1) * LANES] = qp.astype(q_ref.dtype)
    k_ref[...] = rope(z[:, ATTN_W:ATTN_W + KV_W])
    v_ref[...] = z[:, ATTN_W + KV_W:XB_COL]


def _inproj_kernel(x_ref, g_ref, w_ref, cos_ref, sin_ref,
                   q_ref, k_ref, v_ref, xb_ref, gate_ref):
    hn = _rms(x_ref[...], g_ref[...]).astype(BF16)
    z = _dot(hn, w_ref[...])
    _store_qkv(z[:, :XB_COL], cos_ref, sin_ref, q_ref, k_ref, v_ref)
    xb_ref[...] = z[:, XB_COL:GATE_COL]
    gate_ref[...] = z[:, GATE_COL:]


def _inproj(layer, x, g, w, cos, sin, tm, tiles_per_seq, q_dtype):
    t = x.shape[0]
    tok = lambda w_: pl.BlockSpec((tm, w_), lambda i: (i, 0))
    tab = pl.BlockSpec((tm, LANES), lambda i: (i % tiles_per_seq, 0))
    return pl.pallas_call(
        _inproj_kernel,
        grid=(t // tm,),
        in_specs=[tok(D_MODEL), _layer_spec(g, layer), _layer_spec(w, layer), tab, tab],
        out_specs=[tok(ATTN_W), tok(KV_W), tok(KV_W), tok(LRU_W), tok(LRU_W)],
        out_shape=[jax.ShapeDtypeStruct((t, ATTN_W), q_dtype),
                   jax.ShapeDtypeStruct((t, KV_W), F32),
                   jax.ShapeDtypeStruct((t, KV_W), F32),
                   jax.ShapeDtypeStruct((t, LRU_W), F32),
                   jax.ShapeDtypeStruct((t, LRU_W), F32)],
        compiler_params=_params(),
        name="inproj",
    )(x, g, w, cos, sin)


MXU_COLS = 256
FF_SPLIT = (D_FF // MXU_COLS + 1) // 2 * MXU_COLS


def _ffn_kernel(x_ref, att_ref, lru_ref, ga_ref, gl_ref, wo_ref, gpm_ref, gpf_ref,
                wg_ref, wu_ref, wd_ref, gpo_ref, o_ref):
    att = _rms(att_ref[...], ga_ref[...]).astype(BF16)
    lru = _rms(lru_ref[...], gl_ref[...]).astype(BF16)
    m = _dot(att, wo_ref[0:ATTN_W, :]) + _dot(lru, wo_ref[ATTN_W:, :])
    x1 = x_ref[...] + _rms(m, gpm_ref[...])
    hf = _rms(x1, gpf_ref[...]).astype(BF16)
    f = None
    for lo, hi in ((0, FF_SPLIT), (FF_SPLIT, D_FF)):
        gt = _dot(hf, wg_ref[:, lo:hi])
        up = _dot(hf, wu_ref[:, lo:hi])
        hid = (gt * jax.nn.sigmoid(gt) * up).astype(BF16)
        part = _dot(hid, wd_ref[lo:hi, :])
        f = part if f is None else f + part
    o_ref[...] = x1 + _rms(f, gpo_ref[...])


def _ffn(layer, x, att, lru, ga, gl, wo, gpm, gpf, wg, wu, wd, gpo, tm):
    t = x.shape[0]
    tok = lambda w_: pl.BlockSpec((tm, w_), lambda i: (i, 0))
    big = lambda a: _layer_spec(a, layer, single_buffer=True)
    lyr = lambda a: _layer_spec(a, layer)
    return pl.pallas_call(
        _ffn_kernel,
        grid=(t // tm,),
        in_specs=[tok(D_MODEL), tok(ATTN_W), tok(LRU_W), lyr(ga), lyr(gl),
                  big(wo), lyr(gpm), lyr(gpf), big(wg), big(wu), big(wd), lyr(gpo)],
        out_specs=tok(D_MODEL),
        out_shape=jax.ShapeDtypeStruct((t, D_MODEL), F32),
        compiler_params=_params(),
        name="outffn",
    )(x, att, lru, ga, gl, wo, gpm, gpf, wg, wu, wd, gpo)


def _kv_variants(k, v):
    lane = lax.broadcasted_iota(jnp.int32, k.shape, 1)
    low = lane < HEAD_DIM
    kr = pltpu.roll(k, HEAD_DIM, 1)
    vr = pltpu.roll(v, HEAD_DIM, 1)
    zero = jnp.zeros_like(k)
    ka = (jnp.where(low, k, zero), jnp.where(low, kr, zero))
    kb = (jnp.where(low, zero, kr), jnp.where(low, zero, k))
    vv = (jnp.where(low, v, vr), jnp.where(low, vr, v))
    cast = lambda pair: tuple(a.astype(BF16) for a in pair)
    return cast(ka), cast(kb), cast(vv)


def _attn_prompt_kernel(sink_ref, q_ref, k_ref, v_ref, o_ref,
                        ka_ref, kb_ref, vv_ref, *, layer, seq, chunk):
    for c in range(seq // chunk):
        rows = slice(c * chunk, (c + 1) * chunk)
        ka, kb, vv = _kv_variants(k_ref[0, rows, :], v_ref[0, rows, :])
        ones = jnp.ones((chunk, LANES), BF16)
        for h in range(N_KV_HEADS):
            ka_ref[h, rows, :] = ka[h]
            kb_ref[h, rows, :] = kb[h]
            vv_ref[h, rows, :] = jnp.concatenate([vv[h], ones], axis=1)

    def tile(q0, ks, nq, nk, sink_col):
        row = lax.broadcasted_iota(jnp.int32, (nq, nk), 0)
        col = lax.broadcasted_iota(jnp.int32, (nq, nk), 1)
        rel = row - col + (q0 - ks)
        mask = (rel >= 0) & (rel < WINDOW)
        col_row = lax.broadcasted_iota(jnp.int32, (1, nk), 1)
        key_row = lax.broadcasted_iota(jnp.int32, (nk, 2 * LANES), 0)
        val_lane = lax.broadcasted_iota(jnp.int32, (nk, 2 * LANES), 1)
        zero_v = (key_row == sink_col) & (val_lane < LANES)
        vvw = [jnp.where(zero_v, jnp.zeros((), BF16), vv_ref[h, pl.ds(ks, nk), :])
               for h in range(N_KV_HEADS)]
        low = lax.broadcasted_iota(jnp.int32, (nq, LANES), 1) < HEAD_DIM
        for p in range(N_PAIRS):
            h = p // PAIRS_PER_KV
            qp = q_ref[0, pl.ds(q0, nq), p * LANES:(p + 1) * LANES]
            halves = []
            for k_ref_, head in ((ka_ref, 2 * p), (kb_ref, 2 * p + 1)):
                s = _dot_nt(qp, k_ref_[h, pl.ds(ks, nk), :])
                fill = jnp.where(col_row == sink_col, sink_ref[layer, head], -jnp.inf)
                s = jnp.where(mask, s, fill)
                e = jnp.exp(s - jnp.max(s, axis=1, keepdims=True))
                pv = _dot(e.astype(BF16), vvw[h])
                halves.append(pv[:, :LANES] / pv[:, LANES:])
            o_ref[0, pl.ds(q0, nq), p * LANES:(p + 1) * LANES] = jnp.where(low, halves[0], halves[1])

    n_full = seq // WINDOW
    group = 4

    def body(it, carry):
        for t in range(group):
            i = it * group + t
            q0 = pl.multiple_of(i * WINDOW, WINDOW)
            ks = pl.multiple_of(jnp.maximum(q0 - WINDOW, 0), WINDOW)
            sink_col = jnp.where(i == 0, 2 * WINDOW - 1, 0) if t == 0 else 0
            tile(q0, ks, WINDOW, 2 * WINDOW, sink_col)
        return carry

    lax.fori_loop(0, n_full // group, body, 0)
    for i in range(n_full // group * group, n_full):
        tile(i * WINDOW, max(i - 1, 0) * WINDOW, WINDOW, 2 * WINDOW, 2 * WINDOW - 1 if i == 0 else 0)
    if seq > n_full * WINDOW:
        tile(seq - WINDOW, seq - 2 * WINDOW, WINDOW, 2 * WINDOW, 0)


def _attn_prompt(layer, sinks, q, k, v, chunk):
    b, seq, _ = q.shape
    assert seq >= 2 * WINDOW
    kern = functools.partial(_attn_prompt_kernel, layer=layer, seq=seq, chunk=chunk)
    seq_spec = lambda w_: pl.BlockSpec((1, seq, w_), lambda i: (i, 0, 0))
    return pl.pallas_call(
        kern,
        grid=(b,),
        in_specs=[pl.BlockSpec(memory_space=pltpu.SMEM),
                  seq_spec(ATTN_W), seq_spec(KV_W), seq_spec(KV_W)],
        out_specs=seq_spec(ATTN_W),
        out_shape=jax.ShapeDtypeStruct((b, seq, ATTN_W), F32),
        scratch_shapes=[pltpu.VMEM((N_KV_HEADS, seq, LANES), BF16),
                        pltpu.VMEM((N_KV_HEADS, seq, LANES), BF16),
                        pltpu.VMEM((N_KV_HEADS, seq, 2 * LANES), BF16)],
        compiler_params=_params(),
        name="attn_prompt",
    )(sinks, q, k, v)


def _gelu_tanh(x):
    return 0.5 * x * (1.0 + jnp.tanh(0.7978845608028654 * (x + 0.044715 * (x * x * x))))


def _lru_gates(u, wab, bab, logsig):
    ga = _dot(u.astype(BF16), wab) + bab
    r = jax.nn.sigmoid(ga[:, :LRU_W])
    ig = jax.nn.sigmoid(ga[:, LRU_W:])
    log_a = LRU_C * r * logsig
    a = jnp.exp(log_a)
    b = jnp.sqrt(-jnp.tanh(log_a) * (a * a + 1.0)) * ig * u
    return a, b


def _scan_tile(a_ref, b_ref, h0, rows):
    piece = rows // SUBLANES
    at = lambda i: pl.ds(i, SUBLANES, stride=piece)
    hs = [jnp.zeros((SUBLANES, LANES), F32) for _ in range(N_SLABS)]
    ps = [jnp.ones((SUBLANES, LANES), F32) for _ in range(N_SLABS)]
    for i in range(piece):
        for s in range(N_SLABS):
            a = a_ref[s, at(i), :]
            hs[s] = a * hs[s] + b_ref[s, at(i), :]
            ps[s] = a * ps[s]
            b_ref[s, at(i), :] = hs[s]
            a_ref[s, at(i), :] = ps[s]

    sub = lax.broadcasted_iota(jnp.int32, (SUBLANES, LANES), 0)
    carry_in, h_last = [], []
    for s in range(N_SLABS):
        cin = jnp.broadcast_to(h0[:, s * LANES:(s + 1) * LANES], (SUBLANES, LANES))
        acc = jnp.zeros((SUBLANES, LANES), F32)
        for c in range(SUBLANES):
            acc = jnp.where(sub == c, cin, acc)
            end = hs[s] + ps[s] * cin
            cin = jnp.broadcast_to(end[c:c + 1, :], (SUBLANES, LANES))
        carry_in.append(acc)
        h_last.append(cin[0:1, :])

    for i in range(piece):
        for s in range(N_SLABS):
            b_ref[s, at(i), :] = b_ref[s, at(i), :] + a_ref[s, at(i), :] * carry_in[s]
    return jnp.concatenate(h_last, axis=1)


def _inproj_lru_kernel(x_ref, g_ref, w_ref, cos_ref, sin_ref, cw_ref, cb_ref, wab_ref, bab_ref,
                       lam_ref, q_ref, k_ref, v_ref, o_ref, tail_ref, ht_ref,
                       xpad_ref, a_ref, b_ref, h_ref, *, tm, tiles_per_seq):
    pad = SUBLANES

    @pl.when(pl.program_id(0) % tiles_per_seq == 0)
    def _():
        xpad_ref[0:pad, :] = jnp.zeros((pad, LRU_W), F32)
        h_ref[...] = jnp.zeros((1, LRU_W), F32)

    hn = _rms(x_ref[...], g_ref[...]).astype(BF16)
    xb = _dot(hn, w_ref[:, XB_COL:GATE_COL])
    gate = _dot(hn, w_ref[:, GATE_COL:])

    xpad_ref[pad:pad + tm, :] = xb
    u = cb_ref[...]
    for j in range(CONV_W):
        start = pad - (CONV_W - 1) + j
        u = u + xpad_ref[start:start + tm, :] * cw_ref[j:j + 1, :]
    tail = xb[tm - pad:, :]
    tail_ref[0] = tail
    xpad_ref[0:pad, :] = tail

    a, b = _lru_gates(u, wab_ref[...], bab_ref[...], jax.nn.log_sigmoid(lam_ref[...]))
    for s in range(N_SLABS):
        a_ref[s] = a[:, s * LANES:(s + 1) * LANES]
        b_ref[s] = b[:, s * LANES:(s + 1) * LANES]
    h_last = _scan_tile(a_ref, b_ref, h_ref[...], tm)
    h_ref[...] = h_last
    ht_ref[0] = h_last
    for s in range(N_SLABS):
        cols = slice(s * LANES, (s + 1) * LANES)
        o_ref[:, cols] = b_ref[s] * _gelu_tanh(gate[:, cols])

    _store_qkv(_dot(hn, w_ref[:, :XB_COL]), cos_ref, sin_ref, q_ref, k_ref, v_ref)


def _inproj_lru(layer, x, g, w, cos, sin, cw, cb, wab, bab, lam, tm, tiles_per_seq):
    t = x.shape[0]
    n_seq = t // (tm * tiles_per_seq)
    assert tm % SUBLANES == 0 and tm >= SUBLANES
    kern = functools.partial(_inproj_lru_kernel, tm=tm, tiles_per_seq=tiles_per_seq)
    tok = lambda w_: pl.BlockSpec((tm, w_), lambda i: (i, 0))
    tab = pl.BlockSpec((tm, LANES), lambda i: (i % tiles_per_seq, 0))
    per_seq = lambda r: pl.BlockSpec((1, r, LRU_W), lambda i: (i // tiles_per_seq, 0, 0))
    lyr = lambda a: _layer_spec(a, layer)
    return pl.pallas_call(
        kern,
        grid=(t // tm,),
        in_specs=[tok(D_MODEL), lyr(g), lyr(w), tab, tab,
                  lyr(cw), lyr(cb), lyr(wab), lyr(bab), lyr(lam)],
        out_specs=[tok(ATTN_W), tok(KV_W), tok(KV_W), tok(LRU_W), per_seq(SUBLANES), per_seq(1)],
        out_shape=[jax.ShapeDtypeStruct((t, ATTN_W), BF16),
                   jax.ShapeDtypeStruct((t, KV_W), F32),
                   jax.ShapeDtypeStruct((t, KV_W), F32),
                   jax.ShapeDtypeStruct((t, LRU_W), F32),
                   jax.ShapeDtypeStruct((n_seq, SUBLANES, LRU_W), F32),
                   jax.ShapeDtypeStruct((n_seq, 1, LRU_W), F32)],
        scratch_shapes=[pltpu.VMEM((tm + SUBLANES, LRU_W), F32),
                        pltpu.VMEM((N_SLABS, tm, LANES), F32),
                        pltpu.VMEM((N_SLABS, tm, LANES), F32),
                        pltpu.VMEM((1, LRU_W), F32)],
        compiler_params=_params(),
        name="inproj_lru",
    )(x, g, w, cos, sin, cw, cb, wab, bab, lam)


def _div_pow2(x, n):
    assert n & (n - 1) == 0
    return x >> (n.bit_length() - 1)


def _mod_pow2(x, n):
    assert n & (n - 1) == 0
    return x & (n - 1)


def _attn_sample_kernel(sink_ref, q_ref, kn_ref, vn_ref, ck_ref, cv_ref,
                        o_ref, nk_ref, nv_ref, *, layer, group, n_new):
    rpb = PAIRS_PER_KV * n_new
    new_pad = kn_ref.shape[1]
    rows = group * rpb
    ncols = group * new_pad
    row = lax.broadcasted_iota(jnp.int32, (rows, 1), 0)
    qpos = _mod_pow2(row, n_new)
    row_seq = _div_pow2(row, rpb)
    second_pair = _mod_pow2(_div_pow2(row, n_new), PAIRS_PER_KV) == 1
    col = lax.broadcasted_iota(jnp.int32, (1, WINDOW), 1)
    mask_c = col > qpos
    if PAST_LEN < WINDOW:
        mask_c = mask_c & (col >= WINDOW - PAST_LEN)
    ncol = lax.broadcasted_iota(jnp.int32, (1, ncols), 1)
    mask_n = (_div_pow2(ncol, new_pad) == row_seq) & (_mod_pow2(ncol, new_pad) <= qpos)
    low = lax.broadcasted_iota(jnp.int32, (rpb, LANES), 1) < HEAD_DIM
    zero_half = jnp.zeros((HEAD_DIM, LANES), BF16)

    kn_all = kn_ref[...].reshape(ncols, KV_W)
    vn_all = vn_ref[...].reshape(ncols, KV_W)
    knt_all = kn_all.T
    vnt_all = vn_all.T
    vn_lane = lax.broadcasted_iota(jnp.int32, vn_all.shape, 1) < HEAD_DIM
    vn_roll = pltpu.roll(vn_all, HEAD_DIM, 1)

    keep = WINDOW - n_new
    cache_lane = lax.broadcasted_iota(jnp.int32, (HEAD_DIM, WINDOW), 1) < keep

    for h in range(N_KV_HEADS):
        hd = slice(h * HEAD_DIM, (h + 1) * HEAD_DIM)
        q_all = q_ref[:, h].reshape(rows, LANES).astype(BF16)
        q_seq = [q_ref[b, h].astype(BF16) for b in range(group)]
        knt_h = knt_all[hd, :].astype(BF16)
        vvn = (jnp.where(vn_lane, vn_all, vn_roll) if h == 0
               else jnp.where(vn_lane, vn_roll, vn_all)).astype(BF16)
        kt, vt2 = [], []
        for b in range(group):
            kt.append(ck_ref[b, h].astype(BF16))
            vt = cv_ref[b, h].astype(BF16)
            vt2.append(jnp.concatenate([vt, vt], axis=0))
        probs, dens = [], []
        for first in (True, False):
            embed = lambda t: jnp.concatenate([t, zero_half] if first else [zero_half, t], axis=0)
            s_new = jnp.where(mask_n, _dot(q_all, embed(knt_h)), -jnp.inf)
            s_c = jnp.concatenate(
                [_dot(q_seq[b], embed(kt[b])) for b in range(group)], axis=0)
            s_c = jnp.where(mask_c, s_c, -jnp.inf)
            pair0 = 2 * h * PAIRS_PER_KV + (0 if first else 1)
            sink = jnp.where(second_pair, sink_ref[layer,pair0 + 2], sink_ref[layer,pair0])
            mx = jnp.maximum(jnp.maximum(jnp.max(s_c, axis=1, keepdims=True),
                                         jnp.max(s_new, axis=1, keepdims=True)), sink)
            p_c = jnp.exp(s_c - mx)
            p_n = jnp.exp(s_new - mx)
            dens.append(jnp.sum(p_c, axis=1, keepdims=True) + jnp.sum(p_n, axis=1, keepdims=True)
                        + jnp.exp(sink - mx))
            probs.append((p_c, _dot(p_n.astype(BF16), vvn)))
        for b in range(group):
            rs = slice(b * rpb, (b + 1) * rpb)
            pc = jnp.concatenate([probs[0][0][rs], probs[1][0][rs]], axis=0).astype(BF16)
            oc = _dot_nt(pc, vt2[b])
            o_first = (oc[:rpb] + probs[0][1][rs]) / dens[0][rs]
            o_second = (oc[rpb:] + probs[1][1][rs]) / dens[1][rs]
            o_ref[b, h] = jnp.where(low, o_first, o_second)
            shift = (keep - b * new_pad) % LANES
            nk_ref[b, h] = jnp.where(cache_lane, pltpu.roll(ck_ref[b, h], keep, 1),
                                     pltpu.roll(knt_all[hd, :], shift, 1))
            nv_ref[b, h] = jnp.where(cache_lane, pltpu.roll(cv_ref[b, h], keep, 1),
                                     pltpu.roll(vnt_all[hd, :], shift, 1))


def _attn_sample(layer, sinks, q, kn, vn, ck, cv, group, n_new):
    bs = q.shape[0]
    kern = functools.partial(_attn_sample_kernel, layer=layer, group=group, n_new=n_new)
    seq4 = lambda a: pl.BlockSpec((group,) + a.shape[1:], lambda i: (i, 0, 0, 0))
    seq3 = lambda a: pl.BlockSpec((group,) + a.shape[1:], lambda i: (i, 0, 0))
    cache = pl.BlockSpec((None, group) + ck.shape[2:], lambda i: (layer, i, 0, 0, 0))
    cache_out = pl.BlockSpec((group,) + ck.shape[2:], lambda i: (i, 0, 0, 0))
    return pl.pallas_call(
        kern,
        grid=(bs // group,),
        in_specs=[pl.BlockSpec(memory_space=pltpu.SMEM),
                  seq4(q), seq3(kn), seq3(vn), cache, cache],
        out_specs=[seq4(q), cache_out, cache_out],
        out_shape=[jax.ShapeDtypeStruct(q.shape, F32),
                   jax.ShapeDtypeStruct(ck.shape[1:], F32),
                   jax.ShapeDtypeStruct(cv.shape[1:], F32)],
        compiler_params=_params(),
        name="attn_sample",
    )(sinks, q, kn, vn, ck, cv)


def _lru_sample_kernel(xb_ref, gate_ref, conv0_ref, h0_ref, cw_ref, cb_ref, wab_ref,
                       bab_ref, lam_ref, lru_o_ref, ht_ref, *, n_new):
    logsig = jax.nn.log_sigmoid(lam_ref[...])
    wab = wab_ref[...]
    bab = bab_ref[...]
    xs = [conv0_ref[j] for j in range(CONV_W - 1)] + [xb_ref[t] for t in range(n_new)]
    h = h0_ref[...]
    for t in range(n_new):
        u = cb_ref[...]
        for j in range(CONV_W):
            u = u + xs[t + j] * cw_ref[j:j + 1, :]
        a, b = _lru_gates(u, wab, bab, logsig)
        h = a * h + b
        lru_o_ref[t] = h * _gelu_tanh(gate_ref[t])
    ht_ref[...] = h


def _lru_sample(layer, xb_t, gate_t, conv0_t, h0, cw, cb, wab, bab, lam):
    n_new = xb_t.shape[0]
    kern = functools.partial(_lru_sample_kernel, n_new=n_new)
    lyr = lambda a: _layer_spec(a, layer)
    return pl.pallas_call(
        kern,
        grid=(1,),
        in_specs=[_const_spec(xb_t.shape), _const_spec(gate_t.shape),
                  lyr(conv0_t), lyr(h0), lyr(cw), lyr(cb), lyr(wab), lyr(bab), lyr(lam)],
        out_specs=[_const_spec(xb_t.shape), _const_spec(h0.shape[1:])],
        out_shape=[jax.ShapeDtypeStruct(xb_t.shape, F32),
                   jax.ShapeDtypeStruct(h0.shape[1:], F32)],
        compiler_params=_params(),
        name="lru_sample",
    )(xb_t, gate_t, conv0_t, h0, cw, cb, wab, bab, lam)


def _rope_tables(pos):
    half = HEAD_DIM // 2
    inv = ROPE_THETA ** (-jnp.arange(half, dtype=F32) / half)
    ang = pos.astype(F32)[:, None] * inv[None, :]
    cos, sin = jnp.cos(ang), jnp.sin(ang)
    reps = LANES // HEAD_DIM
    return (jnp.concatenate([cos, cos] * reps, axis=1),
            jnp.concatenate([-sin, sin] * reps, axis=1))


def _block_diag(w):
    eye = jnp.eye(LRU_BLOCKS, dtype=w.dtype)
    full = w[:, :, :, None, :] * eye[None, :, None, :, None]
    return full.reshape(w.shape[0], LRU_W, LRU_W)


def _pick_tile(n, candidates):
    for c in candidates:
        if n % c == 0:
            return c
    raise ValueError(f"no tile for {n} tokens")


def kernel(x_prompt, x_sample, cache_k, cache_v, state_h, state_conv, meta_tokens, pre_mix_norm,
           w_in, sinks, conv_w, conv_b, w_a, b_a, w_i, b_i, lam, attn_out_norm, lru_out_norm,
           w_out, post_mix_norm, pre_ffn_norm, w_gate, w_up, w_down, post_ffn_norm):
    bp, seq_in, _ = x_prompt.shape
    seq = seq_in + N_META
    bs, n_new, _ = x_sample.shape
    assert seq % SUBLANES == 0 and (seq % WINDOW) % 16 == 0 and n_new >= CONV_W - 1

    vec = lambda a: a.reshape(DEPTH, 1, -1)
    w_in_b = w_in.astype(BF16)
    w_out_b = w_out.astype(BF16)
    w_gate_b = w_gate.astype(BF16)
    w_up_b = w_up.astype(BF16)
    w_down_b = w_down.astype(BF16)
    wab = jnp.concatenate([_block_diag(w_a), _block_diag(w_i)], axis=2).astype(BF16)
    bab = vec(jnp.concatenate([b_a, b_i], axis=1))
    g_pre, g_att, g_lru = vec(pre_mix_norm), vec(attn_out_norm), vec(lru_out_norm)
    g_pm, g_pf, g_po = vec(post_mix_norm), vec(pre_ffn_norm), vec(post_ffn_norm)
    cb, lam_v = vec(conv_b), vec(lam)

    meta = jnp.broadcast_to(meta_tokens.astype(x_prompt.dtype)[None], (bp, N_META, D_MODEL))
    xp = jnp.concatenate([meta, x_prompt], axis=1).reshape(bp * seq, D_MODEL)
    xs = x_sample.reshape(bs * n_new, D_MODEL)

    cos_p, sin_p = _rope_tables(jnp.arange(seq, dtype=jnp.int32))
    pos_s = PAST_LEN + jnp.arange(n_new, dtype=jnp.int32)
    cos_s, sin_s = _rope_tables(jnp.tile(pos_s, bs))

    tm_p = _pick_tile(seq, (688, 512, 256, 128, 16))
    tm_s = bs * n_new
    tm_f = _pick_tile(bp * seq, (688, 384, 256, 128, 16))
    conv0_p = jnp.zeros((bp, SUBLANES, LRU_W), F32)
    h0_p = jnp.zeros((bp, 1, LRU_W), F32)
    new_pad = SUBLANES
    group = LANES // new_pad
    assert n_new <= new_pad and bs % group == 0 and WINDOW == LANES

    ck_t = cache_k.transpose(0, 1, 3, 4, 2)
    cv_t = cache_v.transpose(0, 1, 3, 4, 2)
    conv_s = state_conv.transpose(0, 2, 1, 3)

    pk, pv, ph, pc, sk, sv, sh, sc = [], [], [], [], [], [], [], []
    for l in range(DEPTH):
        q, k, v, lru, xb_tail, ht = _inproj_lru(l, xp, g_pre, w_in_b, cos_p, sin_p,
                                                conv_w, cb, wab, bab, lam_v, tm_p, seq // tm_p)
        k3 = k.reshape(bp, seq, KV_W)
        v3 = v.reshape(bp, seq, KV_W)
        att = _attn_prompt(l, sinks, q.reshape(bp, seq, ATTN_W), k3, v3, tm_p)
        xp = _ffn(l, xp, att.reshape(bp * seq, ATTN_W), lru,
                  g_att, g_lru, w_out_b, g_pm, g_pf, w_gate_b, w_up_b, w_down_b, g_po, tm_f)
        pk.append(k3[:, -WINDOW:].reshape(bp, WINDOW, N_KV_HEADS, HEAD_DIM))
        pv.append(v3[:, -WINDOW:].reshape(bp, WINDOW, N_KV_HEADS, HEAD_DIM))
        ph.append(ht.reshape(bp, LRU_W))
        pc.append(xb_tail[:, -(CONV_W - 1):])

        q, k, v, xb, gate = _inproj(l, xs, g_pre, w_in_b, cos_s, sin_s, tm_s, 1, F32)
        q4 = q.reshape(bs, n_new, N_KV_HEADS, PAIRS_PER_KV, LANES).transpose(0, 2, 3, 1, 4).reshape(
            bs, N_KV_HEADS, PAIRS_PER_KV * n_new, LANES)
        padn = ((0, 0), (0, new_pad - n_new), (0, 0))
        o4, nk, nv = _attn_sample(l, sinks, q4,
                                  jnp.pad(k.reshape(bs, n_new, KV_W), padn),
                                  jnp.pad(v.reshape(bs, n_new, KV_W), padn),
                                  ck_t, cv_t, group, n_new)
        att_raw = o4.reshape(bs, N_KV_HEADS, PAIRS_PER_KV, n_new, LANES).transpose(
            0, 3, 1, 2, 4).reshape(bs * n_new, ATTN_W)
        xb3 = xb.reshape(bs, n_new, LRU_W)
        tmaj = lambda a: a.transpose(1, 0, 2)
        lru_t, ht = _lru_sample(l, tmaj(xb3), tmaj(gate.reshape(bs, n_new, LRU_W)),
                                conv_s, state_h, conv_w, cb, wab, bab, lam_v)
        xs = _ffn(l, xs, att_raw, tmaj(lru_t).reshape(bs * n_new, LRU_W),
                  g_att, g_lru, w_out_b, g_pm, g_pf, w_gate_b, w_up_b, w_down_b, g_po, tm_s)
        sk.append(nk.transpose(0, 3, 1, 2))
        sv.append(nv.transpose(0, 3, 1, 2))
        sh.append(ht)
        sc.append(jnp.concatenate([state_conv[l], xb3], axis=1)[:, -(CONV_W - 1):])

    y_prompt = xp.reshape(bp, seq, D_MODEL)[:, N_META:]
    y_sample = xs.reshape(bs, n_new, D_MODEL)
    return (y_prompt, y_sample, jnp.stack(pk), jnp.stack(pv), jnp.stack(ph), jnp.stack(pc),
            jnp.stack(sk), jnp.stack(sv), jnp.stack(sh), jnp.stack(sc))
```

```python
import functools

import jax
import jax.numpy as jnp
from jax import lax
from jax.experimental import pallas as pl
from jax.experimental.pallas import tpu as pltpu

D_MODEL = 1024
DEPTH = 4
PAST_LEN = 8192
N_META = 16
HEAD_DIM = 64
N_Q_HEADS = 8
N_KV_HEADS = 2
ATTN_W = N_Q_HEADS * HEAD_DIM
KV_W = N_KV_HEADS * HEAD_DIM
LRU_W = D_MODEL - ATTN_W
LRU_BLOCKS = 8
LRU_BLOCK_W = LRU_W // LRU_BLOCKS
CONV_W = 4
LRU_C = 8.0
WINDOW = 128
ROPE_THETA = 10000.0
D_FF = 2816
IN_W = ATTN_W + 2 * KV_W + 2 * LRU_W
EPS = 1e-6

LANES = 128
SUBLANES = 8
N_SLABS = LRU_W // LANES
N_PAIRS = ATTN_W // LANES
PAIRS_PER_KV = N_PAIRS // N_KV_HEADS
VMEM_LIMIT = 56 * 1024 * 1024

BF16 = jnp.bfloat16
F32 = jnp.float32


def _rms(x, g):
    ms = jnp.mean(x * x, axis=-1, keepdims=True)
    return x * lax.rsqrt(ms + EPS) * g


def _dot(a, b):
    return jnp.dot(a, b, preferred_element_type=F32)


def _dot_nt(a, b):
    return lax.dot_general(a, b, (((1,), (1,)), ((), ())), preferred_element_type=F32)


def _const_spec(shape):
    zeros = (0,) * len(shape)
    return pl.BlockSpec(shape, lambda *_: zeros)


def _layer_spec(arr, layer, single_buffer=False):
    tail = (0,) * (arr.ndim - 1)
    mode = dict(pipeline_mode=pl.Buffered(1)) if single_buffer else {}
    return pl.BlockSpec((None,) + arr.shape[1:], lambda *_: (layer,) + tail, **mode)


def _params(n_axes=1):
    return pltpu.CompilerParams(dimension_semantics=("arbitrary",) * n_axes,
                                vmem_limit_bytes=VMEM_LIMIT)


XB_COL = ATTN_W + 2 * KV_W
GATE_COL = XB_COL + LRU_W


def _store_qkv(z, cos_ref, sin_ref, q_ref, k_ref, v_ref):
    cos = cos_ref[...]
    sin = sin_ref[...]
    lane = lax.broadcasted_iota(jnp.int32, cos.shape, 1)
    first_half = (lane & (HEAD_DIM // 2)) == 0

    def rope(t):
        swapped = jnp.where(first_half,
                            pltpu.roll(t, LANES - HEAD_DIM // 2, 1),
                            pltpu.roll(t, HEAD_DIM // 2, 1))
        return t * cos + swapped * sin

    for p in range(N_PAIRS):
        qp = rope(z[:, p * LANES:(p + 1) * LANES]) * (HEAD_DIM ** -0.5)
        q_ref[:, p * LANES:(p + 1) * LANES] = qp.astype(q_ref.dtype)
    k_ref[...] = rope(z[:, ATTN_W:ATTN_W + KV_W])
    v_ref[...] = z[:, ATTN_W + KV_W:XB_COL]


def _inproj_kernel(x_ref, g_ref, w_ref, cos_ref, sin_ref,
                   q_ref, k_ref, v_ref, xb_ref, gate_ref):
    hn = _rms(x_ref[...], g_ref[...]).astype(BF16)
    z = _dot(hn, w_ref[...])
    _store_qkv(z[:, :XB_COL], cos_ref, sin_ref, q_ref, k_ref, v_ref)
    xb_ref[...] = z[:, XB_COL:GATE_COL]
    gate_ref[...] = z[:, GATE_COL:]


def _inproj(layer, x, g, w, cos, sin, tm, tiles_per_seq, q_dtype):
    t = x.shape[0]
    tok = lambda w_: pl.BlockSpec((tm, w_), lambda i: (i, 0))
    tab = pl.BlockSpec((tm, LANES), lambda i: (i % tiles_per_seq, 0))
    return pl.pallas_call(
        _inproj_kernel,
        grid=(t // tm,),
        in_specs=[tok(D_MODEL), _layer_spec(g, layer), _layer_spec(w, layer), tab, tab],
        out_specs=[tok(ATTN_W), tok(KV_W), tok(KV_W), tok(LRU_W), tok(LRU_W)],
        out_shape=[jax.ShapeDtypeStruct((t, ATTN_W), q_dtype),
                   jax.ShapeDtypeStruct((t, KV_W), F32),
                   jax.ShapeDtypeStruct((t, KV_W), F32),
                   jax.ShapeDtypeStruct((t, LRU_W), F32),
                   jax.ShapeDtypeStruct((t, LRU_W), F32)],
        compiler_params=_params(),
        name="inproj",
    )(x, g, w, cos, sin)


MXU_COLS = 256
FF_SPLIT = (D_FF // MXU_COLS + 1) // 2 * MXU_COLS


def _gelu_tanh(x):
    return 0.5 * x * (1.0 + jnp.tanh(0.7978845608028654 * (x + 0.044715 * (x * x * x))))


def _ffn_kernel(x_ref, att_ref, h_ref, gate_ref, ga_ref, gl_ref, wo_ref, gpm_ref, gpf_ref,
                wg_ref, wu_ref, wd_ref, gpo_ref, o_ref):
    att = _rms(att_ref[...], ga_ref[...]).astype(BF16)
    h = jnp.concatenate([h_ref[s] for s in range(N_SLABS)], axis=1)
    lru = _rms(h * _gelu_tanh(gate_ref[...]), gl_ref[...]).astype(BF16)
    m = _dot(att, wo_ref[0:ATTN_W, :]) + _dot(lru, wo_ref[ATTN_W:, :])
    x1 = x_ref[...] + _rms(m, gpm_ref[...])
    hf = _rms(x1, gpf_ref[...]).astype(BF16)
    f = None
    for lo, hi in ((0, FF_SPLIT), (FF_SPLIT, D_FF)):
        gt = _dot(hf, wg_ref[:, lo:hi])
        up = _dot(hf, wu_ref[:, lo:hi])
        hid = (gt * jax.nn.sigmoid(gt) * up).astype(BF16)
        part = _dot(hid, wd_ref[lo:hi, :])
        f = part if f is None else f + part
    o_ref[...] = x1 + _rms(f, gpo_ref[...])


def _ffn(layer, x, att, h, gate, ga, gl, wo, gpm, gpf, wg, wu, wd, gpo, tm):
    t = x.shape[0]
    tok = lambda w_: pl.BlockSpec((tm, w_), lambda i: (i, 0))
    big = lambda a: _layer_spec(a, layer, single_buffer=True)
    lyr = lambda a: _layer_spec(a, layer)
    return pl.pallas_call(
        _ffn_kernel,
        grid=(t // tm,),
        in_specs=[tok(D_MODEL), tok(ATTN_W),
                  pl.BlockSpec((N_SLABS, tm, LANES), lambda i: (0, i, 0)), tok(LRU_W),
                  lyr(ga), lyr(gl),
                  big(wo), lyr(gpm), lyr(gpf), big(wg), big(wu), big(wd), lyr(gpo)],
        out_specs=tok(D_MODEL),
        out_shape=jax.ShapeDtypeStruct((t, D_MODEL), F32),
        compiler_params=_params(),
        name="outffn",
    )(x, att, h, gate, ga, gl, wo, gpm, gpf, wg, wu, wd, gpo)


def _kv_variants(k, v):
    lane = lax.broadcasted_iota(jnp.int32, k.shape, 1)
    low = lane < HEAD_DIM
    kr = pltpu.roll(k, HEAD_DIM, 1)
    vr = pltpu.roll(v, HEAD_DIM, 1)
    zero = jnp.zeros_like(k)
    ka = (jnp.where(low, k, zero), jnp.where(low, kr, zero))
    kb = (jnp.where(low, zero, kr), jnp.where(low, zero, k))
    vv = (jnp.where(low, v, vr), jnp.where(low, vr, v))
    cast = lambda pair: tuple(a.astype(BF16) for a in pair)
    return cast(ka), cast(kb), cast(vv)


def _attn_prompt_kernel(sink_ref, q_ref, k_ref, v_ref, o_ref,
                        ka_ref, kb_ref, vv_ref, *, layer, seq, chunk):
    for c in range(seq // chunk):
        rows = slice(c * chunk, (c + 1) * chunk)
        ka, kb, vv = _kv_variants(k_ref[0, rows, :], v_ref[0, rows, :])
        ones = jnp.ones((chunk, LANES), BF16)
        for h in range(N_KV_HEADS):
            ka_ref[h, rows, :] = ka[h]
            kb_ref[h, rows, :] = kb[h]
            vv_ref[h, rows, :] = jnp.concatenate([vv[h], ones], axis=1)

    def tile(q0, ks, nq, nk, sink_col):
        row = lax.broadcasted_iota(jnp.int32, (nq, nk), 0)
        col = lax.broadcasted_iota(jnp.int32, (nq, nk), 1)
        rel = row - col + (q0 - ks)
        mask = (rel >= 0) & (rel < WINDOW)
        col_row = lax.broadcasted_iota(jnp.int32, (1, nk), 1)
        key_row = lax.broadcasted_iota(jnp.int32, (nk, 2 * LANES), 0)
        val_lane = lax.broadcasted_iota(jnp.int32, (nk, 2 * LANES), 1)
        zero_v = (key_row == sink_col) & (val_lane < LANES)
        vvw = [jnp.where(zero_v, jnp.zeros((), BF16), vv_ref[h, pl.ds(ks, nk), :])
               for h in range(N_KV_HEADS)]
        low = lax.broadcasted_iota(jnp.int32, (nq, LANES), 1) < HEAD_DIM
        for p in range(N_PAIRS):
            h = p // PAIRS_PER_KV
            qp = q_ref[0, pl.ds(q0, nq), p * LANES:(p + 1) * LANES]
            halves = []
            for k_ref_, head in ((ka_ref, 2 * p), (kb_ref, 2 * p + 1)):
                s = _dot_nt(qp, k_ref_[h, pl.ds(ks, nk), :])
                fill = jnp.where(col_row == sink_col, sink_ref[layer, head], -jnp.inf)
                s = jnp.where(mask, s, fill)
                e = jnp.exp(s - jnp.max(s, axis=1, keepdims=True))
                pv = _dot(e.astype(BF16), vvw[h])
                halves.append(pv[:, :LANES] / pv[:, LANES:])
            o_ref[0, pl.ds(q0, nq), p * LANES:(p + 1) * LANES] = jnp.where(low, halves[0], halves[1])

    n_full = seq // WINDOW
    group = 4

    def body(it, carry):
        for t in range(group):
            i = it * group + t
            q0 = pl.multiple_of(i * WINDOW, WINDOW)
            ks = pl.multiple_of(jnp.maximum(q0 - WINDOW, 0), WINDOW)
            sink_col = jnp.where(i == 0, 2 * WINDOW - 1, 0) if t == 0 else 0
            tile(q0, ks, WINDOW, 2 * WINDOW, sink_col)
        return carry

    lax.fori_loop(0, n_full // group, body, 0)
    for i in range(n_full // group * group, n_full):
        tile(i * WINDOW, max(i - 1, 0) * WINDOW, WINDOW, 2 * WINDOW, 2 * WINDOW - 1 if i == 0 else 0)
    if seq > n_full * WINDOW:
        tile(seq - WINDOW, seq - 2 * WINDOW, WINDOW, 2 * WINDOW, 0)


def _attn_prompt(layer, sinks, q, k, v, chunk):
    b, seq, _ = q.shape
    assert seq >= 2 * WINDOW
    kern = functools.partial(_attn_prompt_kernel, layer=layer, seq=seq, chunk=chunk)
    seq_spec = lambda w_: pl.BlockSpec((1, seq, w_), lambda i: (i, 0, 0))
    return pl.pallas_call(
        kern,
        grid=(b,),
        in_specs=[pl.BlockSpec(memory_space=pltpu.SMEM),
                  seq_spec(ATTN_W), seq_spec(KV_W), seq_spec(KV_W)],
        out_specs=seq_spec(ATTN_W),
        out_shape=jax.ShapeDtypeStruct((b, seq, ATTN_W), F32),
        scratch_shapes=[pltpu.VMEM((N_KV_HEADS, seq, LANES), BF16),
                        pltpu.VMEM((N_KV_HEADS, seq, LANES), BF16),
                        pltpu.VMEM((N_KV_HEADS, seq, 2 * LANES), BF16)],
        compiler_params=_params(),
        name="attn_prompt",
    )(sinks, q, k, v)


def _lru_gates(u, wab, bab, logsig):
    ga = _dot(u.astype(BF16), wab) + bab
    r = jax.nn.sigmoid(ga[:, :LRU_W])
    ig = jax.nn.sigmoid(ga[:, LRU_W:])
    log_a = LRU_C * r * logsig
    a = jnp.exp(log_a)
    b = jnp.sqrt(-jnp.tanh(log_a) * (a * a + 1.0)) * ig * u
    return a, b


def _scan_interleaved(a_ref, b_ref, h0, rows):
    piece = rows // SUBLANES
    at = lambda i: slice(i * SUBLANES, (i + 1) * SUBLANES)
    hs = [jnp.zeros((SUBLANES, LANES), F32) for _ in range(N_SLABS)]
    ps = [jnp.ones((SUBLANES, LANES), F32) for _ in range(N_SLABS)]
    for i in range(piece):
        for s in range(N_SLABS):
            a = a_ref[s, at(i), :]
            hs[s] = a * hs[s] + b_ref[s, at(i), :]
            ps[s] = a * ps[s]
            b_ref[s, at(i), :] = hs[s]
            a_ref[s, at(i), :] = ps[s]

    sub = lax.broadcasted_iota(jnp.int32, (SUBLANES, LANES), 0)
    carry_in, h_last = [], []
    for s in range(N_SLABS):
        cin = jnp.broadcast_to(h0[:, s * LANES:(s + 1) * LANES], (SUBLANES, LANES))
        acc = jnp.zeros((SUBLANES, LANES), F32)
        for c in range(SUBLANES):
            acc = jnp.where(sub == c, cin, acc)
            end = hs[s] + ps[s] * cin
            cin = jnp.broadcast_to(end[c:c + 1, :], (SUBLANES, LANES))
        carry_in.append(acc)
        h_last.append(cin[0:1, :])

    for i in range(piece):
        for s in range(N_SLABS):
            b_ref[s, at(i), :] = b_ref[s, at(i), :] + a_ref[s, at(i), :] * carry_in[s]
    return jnp.concatenate(h_last, axis=1)


def _inproj_lru_kernel(x_ref, g_ref, w_ref, cos_ref, sin_ref, cw_ref, cb_ref, wab_ref, bab_ref,
                       lam_ref, q_ref, k_ref, v_ref, h_out_ref, gate_ref, tail_ref, ht_ref,
                       xs_ref, u_ref, a_ref, b_ref, h_ref, *, tm, tiles_per_seq):
    pad = SUBLANES
    piece = tm // SUBLANES
    slab = lambda s: slice(s * LANES, (s + 1) * LANES)

    @pl.when(pl.program_id(0) % tiles_per_seq == 0)
    def _():
        xs_ref[:, 0:pad, :] = jnp.zeros((N_SLABS, pad, LANES), F32)
        h_ref[...] = jnp.zeros((1, LRU_W), F32)

    hn = _rms(x_ref[...], g_ref[...]).astype(BF16)
    xb = _dot(hn, w_ref[:, XB_COL:GATE_COL])
    gate_ref[...] = _dot(hn, w_ref[:, GATE_COL:])
    tail_ref[0] = xb[tm - pad:, :]
    for s in range(N_SLABS):
        xs_ref[s, pad:pad + tm, :] = xb[:, slab(s)]

    for s in range(N_SLABS):
        bias = jnp.broadcast_to(cb_ref[:, slab(s)], (SUBLANES, LANES))
        wts = [jnp.broadcast_to(cw_ref[j:j + 1, slab(s)], (SUBLANES, LANES))
               for j in range(CONV_W)]
        taps = [xs_ref[s, pl.ds(pad - (CONV_W - 1) + k, SUBLANES, stride=piece), :]
                for k in range(CONV_W - 1)]
        for i in range(piece):
            taps.append(xs_ref[s, pl.ds(pad + i, SUBLANES, stride=piece), :])
            u = bias
            for j in range(CONV_W):
                u = u + taps[j] * wts[j]
            u_ref[i * SUBLANES:(i + 1) * SUBLANES, slab(s)] = u
            taps.pop(0)
    for s in range(N_SLABS):
        xs_ref[s, 0:pad, :] = xs_ref[s, tm:tm + pad, :]

    a, b = _lru_gates(u_ref[...], wab_ref[...], bab_ref[...], jax.nn.log_sigmoid(lam_ref[...]))
    for s in range(N_SLABS):
        a_ref[s] = a[:, slab(s)]
        b_ref[s] = b[:, slab(s)]
    h_last = _scan_interleaved(a_ref, b_ref, h_ref[...], tm)
    h_ref[...] = h_last
    ht_ref[0] = h_last
    for s in range(N_SLABS):
        for i in range(piece):
            h_out_ref[s, pl.ds(i, SUBLANES, stride=piece), :] = (
                b_ref[s, i * SUBLANES:(i + 1) * SUBLANES, :])

    _store_qkv(_dot(hn, w_ref[:, :XB_COL]), cos_ref, sin_ref, q_ref, k_ref, v_ref)


def _inproj_lru(layer, x, g, w, cos, sin, cw, cb, wab, bab, lam, tm, tiles_per_seq):
    t = x.shape[0]
    n_seq = t // (tm * tiles_per_seq)
    assert tm % SUBLANES == 0 and tm >= SUBLANES
    kern = functools.partial(_inproj_lru_kernel, tm=tm, tiles_per_seq=tiles_per_seq)
    tok = lambda w_: pl.BlockSpec((tm, w_), lambda i: (i, 0))
    tab = pl.BlockSpec((tm, LANES), lambda i: (i % tiles_per_seq, 0))
    per_seq = lambda r: pl.BlockSpec((1, r, LRU_W), lambda i: (i // tiles_per_seq, 0, 0))
    lyr = lambda a: _layer_spec(a, layer)
    return pl.pallas_call(
        kern,
        grid=(t // tm,),
        in_specs=[tok(D_MODEL), lyr(g), lyr(w), tab, tab,
                  lyr(cw), lyr(cb), lyr(wab), lyr(bab), lyr(lam)],
        out_specs=[tok(ATTN_W), tok(KV_W), tok(KV_W),
                   pl.BlockSpec((N_SLABS, tm, LANES), lambda i: (0, i, 0)), tok(LRU_W),
                   per_seq(SUBLANES), per_seq(1)],
        out_shape=[jax.ShapeDtypeStruct((t, ATTN_W), BF16),
                   jax.ShapeDtypeStruct((t, KV_W), F32),
                   jax.ShapeDtypeStruct((t, KV_W), F32),
                   jax.ShapeDtypeStruct((N_SLABS, t, LANES), F32),
                   jax.ShapeDtypeStruct((t, LRU_W), F32),
                   jax.ShapeDtypeStruct((n_seq, SUBLANES, LRU_W), F32),
                   jax.ShapeDtypeStruct((n_seq, 1, LRU_W), F32)],
        scratch_shapes=[pltpu.VMEM((N_SLABS, tm + SUBLANES, LANES), F32),
                        pltpu.VMEM((tm, LRU_W), F32),
                        pltpu.VMEM((N_SLABS, tm, LANES), F32),
                        pltpu.VMEM((N_SLABS, tm, LANES), F32),
                        pltpu.VMEM((1, LRU_W), F32)],
        compiler_params=_params(),
        name="inproj_lru",
    )(x, g, w, cos, sin, cw, cb, wab, bab, lam)


def _div_pow2(x, n):
    assert n & (n - 1) == 0
    return x >> (n.bit_length() - 1)


def _mod_pow2(x, n):
    assert n & (n - 1) == 0
    return x & (n - 1)


def _attn_sample_kernel(sink_ref, q_ref, kn_ref, vn_ref, ck_ref, cv_ref,
                        o_ref, nk_ref, nv_ref, *, layer, group, n_new):
    rpb = PAIRS_PER_KV * n_new
    new_pad = kn_ref.shape[1]
    rows = group * rpb
    ncols = group * new_pad
    row = lax.broadcasted_iota(jnp.int32, (rows, 1), 0)
    qpos = _mod_pow2(row, n_new)
    row_seq = _div_pow2(row, rpb)
    second_pair = _mod_pow2(_div_pow2(row, n_new), PAIRS_PER_KV) == 1
    col = lax.broadcasted_iota(jnp.int32, (1, WINDOW), 1)
    mask_c = col > qpos
    if PAST_LEN < WINDOW:
        mask_c = mask_c & (col >= WINDOW - PAST_LEN)
    ncol = lax.broadcasted_iota(jnp.int32, (1, ncols), 1)
    mask_n = (_div_pow2(ncol, new_pad) == row_seq) & (_mod_pow2(ncol, new_pad) <= qpos)
    low = lax.broadcasted_iota(jnp.int32, (rpb, LANES), 1) < HEAD_DIM
    zero_half = jnp.zeros((HEAD_DIM, LANES), BF16)

    kn_all = kn_ref[...].reshape(ncols, KV_W)
    vn_all = vn_ref[...].reshape(ncols, KV_W)
    knt_all = kn_all.T
    vnt_all = vn_all.T
    vn_lane = lax.broadcasted_iota(jnp.int32, vn_all.shape, 1) < HEAD_DIM
    vn_roll = pltpu.roll(vn_all, HEAD_DIM, 1)

    keep = WINDOW - n_new
    cache_lane = lax.broadcasted_iota(jnp.int32, (HEAD_DIM, WINDOW), 1) < keep

    for h in range(N_KV_HEADS):
        hd = slice(h * HEAD_DIM, (h + 1) * HEAD_DIM)
        q_all = q_ref[:, h].reshape(rows, LANES).astype(BF16)
        q_seq = [q_ref[b, h].astype(BF16) for b in range(group)]
        knt_h = knt_all[hd, :].astype(BF16)
        vvn = (jnp.where(vn_lane, vn_all, vn_roll) if h == 0
               else jnp.where(vn_lane, vn_roll, vn_all)).astype(BF16)
        kt, vt2 = [], []
        for b in range(group):
            kt.append(ck_ref[b, h].astype(BF16))
            vt = cv_ref[b, h].astype(BF16)
            vt2.append(jnp.concatenate([vt, vt], axis=0))
        probs, dens = [], []
        for first in (True, False):
            embed = lambda t: jnp.concatenate([t, zero_half] if first else [zero_half, t], axis=0)
            s_new = jnp.where(mask_n, _dot(q_all, embed(knt_h)), -jnp.inf)
            s_c = jnp.concatenate(
                [_dot(q_seq[b], embed(kt[b])) for b in range(group)], axis=0)
            s_c = jnp.where(mask_c, s_c, -jnp.inf)
            pair0 = 2 * h * PAIRS_PER_KV + (0 if first else 1)
            sink = jnp.where(second_pair, sink_ref[layer,pair0 + 2], sink_ref[layer,pair0])
            mx = jnp.maximum(jnp.maximum(jnp.max(s_c, axis=1, keepdims=True),
                                         jnp.max(s_new, axis=1, keepdims=True)), sink)
            p_c = jnp.exp(s_c - mx)
            p_n = jnp.exp(s_new - mx)
            dens.append(jnp.sum(p_c, axis=1, keepdims=True) + jnp.sum(p_n, axis=1, keepdims=True)
                        + jnp.exp(sink - mx))
            probs.append((p_c, _dot(p_n.astype(BF16), vvn)))
        for b in range(group):
            rs = slice(b * rpb, (b + 1) * rpb)
            pc = jnp.concatenate([probs[0][0][rs], probs[1][0][rs]], axis=0).astype(BF16)
            oc = _dot_nt(pc, vt2[b])
            o_first = (oc[:rpb] + probs[0][1][rs]) / dens[0][rs]
            o_second = (oc[rpb:] + probs[1][1][rs]) / dens[1][rs]
            o_ref[b, h] = jnp.where(low, o_first, o_second)
            shift = (keep - b * new_pad) % LANES
            nk_ref[b, h] = jnp.where(cache_lane, pltpu.roll(ck_ref[b, h], keep, 1),
                                     pltpu.roll(knt_all[hd, :], shift, 1))
            nv_ref[b, h] = jnp.where(cache_lane, pltpu.roll(cv_ref[b, h], keep, 1),
                                     pltpu.roll(vnt_all[hd, :], shift, 1))


def _attn_sample(layer, sinks, q, kn, vn, ck, cv, group, n_new):
    bs = q.shape[0]
    kern = functools.partial(_attn_sample_kernel, layer=layer, group=group, n_new=n_new)
    seq4 = lambda a: pl.BlockSpec((group,) + a.shape[1:], lambda i: (i, 0, 0, 0))
    seq3 = lambda a: pl.BlockSpec((group,) + a.shape[1:], lambda i: (i, 0, 0))
    cache = pl.BlockSpec((None, group) + ck.shape[2:], lambda i: (layer, i, 0, 0, 0))
    cache_out = pl.BlockSpec((group,) + ck.shape[2:], lambda i: (i, 0, 0, 0))
    return pl.pallas_call(
        kern,
        grid=(bs // group,),
        in_specs=[pl.BlockSpec(memory_space=pltpu.SMEM),
                  seq4(q), seq3(kn), seq3(vn), cache, cache],
        out_specs=[seq4(q), cache_out, cache_out],
        out_shape=[jax.ShapeDtypeStruct(q.shape, F32),
                   jax.ShapeDtypeStruct(ck.shape[1:], F32),
                   jax.ShapeDtypeStruct(cv.shape[1:], F32)],
        compiler_params=_params(),
        name="attn_sample",
    )(sinks, q, kn, vn, ck, cv)


def _lru_sample_kernel(xb_ref, conv0_ref, h0_ref, cw_ref, cb_ref, wab_ref,
                       bab_ref, lam_ref, h_o_ref, ht_ref, *, n_new):
    logsig = jax.nn.log_sigmoid(lam_ref[...])
    wab = wab_ref[...]
    bab = bab_ref[...]
    xs = [conv0_ref[j] for j in range(CONV_W - 1)] + [xb_ref[t] for t in range(n_new)]
    h = h0_ref[...]
    for t in range(n_new):
        u = cb_ref[...]
        for j in range(CONV_W):
            u = u + xs[t + j] * cw_ref[j:j + 1, :]
        a, b = _lru_gates(u, wab, bab, logsig)
        h = a * h + b
        h_o_ref[t] = h
    ht_ref[...] = h


def _lru_sample(layer, xb_t, conv0_t, h0, cw, cb, wab, bab, lam):
    n_new = xb_t.shape[0]
    kern = functools.partial(_lru_sample_kernel, n_new=n_new)
    lyr = lambda a: _layer_spec(a, layer)
    return pl.pallas_call(
        kern,
        grid=(1,),
        in_specs=[_const_spec(xb_t.shape),
                  lyr(conv0_t), lyr(h0), lyr(cw), lyr(cb), lyr(wab), lyr(bab), lyr(lam)],
        out_specs=[_const_spec(xb_t.shape), _const_spec(h0.shape[1:])],
        out_shape=[jax.ShapeDtypeStruct(xb_t.shape, F32),
                   jax.ShapeDtypeStruct(h0.shape[1:], F32)],
        compiler_params=_params(),
        name="lru_sample",
    )(xb_t, conv0_t, h0, cw, cb, wab, bab, lam)


def _rope_tables(pos):
    half = HEAD_DIM // 2
    inv = ROPE_THETA ** (-jnp.arange(half, dtype=F32) / half)
    ang = pos.astype(F32)[:, None] * inv[None, :]
    cos, sin = jnp.cos(ang), jnp.sin(ang)
    reps = LANES // HEAD_DIM
    return (jnp.concatenate([cos, cos] * reps, axis=1),
            jnp.concatenate([-sin, sin] * reps, axis=1))


def _block_diag(w):
    eye = jnp.eye(LRU_BLOCKS, dtype=w.dtype)
    full = w[:, :, :, None, :] * eye[None, :, None, :, None]
    return full.reshape(w.shape[0], LRU_W, LRU_W)


def _pick_tile(n, candidates):
    for c in candidates:
        if n % c == 0:
            return c
    raise ValueError(f"no tile for {n} tokens")


def kernel(x_prompt, x_sample, cache_k, cache_v, state_h, state_conv, meta_tokens, pre_mix_norm,
           w_in, sinks, conv_w, conv_b, w_a, b_a, w_i, b_i, lam, attn_out_norm, lru_out_norm,
           w_out, post_mix_norm, pre_ffn_norm, w_gate, w_up, w_down, post_ffn_norm):
    bp, seq_in, _ = x_prompt.shape
    seq = seq_in + N_META
    bs, n_new, _ = x_sample.shape
    assert seq % SUBLANES == 0 and (seq % WINDOW) % 16 == 0 and n_new >= CONV_W - 1

    vec = lambda a: a.reshape(DEPTH, 1, -1)
    w_in_b = w_in.astype(BF16)
    w_out_b = w_out.astype(BF16)
    w_gate_b = w_gate.astype(BF16)
    w_up_b = w_up.astype(BF16)
    w_down_b = w_down.astype(BF16)
    wab = jnp.concatenate([_block_diag(w_a), _block_diag(w_i)], axis=2).astype(BF16)
    bab = vec(jnp.concatenate([b_a, b_i], axis=1))
    g_pre, g_att, g_lru = vec(pre_mix_norm), vec(attn_out_norm), vec(lru_out_norm)
    g_pm, g_pf, g_po = vec(post_mix_norm), vec(pre_ffn_norm), vec(post_ffn_norm)
    cb, lam_v = vec(conv_b), vec(lam)

    meta = jnp.broadcast_to(meta_tokens.astype(x_prompt.dtype)[None], (bp, N_META, D_MODEL))
    xp = jnp.concatenate([meta, x_prompt], axis=1).reshape(bp * seq, D_MODEL)
    xs = x_sample.reshape(bs * n_new, D_MODEL)

    cos_p, sin_p = _rope_tables(jnp.arange(seq, dtype=jnp.int32))
    pos_s = PAST_LEN + jnp.arange(n_new, dtype=jnp.int32)
    cos_s, sin_s = _rope_tables(jnp.tile(pos_s, bs))

    tm_p = _pick_tile(seq, (688, 512, 256, 128, 16))
    tm_s = bs * n_new
    tm_f = _pick_tile(bp * seq, (688, 384, 256, 128, 16))
    conv0_p = jnp.zeros((bp, SUBLANES, LRU_W), F32)
    h0_p = jnp.zeros((bp, 1, LRU_W), F32)
    new_pad = SUBLANES
    group = LANES // new_pad
    assert n_new <= new_pad and bs % group == 0 and WINDOW == LANES

    ck_t = cache_k.transpose(0, 1, 3, 4, 2)
    cv_t = cache_v.transpose(0, 1, 3, 4, 2)
    conv_s = state_conv.transpose(0, 2, 1, 3)

    pk, pv, ph, pc, sk, sv, sh, sc = [], [], [], [], [], [], [], []
    for l in range(DEPTH):
        q, k, v, h_seq, gate, xb_tail, ht = _inproj_lru(
            l, xp, g_pre, w_in_b, cos_p, sin_p, conv_w, cb, wab, bab, lam_v, tm_p, seq // tm_p)
        k3 = k.reshape(bp, seq, KV_W)
        v3 = v.reshape(bp, seq, KV_W)
        att = _attn_prompt(l, sinks, q.reshape(bp, seq, ATTN_W), k3, v3, tm_p)
        xp = _ffn(l, xp, att.reshape(bp * seq, ATTN_W), h_seq, gate,
                  g_att, g_lru, w_out_b, g_pm, g_pf, w_gate_b, w_up_b, w_down_b, g_po, tm_f)
        pk.append(k3[:, -WINDOW:].reshape(bp, WINDOW, N_KV_HEADS, HEAD_DIM))
        pv.append(v3[:, -WINDOW:].reshape(bp, WINDOW, N_KV_HEADS, HEAD_DIM))
        ph.append(ht.reshape(bp, LRU_W))
        pc.append(xb_tail[:, -(CONV_W - 1):])

        q, k, v, xb, gate = _inproj(l, xs, g_pre, w_in_b, cos_s, sin_s, tm_s, 1, F32)
        q4 = q.reshape(bs, n_new, N_KV_HEADS, PAIRS_PER_KV, LANES).transpose(0, 2, 3, 1, 4).reshape(
            bs, N_KV_HEADS, PAIRS_PER_KV * n_new, LANES)
        padn = ((0, 0), (0, new_pad - n_new), (0, 0))
        o4, nk, nv = _attn_sample(l, sinks, q4,
                                  jnp.pad(k.reshape(bs, n_new, KV_W), padn),
                                  jnp.pad(v.reshape(bs, n_new, KV_W), padn),
                                  ck_t, cv_t, group, n_new)
        att_raw = o4.reshape(bs, N_KV_HEADS, PAIRS_PER_KV, n_new, LANES).transpose(
            0, 3, 1, 2, 4).reshape(bs * n_new, ATTN_W)
        xb3 = xb.reshape(bs, n_new, LRU_W)
        tmaj = lambda a: a.transpose(1, 0, 2)
        h_t, ht = _lru_sample(l, tmaj(xb3), conv_s, state_h, conv_w, cb, wab, bab, lam_v)
        h_slabs = tmaj(h_t).reshape(bs * n_new, N_SLABS, LANES).transpose(1, 0, 2)
        xs = _ffn(l, xs, att_raw, h_slabs, gate,
                  g_att, g_lru, w_out_b, g_pm, g_pf, w_gate_b, w_up_b, w_down_b, g_po, tm_s)
        sk.append(nk.transpose(0, 3, 1, 2))
        sv.append(nv.transpose(0, 3, 1, 2))
        sh.append(ht)
        sc.append(jnp.concatenate([state_conv[l], xb3], axis=1)[:, -(CONV_W - 1):])

    y_prompt = xp.reshape(bp, seq, D_MODEL)[:, N_META:]
    y_sample = xs.reshape(bs, n_new, D_MODEL)
    return (y_prompt, y_sample, jnp.stack(pk), jnp.stack(pv), jnp.stack(ph), jnp.stack(pc),
            jnp.stack(sk), jnp.stack(sv), jnp.stack(sh), jnp.stack(sc))
```

```python
import functools

import jax
import jax.numpy as jnp
from jax import lax
from jax.experimental import pallas as pl
from jax.experimental.pallas import tpu as pltpu

D_MODEL = 1024
DEPTH = 4
PAST_LEN = 8192
N_META = 16
HEAD_DIM = 64
N_Q_HEADS = 8
N_KV_HEADS = 2
ATTN_W = N_Q_HEADS * HEAD_DIM
KV_W = N_KV_HEADS * HEAD_DIM
LRU_W = D_MODEL - ATTN_W
LRU_BLOCKS = 8
LRU_BLOCK_W = LRU_W // LRU_BLOCKS
CONV_W = 4
LRU_C = 8.0
WINDOW = 128
ROPE_THETA = 10000.0
D_FF = 2816
IN_W = ATTN_W + 2 * KV_W + 2 * LRU_W
EPS = 1e-6

LANES = 128
SUBLANES = 8
N_SLABS = LRU_W // LANES
N_PAIRS = ATTN_W // LANES
PAIRS_PER_KV = N_PAIRS // N_KV_HEADS
VMEM_LIMIT = 56 * 1024 * 1024

BF16 = jnp.bfloat16
F32 = jnp.float32


def _rms(x, g):
    ms = jnp.mean(x * x, axis=-1, keepdims=True)
    return x * lax.rsqrt(ms + EPS) * g


def _dot(a, b):
    return jnp.dot(a, b, preferred_element_type=F32)


def _dot_nt(a, b):
    return lax.dot_general(a, b, (((1,), (1,)), ((), ())), preferred_element_type=F32)


def _const_spec(shape):
    zeros = (0,) * len(shape)
    return pl.BlockSpec(shape, lambda *_: zeros)


def _layer_spec(arr, layer, single_buffer=False):
    tail = (0,) * (arr.ndim - 1)
    mode = dict(pipeline_mode=pl.Buffered(1)) if single_buffer else {}
    return pl.BlockSpec((None,) + arr.shape[1:], lambda *_: (layer,) + tail, **mode)


def _params(n_axes=1):
    return pltpu.CompilerParams(dimension_semantics=("arbitrary",) * n_axes,
                                vmem_limit_bytes=VMEM_LIMIT)


XB_COL = ATTN_W + 2 * KV_W
GATE_COL = XB_COL + LRU_W


def _store_qkv(z, cos_ref, sin_ref, q_ref, k_ref, v_ref):
    cos = cos_ref[...]
    sin = sin_ref[...]
    lane = lax.broadcasted_iota(jnp.int32, cos.shape, 1)
    first_half = (lane & (HEAD_DIM // 2)) == 0

    def rope(t):
        swapped = jnp.where(first_half,
                            pltpu.roll(t, LANES - HEAD_DIM // 2, 1),
                            pltpu.roll(t, HEAD_DIM // 2, 1))
        return t * cos + swapped * sin

    for p in range(N_PAIRS):
        qp = rope(z[:, p * LANES:(p + 1) * LANES]) * (HEAD_DIM ** -0.5)
        q_ref[:, p * LANES:(p + 1) * LANES] = qp.astype(q_ref.dtype)
    k_ref[...] = rope(z[:, ATTN_W:ATTN_W + KV_W])
    v_ref[...] = z[:, ATTN_W + KV_W:XB_COL]


def _inproj_sample_kernel(x_ref, g_ref, w_ref, cos_ref, sin_ref, conv0_ref, h0_ref, cw_ref, cb_ref,
                          wab_ref, bab_ref, lam_ref,
                          q_ref, k_ref, v_ref, xb_ref, gate_ref, h_out_ref, ht_ref, *, n_new):
    n_seq = h0_ref.shape[0]
    hn = _rms(x_ref[...], g_ref[...]).astype(BF16)
    z = _dot(hn, w_ref[...])
    _store_qkv(z[:, :XB_COL], cos_ref, sin_ref, q_ref, k_ref, v_ref)
    xb = z[:, XB_COL:GATE_COL]
    xb_ref[...] = xb
    gate_ref[...] = z[:, GATE_COL:]

    step = lambda a, t: a[t * n_seq:(t + 1) * n_seq, :]
    xs = [conv0_ref[j] for j in range(CONV_W - 1)] + [step(xb, t) for t in range(n_new)]
    us = []
    for t in range(n_new):
        u = cb_ref[...]
        for j in range(CONV_W):
            u = u + xs[t + j] * cw_ref[j:j + 1, :]
        us.append(u)
    a, b = _lru_gates(jnp.concatenate(us, axis=0), wab_ref[...], bab_ref[...],
                      jax.nn.log_sigmoid(lam_ref[...]))
    h = h0_ref[...]
    for t in range(n_new):
        h = step(a, t) * h + step(b, t)
        for s in range(N_SLABS):
            h_out_ref[s, t * n_seq:(t + 1) * n_seq, :] = h[:, s * LANES:(s + 1) * LANES]
    ht_ref[...] = h


def _inproj_sample(layer, x, g, w, cos, sin, conv0, h0, cw, cb, wab, bab, lam, n_new):
    t = x.shape[0]
    n_seq = t // n_new
    kern = functools.partial(_inproj_sample_kernel, n_new=n_new)
    lyr = lambda a: _layer_spec(a, layer)
    whole = lambda *shape: _const_spec(shape)
    return pl.pallas_call(
        kern,
        grid=(1,),
        in_specs=[whole(t, D_MODEL), lyr(g), lyr(w), whole(t, LANES), whole(t, LANES),
                  lyr(conv0), lyr(h0), lyr(cw), lyr(cb), lyr(wab), lyr(bab), lyr(lam)],
        out_specs=[whole(t, ATTN_W), whole(t, KV_W), whole(t, KV_W), whole(t, LRU_W),
                   whole(t, LRU_W), whole(N_SLABS, t, LANES), whole(n_seq, LRU_W)],
        out_shape=[jax.ShapeDtypeStruct((t, ATTN_W), F32),
                   jax.ShapeDtypeStruct((t, KV_W), F32),
                   jax.ShapeDtypeStruct((t, KV_W), F32),
                   jax.ShapeDtypeStruct((t, LRU_W), F32),
                   jax.ShapeDtypeStruct((t, LRU_W), F32),
                   jax.ShapeDtypeStruct((N_SLABS, t, LANES), F32),
                   jax.ShapeDtypeStruct((n_seq, LRU_W), F32)],
        compiler_params=_params(),
        name="inproj_sample",
    )(x, g, w, cos, sin, conv0, h0, cw, cb, wab, bab, lam)


MXU_COLS = 256
FF_SPLIT = (D_FF // MXU_COLS + 1) // 2 * MXU_COLS


def _gelu_tanh(x):
    return 0.5 * x * (1.0 + jnp.tanh(0.7978845608028654 * (x + 0.044715 * (x * x * x))))


def _mixer_out(x_ref, att_ref, h_ref, gate_ref, ga_ref, gl_ref, wo_ref, gpm_ref, gpf_ref):
    att = _rms(att_ref[...], ga_ref[...]).astype(BF16)
    h = jnp.concatenate([h_ref[s] for s in range(N_SLABS)], axis=1)
    lru = _rms(h * _gelu_tanh(gate_ref[...]), gl_ref[...]).astype(BF16)
    m = _dot(att, wo_ref[0:ATTN_W, :]) + _dot(lru, wo_ref[ATTN_W:, :])
    x1 = x_ref[...] + _rms(m, gpm_ref[...])
    return x1, _rms(x1, gpf_ref[...]).astype(BF16)


def _swiglu_out(x1, hf, wg_ref, wu_ref, wd_ref, gpo_ref):
    f = None
    for lo, hi in ((0, FF_SPLIT), (FF_SPLIT, D_FF)):
        gt = _dot(hf, wg_ref[:, lo:hi])
        up = _dot(hf, wu_ref[:, lo:hi])
        hid = (gt * jax.nn.sigmoid(gt) * up).astype(BF16)
        part = _dot(hid, wd_ref[lo:hi, :])
        f = part if f is None else f + part
    return x1 + _rms(f, gpo_ref[...])


def _ffn_kernel(x_ref, att_ref, h_ref, gate_ref, ga_ref, gl_ref, wo_ref, gpm_ref, gpf_ref,
                wg_ref, wu_ref, wd_ref, gpo_ref, o_ref):
    x1, hf = _mixer_out(x_ref, att_ref, h_ref, gate_ref, ga_ref, gl_ref, wo_ref, gpm_ref, gpf_ref)
    o_ref[...] = _swiglu_out(x1, hf, wg_ref, wu_ref, wd_ref, gpo_ref)


def _ffn(layer, x, att, h, gate, ga, gl, wo, gpm, gpf, wg, wu, wd, gpo, tm):
    t = x.shape[0]
    tok = lambda w_: pl.BlockSpec((tm, w_), lambda i: (i, 0))
    big = lambda a: _layer_spec(a, layer, single_buffer=True)
    lyr = lambda a: _layer_spec(a, layer)
    return pl.pallas_call(
        _ffn_kernel,
        grid=(t // tm,),
        in_specs=[tok(D_MODEL), tok(ATTN_W),
                  pl.BlockSpec((N_SLABS, tm, LANES), lambda i: (0, i, 0)), tok(LRU_W),
                  lyr(ga), lyr(gl),
                  big(wo), lyr(gpm), lyr(gpf), big(wg), big(wu), big(wd), lyr(gpo)],
        out_specs=tok(D_MODEL),
        out_shape=jax.ShapeDtypeStruct((t, D_MODEL), F32),
        compiler_params=_params(),
        name="outffn",
    )(x, att, h, gate, ga, gl, wo, gpm, gpf, wg, wu, wd, gpo)


def _kv_variants(k, v):
    lane = lax.broadcasted_iota(jnp.int32, k.shape, 1)
    low = lane < HEAD_DIM
    kr = pltpu.roll(k, HEAD_DIM, 1)
    vr = pltpu.roll(v, HEAD_DIM, 1)
    zero = jnp.zeros_like(k)
    ka = (jnp.where(low, k, zero), jnp.where(low, kr, zero))
    kb = (jnp.where(low, zero, kr), jnp.where(low, zero, k))
    vv = (jnp.where(low, v, vr), jnp.where(low, vr, v))
    cast = lambda pair: tuple(a.astype(BF16) for a in pair)
    return cast(ka), cast(kb), cast(vv)


def _attn_prompt_kernel(sink_ref, q_ref, k_ref, v_ref, o_ref,
                        ka_ref, kb_ref, vv_ref, *, layer, seq, chunk):
    for c in range(seq // chunk):
        rows = slice(c * chunk, (c + 1) * chunk)
        ka, kb, vv = _kv_variants(k_ref[0, rows, :], v_ref[0, rows, :])
        ones = jnp.ones((chunk, LANES), BF16)
        for h in range(N_KV_HEADS):
            ka_ref[h, rows, :] = ka[h]
            kb_ref[h, rows, :] = kb[h]
            vv_ref[h, rows, :] = jnp.concatenate([vv[h], ones], axis=1)

    def tile(q0, ks, nq, nk, sink_col):
        row = lax.broadcasted_iota(jnp.int32, (nq, nk), 0)
        col = lax.broadcasted_iota(jnp.int32, (nq, nk), 1)
        rel = row - col + (q0 - ks)
        mask = (rel >= 0) & (rel < WINDOW)
        col_row = lax.broadcasted_iota(jnp.int32, (1, nk), 1)
        key_row = lax.broadcasted_iota(jnp.int32, (nk, 2 * LANES), 0)
        val_lane = lax.broadcasted_iota(jnp.int32, (nk, 2 * LANES), 1)
        zero_v = (key_row == sink_col) & (val_lane < LANES)
        vvw = [jnp.where(zero_v, jnp.zeros((), BF16), vv_ref[h, pl.ds(ks, nk), :])
               for h in range(N_KV_HEADS)]
        low = lax.broadcasted_iota(jnp.int32, (nq, LANES), 1) < HEAD_DIM
        for p in range(N_PAIRS):
            h = p // PAIRS_PER_KV
            qp = q_ref[0, pl.ds(q0, nq), p * LANES:(p + 1) * LANES]
            halves = []
            for k_ref_, head in ((ka_ref, 2 * p), (kb_ref, 2 * p + 1)):
                s = _dot_nt(qp, k_ref_[h, pl.ds(ks, nk), :])
                fill = jnp.where(col_row == sink_col, sink_ref[layer, head], -jnp.inf)
                s = jnp.where(mask, s, fill)
                e = jnp.exp(s - jnp.max(s, axis=1, keepdims=True))
                pv = _dot(e.astype(BF16), vvw[h])
                halves.append(pv[:, :LANES] / pv[:, LANES:])
            o_ref[0, pl.ds(q0, nq), p * LANES:(p + 1) * LANES] = jnp.where(low, halves[0], halves[1])

    n_full = seq // WINDOW
    group = 4

    def body(it, carry):
        for t in range(group):
            i = it * group + t
            q0 = pl.multiple_of(i * WINDOW, WINDOW)
            ks = pl.multiple_of(jnp.maximum(q0 - WINDOW, 0), WINDOW)
            sink_col = jnp.where(i == 0, 2 * WINDOW - 1, 0) if t == 0 else 0
            tile(q0, ks, WINDOW, 2 * WINDOW, sink_col)
        return carry

    lax.fori_loop(0, n_full // group, body, 0)
    for i in range(n_full // group * group, n_full):
        tile(i * WINDOW, max(i - 1, 0) * WINDOW, WINDOW, 2 * WINDOW, 2 * WINDOW - 1 if i == 0 else 0)
    if seq > n_full * WINDOW:
        tile(seq - WINDOW, seq - 2 * WINDOW, WINDOW, 2 * WINDOW, 0)


def _attn_prompt(layer, sinks, q, k, v, chunk):
    b, seq, _ = q.shape
    assert seq >= 2 * WINDOW
    kern = functools.partial(_attn_prompt_kernel, layer=layer, seq=seq, chunk=chunk)
    seq_spec = lambda w_: pl.BlockSpec((1, seq, w_), lambda i: (i, 0, 0))
    return pl.pallas_call(
        kern,
        grid=(b,),
        in_specs=[pl.BlockSpec(memory_space=pltpu.SMEM),
                  seq_spec(ATTN_W), seq_spec(KV_W), seq_spec(KV_W)],
        out_specs=seq_spec(ATTN_W),
        out_shape=jax.ShapeDtypeStruct((b, seq, ATTN_W), F32),
        scratch_shapes=[pltpu.VMEM((N_KV_HEADS, seq, LANES), BF16),
                        pltpu.VMEM((N_KV_HEADS, seq, LANES), BF16),
                        pltpu.VMEM((N_KV_HEADS, seq, 2 * LANES), BF16)],
        compiler_params=_params(),
        name="attn_prompt",
    )(sinks, q, k, v)


def _lru_gates(u, wab, bab, logsig):
    ga = _dot(u.astype(BF16), wab) + bab
    r = jax.nn.sigmoid(ga[:, :LRU_W])
    ig = jax.nn.sigmoid(ga[:, LRU_W:])
    log_a = LRU_C * r * logsig
    a = jnp.exp(log_a)
    b = jnp.sqrt(-jnp.tanh(log_a) * (a * a + 1.0)) * ig * u
    return a, b


def _scan_interleaved(a_ref, b_ref, h0, rows):
    piece = rows // SUBLANES
    at = lambda i: slice(i * SUBLANES, (i + 1) * SUBLANES)
    hs = [jnp.zeros((SUBLANES, LANES), F32) for _ in range(N_SLABS)]
    ps = [jnp.ones((SUBLANES, LANES), F32) for _ in range(N_SLABS)]
    for i in range(piece):
        for s in range(N_SLABS):
            a = a_ref[s, at(i), :]
            hs[s] = a * hs[s] + b_ref[s, at(i), :]
            ps[s] = a * ps[s]
            b_ref[s, at(i), :] = hs[s]
            a_ref[s, at(i), :] = ps[s]

    sub = lax.broadcasted_iota(jnp.int32, (SUBLANES, LANES), 0)
    carry_in, h_last = [], []
    for s in range(N_SLABS):
        cin = jnp.broadcast_to(h0[:, s * LANES:(s + 1) * LANES], (SUBLANES, LANES))
        acc = jnp.zeros((SUBLANES, LANES), F32)
        for c in range(SUBLANES):
            acc = jnp.where(sub == c, cin, acc)
            end = hs[s] + ps[s] * cin
            cin = jnp.broadcast_to(end[c:c + 1, :], (SUBLANES, LANES))
        carry_in.append(acc)
        h_last.append(cin[0:1, :])

    for i in range(piece):
        for s in range(N_SLABS):
            b_ref[s, at(i), :] = b_ref[s, at(i), :] + a_ref[s, at(i), :] * carry_in[s]
    return jnp.concatenate(h_last, axis=1)


def _inproj_lru_kernel(x_ref, g_ref, w_ref, cos_ref, sin_ref, cw_ref, cb_ref, wab_ref, bab_ref,
                       lam_ref, q_ref, k_ref, v_ref, h_out_ref, gate_ref, tail_ref, ht_ref,
                       xs_ref, u_ref, a_ref, b_ref, h_ref, *, tm, tiles_per_seq):
    pad = SUBLANES
    piece = tm // SUBLANES
    slab = lambda s: slice(s * LANES, (s + 1) * LANES)

    @pl.when(pl.program_id(0) % tiles_per_seq == 0)
    def _():
        xs_ref[:, 0:pad, :] = jnp.zeros((N_SLABS, pad, LANES), F32)
        h_ref[...] = jnp.zeros((1, LRU_W), F32)

    hn = _rms(x_ref[...], g_ref[...]).astype(BF16)
    xb = _dot(hn, w_ref[:, XB_COL:GATE_COL])
    gate_ref[...] = _dot(hn, w_ref[:, GATE_COL:])
    tail_ref[0] = xb[tm - pad:, :]
    for s in range(N_SLABS):
        xs_ref[s, pad:pad + tm, :] = xb[:, slab(s)]

    for s in range(N_SLABS):
        bias = jnp.broadcast_to(cb_ref[:, slab(s)], (SUBLANES, LANES))
        wts = [jnp.broadcast_to(cw_ref[j:j + 1, slab(s)], (SUBLANES, LANES))
               for j in range(CONV_W)]
        taps = [xs_ref[s, pl.ds(pad - (CONV_W - 1) + k, SUBLANES, stride=piece), :]
                for k in range(CONV_W - 1)]
        for i in range(piece):
            taps.append(xs_ref[s, pl.ds(pad + i, SUBLANES, stride=piece), :])
            u = bias
            for j in range(CONV_W):
                u = u + taps[j] * wts[j]
            u_ref[i * SUBLANES:(i + 1) * SUBLANES, slab(s)] = u
            taps.pop(0)
    for s in range(N_SLABS):
        xs_ref[s, 0:pad, :] = xs_ref[s, tm:tm + pad, :]

    a, b = _lru_gates(u_ref[...], wab_ref[...], bab_ref[...], jax.nn.log_sigmoid(lam_ref[...]))
    for s in range(N_SLABS):
        a_ref[s] = a[:, slab(s)]
        b_ref[s] = b[:, slab(s)]
    h_last = _scan_interleaved(a_ref, b_ref, h_ref[...], tm)
    h_ref[...] = h_last
    ht_ref[0] = h_last
    for s in range(N_SLABS):
        for i in range(piece):
            h_out_ref[s, pl.ds(i, SUBLANES, stride=piece), :] = (
                b_ref[s, i * SUBLANES:(i + 1) * SUBLANES, :])

    _store_qkv(_dot(hn, w_ref[:, :XB_COL]), cos_ref, sin_ref, q_ref, k_ref, v_ref)


def _inproj_lru(layer, x, g, w, cos, sin, cw, cb, wab, bab, lam, tm, tiles_per_seq):
    t = x.shape[0]
    n_seq = t // (tm * tiles_per_seq)
    assert tm % SUBLANES == 0 and tm >= SUBLANES
    kern = functools.partial(_inproj_lru_kernel, tm=tm, tiles_per_seq=tiles_per_seq)
    tok = lambda w_: pl.BlockSpec((tm, w_), lambda i: (i, 0))
    tab = pl.BlockSpec((tm, LANES), lambda i: (i % tiles_per_seq, 0))
    per_seq = lambda r: pl.BlockSpec((1, r, LRU_W), lambda i: (i // tiles_per_seq, 0, 0))
    lyr = lambda a: _layer_spec(a, layer)
    return pl.pallas_call(
        kern,
        grid=(t // tm,),
        in_specs=[tok(D_MODEL), lyr(g), lyr(w), tab, tab,
                  lyr(cw), lyr(cb), lyr(wab), lyr(bab), lyr(lam)],
        out_specs=[tok(ATTN_W), tok(KV_W), tok(KV_W),
                   pl.BlockSpec((N_SLABS, tm, LANES), lambda i: (0, i, 0)), tok(LRU_W),
                   per_seq(SUBLANES), per_seq(1)],
        out_shape=[jax.ShapeDtypeStruct((t, ATTN_W), BF16),
                   jax.ShapeDtypeStruct((t, KV_W), F32),
                   jax.ShapeDtypeStruct((t, KV_W), F32),
                   jax.ShapeDtypeStruct((N_SLABS, t, LANES), F32),
                   jax.ShapeDtypeStruct((t, LRU_W), F32),
                   jax.ShapeDtypeStruct((n_seq, SUBLANES, LRU_W), F32),
                   jax.ShapeDtypeStruct((n_seq, 1, LRU_W), F32)],
        scratch_shapes=[pltpu.VMEM((N_SLABS, tm + SUBLANES, LANES), F32),
                        pltpu.VMEM((tm, LRU_W), F32),
                        pltpu.VMEM((N_SLABS, tm, LANES), F32),
                        pltpu.VMEM((N_SLABS, tm, LANES), F32),
                        pltpu.VMEM((1, LRU_W), F32)],
        compiler_params=_params(),
        name="inproj_lru",
    )(x, g, w, cos, sin, cw, cb, wab, bab, lam)


def _div_pow2(x, n):
    assert n & (n - 1) == 0
    return x >> (n.bit_length() - 1)


def _mod_pow2(x, n):
    assert n & (n - 1) == 0
    return x & (n - 1)


def _attn_sample_kernel(sink_ref, q_ref, kn_ref, vn_ref, ck_ref, cv_ref, nk_all_ref, nv_all_ref,
                        o_ref, nk_ref, nv_ref, *, layer, group, n_new):
    del nk_all_ref, nv_all_ref
    rpb = PAIRS_PER_KV * n_new
    new_pad = kn_ref.shape[1]
    rows = group * rpb
    ncols = group * new_pad
    row = lax.broadcasted_iota(jnp.int32, (rows, 1), 0)
    qpos = _mod_pow2(row, n_new)
    row_seq = _div_pow2(row, rpb)
    second_pair = _mod_pow2(_div_pow2(row, n_new), PAIRS_PER_KV) == 1
    col = lax.broadcasted_iota(jnp.int32, (1, WINDOW), 1)
    mask_c = col > qpos
    if PAST_LEN < WINDOW:
        mask_c = mask_c & (col >= WINDOW - PAST_LEN)
    ncol = lax.broadcasted_iota(jnp.int32, (1, ncols), 1)
    mask_n = (_div_pow2(ncol, new_pad) == row_seq) & (_mod_pow2(ncol, new_pad) <= qpos)
    low = lax.broadcasted_iota(jnp.int32, (rpb, LANES), 1) < HEAD_DIM
    zero_half = jnp.zeros((HEAD_DIM, LANES), BF16)

    kn_all = kn_ref[...].reshape(ncols, KV_W)
    vn_all = vn_ref[...].reshape(ncols, KV_W)
    knt_all = kn_all.T
    vnt_all = vn_all.T
    vn_lane = lax.broadcasted_iota(jnp.int32, vn_all.shape, 1) < HEAD_DIM
    vn_roll = pltpu.roll(vn_all, HEAD_DIM, 1)

    keep = WINDOW - n_new
    cache_lane = lax.broadcasted_iota(jnp.int32, (HEAD_DIM, WINDOW), 1) < keep

    for h in range(N_KV_HEADS):
        hd = slice(h * HEAD_DIM, (h + 1) * HEAD_DIM)
        q_all = q_ref[:, h].reshape(rows, LANES).astype(BF16)
        q_seq = [q_ref[b, h].astype(BF16) for b in range(group)]
        knt_h = knt_all[hd, :].astype(BF16)
        vvn = (jnp.where(vn_lane, vn_all, vn_roll) if h == 0
               else jnp.where(vn_lane, vn_roll, vn_all)).astype(BF16)
        kt, vt2 = [], []
        for b in range(group):
            kt.append(ck_ref[b, h].astype(BF16))
            vt = cv_ref[b, h].astype(BF16)
            vt2.append(jnp.concatenate([vt, vt], axis=0))
        probs, dens = [], []
        for first in (True, False):
            embed = lambda t: jnp.concatenate([t, zero_half] if first else [zero_half, t], axis=0)
            s_new = jnp.where(mask_n, _dot(q_all, embed(knt_h)), -jnp.inf)
            s_c = jnp.concatenate(
                [_dot(q_seq[b], embed(kt[b])) for b in range(group)], axis=0)
            s_c = jnp.where(mask_c, s_c, -jnp.inf)
            pair0 = 2 * h * PAIRS_PER_KV + (0 if first else 1)
            sink = jnp.where(second_pair, sink_ref[layer,pair0 + 2], sink_ref[layer,pair0])
            mx = jnp.maximum(jnp.maximum(jnp.max(s_c, axis=1, keepdims=True),
                                         jnp.max(s_new, axis=1, keepdims=True)), sink)
            p_c = jnp.exp(s_c - mx)
            p_n = jnp.exp(s_new - mx)
            dens.append(jnp.sum(p_c, axis=1, keepdims=True) + jnp.sum(p_n, axis=1, keepdims=True)
                        + jnp.exp(sink - mx))
            probs.append((p_c, _dot(p_n.astype(BF16), vvn)))
        for b in range(group):
            rs = slice(b * rpb, (b + 1) * rpb)
            pc = jnp.concatenate([probs[0][0][rs], probs[1][0][rs]], axis=0).astype(BF16)
            oc = _dot_nt(pc, vt2[b])
            o_first = (oc[:rpb] + probs[0][1][rs]) / dens[0][rs]
            o_second = (oc[rpb:] + probs[1][1][rs]) / dens[1][rs]
            o_ref[b, h] = jnp.where(low, o_first, o_second)
            shift = (keep - b * new_pad) % LANES
            nk_ref[b, h] = jnp.where(cache_lane, pltpu.roll(ck_ref[b, h], keep, 1),
                                     pltpu.roll(knt_all[hd, :], shift, 1))
            nv_ref[b, h] = jnp.where(cache_lane, pltpu.roll(cv_ref[b, h], keep, 1),
                                     pltpu.roll(vnt_all[hd, :], shift, 1))


def _attn_sample(layer, sinks, q, kn, vn, ck, cv, nk_all, nv_all, group, n_new):
    bs = q.shape[0]
    kern = functools.partial(_attn_sample_kernel, layer=layer, group=group, n_new=n_new)
    seq4 = lambda a: pl.BlockSpec((group,) + a.shape[1:], lambda i: (i, 0, 0, 0))
    seq3 = lambda a: pl.BlockSpec((group,) + a.shape[1:], lambda i: (i, 0, 0))
    cache = pl.BlockSpec((None, group) + ck.shape[2:], lambda i: (layer, i, 0, 0, 0))
    in_hbm = pl.BlockSpec(memory_space=pl.ANY)
    return pl.pallas_call(
        kern,
        grid=(bs // group,),
        in_specs=[pl.BlockSpec(memory_space=pltpu.SMEM),
                  seq4(q), seq3(kn), seq3(vn), cache, cache, in_hbm, in_hbm],
        out_specs=[seq4(q), cache, cache],
        out_shape=[jax.ShapeDtypeStruct(q.shape, F32),
                   jax.ShapeDtypeStruct(nk_all.shape, F32),
                   jax.ShapeDtypeStruct(nv_all.shape, F32)],
        input_output_aliases={6: 1, 7: 2},
        compiler_params=_params(),
        name="attn_sample",
    )(sinks, q, kn, vn, ck, cv, nk_all, nv_all)


def _rope_tables(pos):
    half = HEAD_DIM // 2
    inv = ROPE_THETA ** (-jnp.arange(half, dtype=F32) / half)
    ang = pos.astype(F32)[:, None] * inv[None, :]
    cos, sin = jnp.cos(ang), jnp.sin(ang)
    reps = LANES // HEAD_DIM
    return (jnp.concatenate([cos, cos] * reps, axis=1),
            jnp.concatenate([-sin, sin] * reps, axis=1))


def _block_diag(w):
    eye = jnp.eye(LRU_BLOCKS, dtype=w.dtype)
    full = w[:, :, :, None, :] * eye[None, :, None, :, None]
    return full.reshape(w.shape[0], LRU_W, LRU_W)


def _pick_tile(n, candidates):
    for c in candidates:
        if n % c == 0:
            return c
    raise ValueError(f"no tile for {n} tokens")


def kernel(x_prompt, x_sample, cache_k, cache_v, state_h, state_conv, meta_tokens, pre_mix_norm,
           w_in, sinks, conv_w, conv_b, w_a, b_a, w_i, b_i, lam, attn_out_norm, lru_out_norm,
           w_out, post_mix_norm, pre_ffn_norm, w_gate, w_up, w_down, post_ffn_norm):
    bp, seq_in, _ = x_prompt.shape
    seq = seq_in + N_META
    bs, n_new, _ = x_sample.shape
    assert seq % SUBLANES == 0 and (seq % WINDOW) % 16 == 0 and n_new >= CONV_W - 1

    vec = lambda a: a.reshape(DEPTH, 1, -1)
    w_in_b = w_in.astype(BF16)
    w_out_b = w_out.astype(BF16)
    w_gate_b = w_gate.astype(BF16)
    w_up_b = w_up.astype(BF16)
    w_down_b = w_down.astype(BF16)
    wab = jnp.concatenate([_block_diag(w_a), _block_diag(w_i)], axis=2).astype(BF16)
    bab = vec(jnp.concatenate([b_a, b_i], axis=1))
    g_pre, g_att, g_lru = vec(pre_mix_norm), vec(attn_out_norm), vec(lru_out_norm)
    g_pm, g_pf, g_po = vec(post_mix_norm), vec(pre_ffn_norm), vec(post_ffn_norm)
    cb, lam_v = vec(conv_b), vec(lam)

    meta = jnp.broadcast_to(meta_tokens.astype(x_prompt.dtype)[None], (bp, N_META, D_MODEL))
    xp = jnp.concatenate([meta, x_prompt], axis=1).reshape(bp * seq, D_MODEL)
    xs = x_sample.transpose(1, 0, 2).reshape(n_new * bs, D_MODEL)

    cos_p, sin_p = _rope_tables(jnp.arange(seq, dtype=jnp.int32))
    pos_s = PAST_LEN + jnp.arange(n_new, dtype=jnp.int32)
    cos_s, sin_s = _rope_tables(jnp.repeat(pos_s, bs))

    tm_p = _pick_tile(seq, (688, 512, 256, 128, 16))
    tm_s = bs * n_new
    tm_f = _pick_tile(bp * seq, (688, 384, 256, 128, 16))
    new_pad = SUBLANES
    group = LANES // new_pad
    assert n_new <= new_pad and bs % group == 0 and WINDOW == LANES

    ck_t = cache_k.transpose(0, 1, 3, 4, 2)
    cv_t = cache_v.transpose(0, 1, 3, 4, 2)
    conv_s = state_conv.transpose(0, 2, 1, 3)

    nk_all = jnp.zeros(ck_t.shape, F32)
    nv_all = jnp.zeros(cv_t.shape, F32)
    pk, pv, ph, pc, sh, sc = [], [], [], [], [], []
    for l in range(DEPTH):
        q, k, v, h_seq, gate, xb_tail, ht = _inproj_lru(
            l, xp, g_pre, w_in_b, cos_p, sin_p, conv_w, cb, wab, bab, lam_v, tm_p, seq // tm_p)
        k3 = k.reshape(bp, seq, KV_W)
        v3 = v.reshape(bp, seq, KV_W)
        att = _attn_prompt(l, sinks, q.reshape(bp, seq, ATTN_W), k3, v3, tm_p)
        xp = _ffn(l, xp, att.reshape(bp * seq, ATTN_W), h_seq, gate,
                  g_att, g_lru, w_out_b, g_pm, g_pf, w_gate_b, w_up_b, w_down_b, g_po, tm_f)
        pk.append(k3[:, -WINDOW:].reshape(bp, WINDOW, N_KV_HEADS, HEAD_DIM))
        pv.append(v3[:, -WINDOW:].reshape(bp, WINDOW, N_KV_HEADS, HEAD_DIM))
        ph.append(ht.reshape(bp, LRU_W))
        pc.append(xb_tail[:, -(CONV_W - 1):])

        q, k, v, xb, gate, h_slabs, ht = _inproj_sample(
            l, xs, g_pre, w_in_b, cos_s, sin_s, conv_s, state_h, conv_w, cb, wab, bab, lam_v, n_new)
        q4 = q.reshape(n_new, bs, N_KV_HEADS, PAIRS_PER_KV, LANES).transpose(1, 2, 3, 0, 4).reshape(
            bs, N_KV_HEADS, PAIRS_PER_KV * n_new, LANES)
        seq_major = lambda a: a.reshape(n_new, bs, -1).transpose(1, 0, 2)
        padn = ((0, 0), (0, new_pad - n_new), (0, 0))
        o4, nk_all, nv_all = _attn_sample(l, sinks, q4, jnp.pad(seq_major(k), padn),
                                          jnp.pad(seq_major(v), padn), ck_t, cv_t,
                                          nk_all, nv_all, group, n_new)
        att_raw = o4.reshape(bs, N_KV_HEADS, PAIRS_PER_KV, n_new, LANES).transpose(
            3, 0, 1, 2, 4).reshape(n_new * bs, ATTN_W)
        xs = _ffn(l, xs, att_raw, h_slabs, gate,
                  g_att, g_lru, w_out_b, g_pm, g_pf, w_gate_b, w_up_b, w_down_b, g_po, tm_s)
        sh.append(ht)
        sc.append(seq_major(xb)[:, -(CONV_W - 1):])

    y_prompt = xp.reshape(bp, seq, D_MODEL)[:, N_META:]
    y_sample = xs.reshape(n_new, bs, D_MODEL).transpose(1, 0, 2)
    sample_k = nk_all.transpose(0, 1, 4, 2, 3)
    sample_v = nv_all.transpose(0, 1, 4, 2, 3)
    return (y_prompt, y_sample, jnp.stack(pk), jnp.stack(pv), jnp.stack(ph), jnp.stack(pc),
            sample_k, sample_v, jnp.stack(sh), jnp.stack(sc))
```

```python
import functools

import jax
import jax.numpy as jnp
from jax import lax
from jax.experimental import pallas as pl
from jax.experimental.pallas import tpu as pltpu

D_MODEL = 1024
DEPTH = 4
PAST_LEN = 8192
N_META = 16
HEAD_DIM = 64
N_Q_HEADS = 8
N_KV_HEADS = 2
ATTN_W = N_Q_HEADS * HEAD_DIM
KV_W = N_KV_HEADS * HEAD_DIM
LRU_W = D_MODEL - ATTN_W
LRU_BLOCKS = 8
LRU_BLOCK_W = LRU_W // LRU_BLOCKS
CONV_W = 4
LRU_C = 8.0
WINDOW = 128
ROPE_THETA = 10000.0
D_FF = 2816
IN_W = ATTN_W + 2 * KV_W + 2 * LRU_W
EPS = 1e-6

LANES = 128
SUBLANES = 8
N_SLABS = LRU_W // LANES
N_PAIRS = ATTN_W // LANES
PAIRS_PER_KV = N_PAIRS // N_KV_HEADS
VMEM_LIMIT = 56 * 1024 * 1024

BF16 = jnp.bfloat16
F32 = jnp.float32


def _rms(x, g):
    ms = jnp.mean(x * x, axis=-1, keepdims=True)
    return x * lax.rsqrt(ms + EPS) * g


def _dot(a, b):
    return jnp.dot(a, b, preferred_element_type=F32)


def _dot_nt(a, b):
    return lax.dot_general(a, b, (((1,), (1,)), ((), ())), preferred_element_type=F32)


def _const_spec(shape):
    zeros = (0,) * len(shape)
    return pl.BlockSpec(shape, lambda *_: zeros)


def _layer_spec(arr, layer, single_buffer=False):
    tail = (0,) * (arr.ndim - 1)
    mode = dict(pipeline_mode=pl.Buffered(1)) if single_buffer else {}
    return pl.BlockSpec((None,) + arr.shape[1:], lambda *_: (layer,) + tail, **mode)


def _params(n_axes=1):
    return pltpu.CompilerParams(dimension_semantics=("arbitrary",) * n_axes,
                                vmem_limit_bytes=VMEM_LIMIT)


XB_COL = ATTN_W + 2 * KV_W
GATE_COL = XB_COL + LRU_W


def _store_qkv(z, cos_ref, sin_ref, q_ref, k_ref, v_ref):
    cos = cos_ref[...]
    sin = sin_ref[...]
    lane = lax.broadcasted_iota(jnp.int32, cos.shape, 1)
    first_half = (lane & (HEAD_DIM // 2)) == 0

    def rope(t):
        swapped = jnp.where(first_half,
                            pltpu.roll(t, LANES - HEAD_DIM // 2, 1),
                            pltpu.roll(t, HEAD_DIM // 2, 1))
        return t * cos + swapped * sin

    for p in range(N_PAIRS):
        qp = rope(z[:, p * LANES:(p + 1) * LANES]) * (HEAD_DIM ** -0.5)
        q_ref[:, p * LANES:(p + 1) * LANES] = qp.astype(q_ref.dtype)
    k_ref[...] = rope(z[:, ATTN_W:ATTN_W + KV_W])
    v_ref[...] = z[:, ATTN_W + KV_W:XB_COL]


def _inproj_sample_kernel(x_ref, g_ref, w_ref, cos_ref, sin_ref, conv0_ref, h0_ref, cw_ref, cb_ref,
                          wab_ref, bab_ref, lam_ref,
                          q_ref, k_ref, v_ref, xb_ref, gate_ref, h_out_ref, ht_ref, *, n_new):
    n_seq = h0_ref.shape[0]
    hn = _rms(x_ref[...], g_ref[...]).astype(BF16)
    z = _dot(hn, w_ref[...])
    _store_qkv(z[:, :XB_COL], cos_ref, sin_ref, q_ref, k_ref, v_ref)
    xb = z[:, XB_COL:GATE_COL]
    xb_ref[...] = xb
    gate_ref[...] = z[:, GATE_COL:]

    step = lambda a, t: a[t * n_seq:(t + 1) * n_seq, :]
    xs = [conv0_ref[j] for j in range(CONV_W - 1)] + [step(xb, t) for t in range(n_new)]
    us = []
    for t in range(n_new):
        u = cb_ref[...]
        for j in range(CONV_W):
            u = u + xs[t + j] * cw_ref[j:j + 1, :]
        us.append(u)
    a, b = _lru_gates(jnp.concatenate(us, axis=0), wab_ref[...], bab_ref[...],
                      jax.nn.log_sigmoid(lam_ref[...]))
    h = h0_ref[...]
    for t in range(n_new):
        h = step(a, t) * h + step(b, t)
        for s in range(N_SLABS):
            h_out_ref[s, t * n_seq:(t + 1) * n_seq, :] = h[:, s * LANES:(s + 1) * LANES]
    ht_ref[...] = h


def _inproj_sample(layer, x, g, w, cos, sin, conv0, h0, cw, cb, wab, bab, lam, n_new):
    t = x.shape[0]
    n_seq = t // n_new
    kern = functools.partial(_inproj_sample_kernel, n_new=n_new)
    lyr = lambda a: _layer_spec(a, layer)
    whole = lambda *shape: _const_spec(shape)
    return pl.pallas_call(
        kern,
        grid=(1,),
        in_specs=[whole(t, D_MODEL), lyr(g), lyr(w), whole(t, LANES), whole(t, LANES),
                  lyr(conv0), lyr(h0), lyr(cw), lyr(cb), lyr(wab), lyr(bab), lyr(lam)],
        out_specs=[whole(t, ATTN_W), whole(t, KV_W), whole(t, KV_W), whole(t, LRU_W),
                   whole(t, LRU_W), whole(N_SLABS, t, LANES), whole(n_seq, LRU_W)],
        out_shape=[jax.ShapeDtypeStruct((t, ATTN_W), F32),
                   jax.ShapeDtypeStruct((t, KV_W), F32),
                   jax.ShapeDtypeStruct((t, KV_W), F32),
                   jax.ShapeDtypeStruct((t, LRU_W), F32),
                   jax.ShapeDtypeStruct((t, LRU_W), F32),
                   jax.ShapeDtypeStruct((N_SLABS, t, LANES), F32),
                   jax.ShapeDtypeStruct((n_seq, LRU_W), F32)],
        compiler_params=_params(),
        name="inproj_sample",
    )(x, g, w, cos, sin, conv0, h0, cw, cb, wab, bab, lam)


MXU_COLS = 256
FF_SPLIT = (D_FF // MXU_COLS + 1) // 2 * MXU_COLS


def _gelu_tanh(x):
    return 0.5 * x * (1.0 + jnp.tanh(0.7978845608028654 * (x + 0.044715 * (x * x * x))))


def _mixer_out(x, att_ref, h_ref, gate_ref, ga_ref, gl_ref, wo_ref, gpm_ref, gpf_ref):
    att = _rms(att_ref[...], ga_ref[...]).astype(BF16)
    h = jnp.concatenate([h_ref[s] for s in range(N_SLABS)], axis=1)
    lru = _rms(h * _gelu_tanh(gate_ref[...]), gl_ref[...]).astype(BF16)
    m = _dot(att, wo_ref[0:ATTN_W, :]) + _dot(lru, wo_ref[ATTN_W:, :])
    x1 = x + _rms(m, gpm_ref[...])
    return x1, _rms(x1, gpf_ref[...]).astype(BF16)


def _swiglu(hf, wg_ref, wu_ref, wd_ref):
    f = None
    for lo, hi in ((0, FF_SPLIT), (FF_SPLIT, D_FF)):
        gt = _dot(hf, wg_ref[:, lo:hi])
        up = _dot(hf, wu_ref[:, lo:hi])
        hid = (gt * jax.nn.sigmoid(gt) * up).astype(BF16)
        part = _dot(hid, wd_ref[lo:hi, :])
        f = part if f is None else f + part
    return f


def _ffn_kernel(*refs, tiles_per_seq, meta_front):
    x_ref, refs = refs[0], refs[1:]
    meta_ref, refs = (refs[0], refs[1:]) if meta_front else (None, refs)
    (att_ref, h_ref, gate_ref, ga_ref, gl_ref, wo_ref, gpm_ref, gpf_ref,
     wg_ref, wu_ref, wd_ref, gpo_ref, o_ref) = refs
    x = _meta_front_tile(x_ref, meta_ref, tiles_per_seq) if meta_front else x_ref[...]
    x1, hf = _mixer_out(x, att_ref, h_ref, gate_ref, ga_ref, gl_ref, wo_ref, gpm_ref, gpf_ref)
    o_ref[...] = x1 + _rms(_swiglu(hf, wg_ref, wu_ref, wd_ref), gpo_ref[...])


def _ffn(layer, x, meta, att, h, gate, ga, gl, wo, gpm, gpf, wg, wu, wd, gpo,
         tm, tiles_per_seq, n_seq, drop_meta):
    keep = tiles_per_seq * tm
    seq = keep + drop_meta
    meta_front = meta is not None
    assert not (meta_front and drop_meta)
    r = SUBLANES
    assert tm % r == 0 and drop_meta % r == 0
    in_row = lambda i: r * ((i // tiles_per_seq) * (seq // r) + drop_meta // r
                            + (i % tiles_per_seq) * (tm // r))

    def rows(width):
        if drop_meta:
            return pl.BlockSpec((pl.Element(tm), pl.Element(width)), lambda i: (in_row(i), 0))
        return pl.BlockSpec((tm, width), lambda i: (i, 0))

    if drop_meta:
        h_spec = pl.BlockSpec((pl.Element(N_SLABS), pl.Element(tm), pl.Element(LANES)),
                              lambda i: (0, in_row(i), 0))
    else:
        h_spec = pl.BlockSpec((N_SLABS, tm, LANES), lambda i: (0, i, 0))
    out_spec = pl.BlockSpec((tm, D_MODEL), lambda i: (i, 0))
    if meta_front:
        x_specs = [_meta_front_spec(tm, tiles_per_seq, x.shape[0] // n_seq, meta.shape[0]),
                   _const_spec(meta.shape)]
        x_args = (x, meta)
    else:
        x_specs, x_args = [rows(D_MODEL)], (x,)
    big = lambda a: _layer_spec(a, layer, single_buffer=True)
    lyr = lambda a: _layer_spec(a, layer)
    return pl.pallas_call(
        functools.partial(_ffn_kernel, tiles_per_seq=tiles_per_seq, meta_front=meta_front),
        grid=(n_seq * tiles_per_seq,),
        in_specs=x_specs + [rows(ATTN_W), h_spec, rows(LRU_W), lyr(ga), lyr(gl),
                            big(wo), lyr(gpm), lyr(gpf), big(wg), big(wu), big(wd), lyr(gpo)],
        out_specs=out_spec,
        out_shape=jax.ShapeDtypeStruct((n_seq * keep, D_MODEL), F32),
        compiler_params=_params(),
        name="outffn",
    )(*x_args, att, h, gate, ga, gl, wo, gpm, gpf, wg, wu, wd, gpo)


def _kv_variants(k, v):
    lane = lax.broadcasted_iota(jnp.int32, k.shape, 1)
    low = lane < HEAD_DIM
    kr = pltpu.roll(k, HEAD_DIM, 1)
    vr = pltpu.roll(v, HEAD_DIM, 1)
    zero = jnp.zeros_like(k)
    ka = (jnp.where(low, k, zero), jnp.where(low, kr, zero))
    kb = (jnp.where(low, zero, kr), jnp.where(low, zero, k))
    vv = (jnp.where(low, v, vr), jnp.where(low, vr, v))
    cast = lambda pair: tuple(a.astype(BF16) for a in pair)
    return cast(ka), cast(kb), cast(vv)


def _attn_prompt_kernel(sink_ref, q_ref, k_ref, v_ref, o_ref,
                        ka_ref, kb_ref, vv_ref, *, layer, seq, chunk):
    for c in range(seq // chunk):
        rows = slice(c * chunk, (c + 1) * chunk)
        ka, kb, vv = _kv_variants(k_ref[0, rows, :], v_ref[0, rows, :])
        ones = jnp.ones((chunk, LANES), BF16)
        for h in range(N_KV_HEADS):
            ka_ref[h, rows, :] = ka[h]
            kb_ref[h, rows, :] = kb[h]
            vv_ref[h, rows, :] = jnp.concatenate([vv[h], ones], axis=1)

    def tile(q0, ks, nq, nk, sink_col):
        row = lax.broadcasted_iota(jnp.int32, (nq, nk), 0)
        col = lax.broadcasted_iota(jnp.int32, (nq, nk), 1)
        rel = row - col + (q0 - ks)
        mask = (rel >= 0) & (rel < WINDOW)
        col_row = lax.broadcasted_iota(jnp.int32, (1, nk), 1)
        key_row = lax.broadcasted_iota(jnp.int32, (nk, 2 * LANES), 0)
        val_lane = lax.broadcasted_iota(jnp.int32, (nk, 2 * LANES), 1)
        zero_v = (key_row == sink_col) & (val_lane < LANES)
        vvw = [jnp.where(zero_v, jnp.zeros((), BF16), vv_ref[h, pl.ds(ks, nk), :])
               for h in range(N_KV_HEADS)]
        low = lax.broadcasted_iota(jnp.int32, (nq, LANES), 1) < HEAD_DIM
        for p in range(N_PAIRS):
            h = p // PAIRS_PER_KV
            qp = q_ref[0, pl.ds(q0, nq), p * LANES:(p + 1) * LANES]
            halves = []
            for k_ref_, head in ((ka_ref, 2 * p), (kb_ref, 2 * p + 1)):
                s = _dot_nt(qp, k_ref_[h, pl.ds(ks, nk), :])
                fill = jnp.where(col_row == sink_col, sink_ref[layer, head], -jnp.inf)
                s = jnp.where(mask, s, fill)
                e = jnp.exp(s - jnp.max(s, axis=1, keepdims=True))
                pv = _dot(e.astype(BF16), vvw[h])
                halves.append(pv[:, :LANES] / pv[:, LANES:])
            o_ref[0, pl.ds(q0, nq), p * LANES:(p + 1) * LANES] = jnp.where(low, halves[0], halves[1])

    n_full = seq // WINDOW
    group = 4

    def body(it, carry):
        for t in range(group):
            i = it * group + t
            q0 = pl.multiple_of(i * WINDOW, WINDOW)
            ks = pl.multiple_of(jnp.maximum(q0 - WINDOW, 0), WINDOW)
            sink_col = jnp.where(i == 0, 2 * WINDOW - 1, 0) if t == 0 else 0
            tile(q0, ks, WINDOW, 2 * WINDOW, sink_col)
        return carry

    lax.fori_loop(0, n_full // group, body, 0)
    for i in range(n_full // group * group, n_full):
        tile(i * WINDOW, max(i - 1, 0) * WINDOW, WINDOW, 2 * WINDOW, 2 * WINDOW - 1 if i == 0 else 0)
    if seq > n_full * WINDOW:
        tile(seq - WINDOW, seq - 2 * WINDOW, WINDOW, 2 * WINDOW, 0)


def _attn_prompt(layer, sinks, q, k, v, chunk):
    b, seq, _ = q.shape
    assert seq >= 2 * WINDOW
    kern = functools.partial(_attn_prompt_kernel, layer=layer, seq=seq, chunk=chunk)
    seq_spec = lambda w_: pl.BlockSpec((1, seq, w_), lambda i: (i, 0, 0))
    return pl.pallas_call(
        kern,
        grid=(b,),
        in_specs=[pl.BlockSpec(memory_space=pltpu.SMEM),
                  seq_spec(ATTN_W), seq_spec(KV_W), seq_spec(KV_W)],
        out_specs=seq_spec(ATTN_W),
        out_shape=jax.ShapeDtypeStruct((b, seq, ATTN_W), F32),
        scratch_shapes=[pltpu.VMEM((N_KV_HEADS, seq, LANES), BF16),
                        pltpu.VMEM((N_KV_HEADS, seq, LANES), BF16),
                        pltpu.VMEM((N_KV_HEADS, seq, 2 * LANES), BF16)],
        compiler_params=_params(),
        name="attn_prompt",
    )(sinks, q, k, v)


def _lru_gates(u, wab, bab, logsig):
    ga = _dot(u.astype(BF16), wab) + bab
    r = jax.nn.sigmoid(ga[:, :LRU_W])
    ig = jax.nn.sigmoid(ga[:, LRU_W:])
    log_a = LRU_C * r * logsig
    a = jnp.exp(log_a)
    b = jnp.sqrt(-jnp.tanh(log_a) * (a * a + 1.0)) * ig * u
    return a, b


def _scan_interleaved(a_ref, b_ref, h0, rows):
    piece = rows // SUBLANES
    at = lambda i: slice(i * SUBLANES, (i + 1) * SUBLANES)
    hs = [jnp.zeros((SUBLANES, LANES), F32) for _ in range(N_SLABS)]
    ps = [jnp.ones((SUBLANES, LANES), F32) for _ in range(N_SLABS)]
    for i in range(piece):
        for s in range(N_SLABS):
            a = a_ref[s, at(i), :]
            hs[s] = a * hs[s] + b_ref[s, at(i), :]
            ps[s] = a * ps[s]
            b_ref[s, at(i), :] = hs[s]
            a_ref[s, at(i), :] = ps[s]

    sub = lax.broadcasted_iota(jnp.int32, (SUBLANES, LANES), 0)
    carry_in, h_last = [], []
    for s in range(N_SLABS):
        cin = jnp.broadcast_to(h0[:, s * LANES:(s + 1) * LANES], (SUBLANES, LANES))
        acc = jnp.zeros((SUBLANES, LANES), F32)
        for c in range(SUBLANES):
            acc = jnp.where(sub == c, cin, acc)
            end = hs[s] + ps[s] * cin
            cin = jnp.broadcast_to(end[c:c + 1, :], (SUBLANES, LANES))
        carry_in.append(acc)
        h_last.append(cin[0:1, :])

    for i in range(piece):
        for s in range(N_SLABS):
            b_ref[s, at(i), :] = b_ref[s, at(i), :] + a_ref[s, at(i), :] * carry_in[s]
    return jnp.concatenate(h_last, axis=1)


def _meta_front_tile(x_ref, meta_ref, tiles_per_seq):
    blk = x_ref[...]
    n_meta = meta_ref.shape[0]
    with_meta = jnp.concatenate([meta_ref[...], blk[:blk.shape[0] - n_meta]], axis=0)
    return jnp.where(pl.program_id(0) % tiles_per_seq == 0, with_meta, blk)


def _inproj_lru_kernel(*refs, tm, tiles_per_seq, meta_front):
    x_ref, refs = refs[0], refs[1:]
    meta_ref, refs = (refs[0], refs[1:]) if meta_front else (None, refs)
    (g_ref, w_ref, cos_ref, sin_ref, cw_ref, cb_ref, wab_ref, bab_ref, lam_ref,
     q_ref, k_ref, v_ref, h_out_ref, gate_ref, tail_ref, ht_ref,
     xs_ref, u_ref, a_ref, b_ref, h_ref) = refs
    pad = SUBLANES
    piece = tm // SUBLANES
    slab = lambda s: slice(s * LANES, (s + 1) * LANES)

    @pl.when(pl.program_id(0) % tiles_per_seq == 0)
    def _():
        xs_ref[:, 0:pad, :] = jnp.zeros((N_SLABS, pad, LANES), F32)
        h_ref[...] = jnp.zeros((1, LRU_W), F32)

    x = _meta_front_tile(x_ref, meta_ref, tiles_per_seq) if meta_front else x_ref[...]
    hn = _rms(x, g_ref[...]).astype(BF16)
    xb = _dot(hn, w_ref[:, XB_COL:GATE_COL])
    gate_ref[...] = _dot(hn, w_ref[:, GATE_COL:])
    tail_ref[0] = xb[tm - pad:, :]
    for s in range(N_SLABS):
        xs_ref[s, pad:pad + tm, :] = xb[:, slab(s)]

    for s in range(N_SLABS):
        bias = jnp.broadcast_to(cb_ref[:, slab(s)], (SUBLANES, LANES))
        wts = [jnp.broadcast_to(cw_ref[j:j + 1, slab(s)], (SUBLANES, LANES))
               for j in range(CONV_W)]
        taps = [xs_ref[s, pl.ds(pad - (CONV_W - 1) + k, SUBLANES, stride=piece), :]
                for k in range(CONV_W - 1)]
        for i in range(piece):
            taps.append(xs_ref[s, pl.ds(pad + i, SUBLANES, stride=piece), :])
            u = bias
            for j in range(CONV_W):
                u = u + taps[j] * wts[j]
            u_ref[i * SUBLANES:(i + 1) * SUBLANES, slab(s)] = u
            taps.pop(0)
    for s in range(N_SLABS):
        xs_ref[s, 0:pad, :] = xs_ref[s, tm:tm + pad, :]

    a, b = _lru_gates(u_ref[...], wab_ref[...], bab_ref[...], jax.nn.log_sigmoid(lam_ref[...]))
    for s in range(N_SLABS):
        a_ref[s] = a[:, slab(s)]
        b_ref[s] = b[:, slab(s)]
    h_last = _scan_interleaved(a_ref, b_ref, h_ref[...], tm)
    h_ref[...] = h_last
    ht_ref[0] = h_last
    for s in range(N_SLABS):
        for i in range(piece):
            h_out_ref[s, pl.ds(i, SUBLANES, stride=piece), :] = (
                b_ref[s, i * SUBLANES:(i + 1) * SUBLANES, :])

    _store_qkv(_dot(hn, w_ref[:, :XB_COL]), cos_ref, sin_ref, q_ref, k_ref, v_ref)


def _meta_front_spec(tm, tiles_per_seq, seq_in, n_meta):
    r = SUBLANES
    assert seq_in % r == 0 and tm % r == 0 and n_meta % r == 0

    def start(i):
        j = i % tiles_per_seq
        return r * ((i // tiles_per_seq) * (seq_in // r)
                    + jnp.maximum(j * (tm // r) - n_meta // r, 0)), 0
    return pl.BlockSpec((pl.Element(tm), pl.Element(D_MODEL)), start)


def _inproj_lru(layer, x, meta, g, w, cos, sin, cw, cb, wab, bab, lam, tm, tiles_per_seq, n_seq):
    t = n_seq * tiles_per_seq * tm
    assert tm % SUBLANES == 0 and tm >= SUBLANES
    meta_front = meta is not None
    kern = functools.partial(_inproj_lru_kernel, tm=tm, tiles_per_seq=tiles_per_seq,
                             meta_front=meta_front)
    tok = lambda w_: pl.BlockSpec((tm, w_), lambda i: (i, 0))
    tab = pl.BlockSpec((tm, LANES), lambda i: (i % tiles_per_seq, 0))
    per_seq = lambda r: pl.BlockSpec((1, r, LRU_W), lambda i: (i // tiles_per_seq, 0, 0))
    lyr = lambda a: _layer_spec(a, layer)
    if meta_front:
        x_specs = [_meta_front_spec(tm, tiles_per_seq, x.shape[0] // n_seq, meta.shape[0]),
                   _const_spec(meta.shape)]
        x_args = (x, meta)
    else:
        x_specs, x_args = [tok(D_MODEL)], (x,)
    return pl.pallas_call(
        kern,
        grid=(t // tm,),
        in_specs=x_specs + [lyr(g), lyr(w), tab, tab,
                            lyr(cw), lyr(cb), lyr(wab), lyr(bab), lyr(lam)],
        out_specs=[tok(ATTN_W), tok(KV_W), tok(KV_W),
                   pl.BlockSpec((N_SLABS, tm, LANES), lambda i: (0, i, 0)), tok(LRU_W),
                   per_seq(SUBLANES), per_seq(1)],
        out_shape=[jax.ShapeDtypeStruct((t, ATTN_W), BF16),
                   jax.ShapeDtypeStruct((t, KV_W), F32),
                   jax.ShapeDtypeStruct((t, KV_W), F32),
                   jax.ShapeDtypeStruct((N_SLABS, t, LANES), F32),
                   jax.ShapeDtypeStruct((t, LRU_W), F32),
                   jax.ShapeDtypeStruct((n_seq, SUBLANES, LRU_W), F32),
                   jax.ShapeDtypeStruct((n_seq, 1, LRU_W), F32)],
        scratch_shapes=[pltpu.VMEM((N_SLABS, tm + SUBLANES, LANES), F32),
                        pltpu.VMEM((tm, LRU_W), F32),
                        pltpu.VMEM((N_SLABS, tm, LANES), F32),
                        pltpu.VMEM((N_SLABS, tm, LANES), F32),
                        pltpu.VMEM((1, LRU_W), F32)],
        compiler_params=_params(),
        name="inproj_lru",
    )(*x_args, g, w, cos, sin, cw, cb, wab, bab, lam)


def _div_pow2(x, n):
    assert n & (n - 1) == 0
    return x >> (n.bit_length() - 1)


def _mod_pow2(x, n):
    assert n & (n - 1) == 0
    return x & (n - 1)


def _attn_sample_kernel(sink_ref, q_ref, kn_ref, vn_ref, ck_ref, cv_ref, nk_all_ref, nv_all_ref,
                        o_ref, nk_ref, nv_ref, *, layer, group, n_new):
    del nk_all_ref, nv_all_ref
    rpb = PAIRS_PER_KV * n_new
    new_pad = kn_ref.shape[1]
    rows = group * rpb
    ncols = group * new_pad
    row = lax.broadcasted_iota(jnp.int32, (rows, 1), 0)
    qpos = _mod_pow2(row, n_new)
    row_seq = _div_pow2(row, rpb)
    second_pair = _mod_pow2(_div_pow2(row, n_new), PAIRS_PER_KV) == 1
    col = lax.broadcasted_iota(jnp.int32, (1, WINDOW), 1)
    mask_c = col > qpos
    if PAST_LEN < WINDOW:
        mask_c = mask_c & (col >= WINDOW - PAST_LEN)
    ncol = lax.broadcasted_iota(jnp.int32, (1, ncols), 1)
    mask_n = (_div_pow2(ncol, new_pad) == row_seq) & (_mod_pow2(ncol, new_pad) <= qpos)
    low = lax.broadcasted_iota(jnp.int32, (rpb, LANES), 1) < HEAD_DIM
    zero_half = jnp.zeros((HEAD_DIM, LANES), BF16)

    kn_all = kn_ref[...].reshape(ncols, KV_W)
    vn_all = vn_ref[...].reshape(ncols, KV_W)
    knt_all = kn_all.T
    vnt_all = vn_all.T
    vn_lane = lax.broadcasted_iota(jnp.int32, vn_all.shape, 1) < HEAD_DIM
    vn_roll = pltpu.roll(vn_all, HEAD_DIM, 1)

    keep = WINDOW - n_new
    cache_lane = lax.broadcasted_iota(jnp.int32, (HEAD_DIM, WINDOW), 1) < keep

    for h in range(N_KV_HEADS):
        hd = slice(h * HEAD_DIM, (h + 1) * HEAD_DIM)
        q_all = q_ref[:, h].reshape(rows, LANES).astype(BF16)
        q_seq = [q_ref[b, h].astype(BF16) for b in range(group)]
        knt_h = knt_all[hd, :].astype(BF16)
        vvn = (jnp.where(vn_lane, vn_all, vn_roll) if h == 0
               else jnp.where(vn_lane, vn_roll, vn_all)).astype(BF16)
        kt, vt2 = [], []
        for b in range(group):
            kt.append(ck_ref[b, h].astype(BF16))
            vt = cv_ref[b, h].astype(BF16)
            vt2.append(jnp.concatenate([vt, vt], axis=0))
        probs, dens = [], []
        for first in (True, False):
            embed = lambda t: jnp.concatenate([t, zero_half] if first else [zero_half, t], axis=0)
            s_new = jnp.where(mask_n, _dot(q_all, embed(knt_h)), -jnp.inf)
            s_c = jnp.concatenate(
                [_dot(q_seq[b], embed(kt[b])) for b in range(group)], axis=0)
            s_c = jnp.where(mask_c, s_c, -jnp.inf)
            pair0 = 2 * h * PAIRS_PER_KV + (0 if first else 1)
            sink = jnp.where(second_pair, sink_ref[layer,pair0 + 2], sink_ref[layer,pair0])
            mx = jnp.maximum(jnp.maximum(jnp.max(s_c, axis=1, keepdims=True),
                                         jnp.max(s_new, axis=1, keepdims=True)), sink)
            p_c = jnp.exp(s_c - mx)
            p_n = jnp.exp(s_new - mx)
            dens.append(jnp.sum(p_c, axis=1, keepdims=True) + jnp.sum(p_n, axis=1, keepdims=True)
                        + jnp.exp(sink - mx))
            probs.append((p_c, _dot(p_n.astype(BF16), vvn)))
        for b in range(group):
            rs = slice(b * rpb, (b + 1) * rpb)
            pc = jnp.concatenate([probs[0][0][rs], probs[1][0][rs]], axis=0).astype(BF16)
            oc = _dot_nt(pc, vt2[b])
            o_first = (oc[:rpb] + probs[0][1][rs]) / dens[0][rs]
            o_second = (oc[rpb:] + probs[1][1][rs]) / dens[1][rs]
            o_ref[b, h] = jnp.where(low, o_first, o_second)
            shift = (keep - b * new_pad) % LANES
            nk_ref[b, h] = jnp.where(cache_lane, pltpu.roll(ck_ref[b, h], keep, 1),
                                     pltpu.roll(knt_all[hd, :], shift, 1))
            nv_ref[b, h] = jnp.where(cache_lane, pltpu.roll(cv_ref[b, h], keep, 1),
                                     pltpu.roll(vnt_all[hd, :], shift, 1))


def _attn_sample(layer, sinks, q, kn, vn, ck, cv, nk_all, nv_all, group, n_new):
    bs = q.shape[0]
    kern = functools.partial(_attn_sample_kernel, layer=layer, group=group, n_new=n_new)
    seq4 = lambda a: pl.BlockSpec((group,) + a.shape[1:], lambda i: (i, 0, 0, 0))
    seq3 = lambda a: pl.BlockSpec((group,) + a.shape[1:], lambda i: (i, 0, 0))
    cache = pl.BlockSpec((None, group) + ck.shape[2:], lambda i: (layer, i, 0, 0, 0))
    in_hbm = pl.BlockSpec(memory_space=pl.ANY)
    return pl.pallas_call(
        kern,
        grid=(bs // group,),
        in_specs=[pl.BlockSpec(memory_space=pltpu.SMEM),
                  seq4(q), seq3(kn), seq3(vn), cache, cache, in_hbm, in_hbm],
        out_specs=[seq4(q), cache, cache],
        out_shape=[jax.ShapeDtypeStruct(q.shape, F32),
                   jax.ShapeDtypeStruct(nk_all.shape, F32),
                   jax.ShapeDtypeStruct(nv_all.shape, F32)],
        input_output_aliases={6: 1, 7: 2},
        compiler_params=_params(),
        name="attn_sample",
    )(sinks, q, kn, vn, ck, cv, nk_all, nv_all)


def _rope_tables(pos):
    half = HEAD_DIM // 2
    inv = ROPE_THETA ** (-jnp.arange(half, dtype=F32) / half)
    ang = pos.astype(F32)[:, None] * inv[None, :]
    cos, sin = jnp.cos(ang), jnp.sin(ang)
    reps = LANES // HEAD_DIM
    return (jnp.concatenate([cos, cos] * reps, axis=1),
            jnp.concatenate([-sin, sin] * reps, axis=1))


def _block_diag(w):
    eye = jnp.eye(LRU_BLOCKS, dtype=w.dtype)
    full = w[:, :, :, None, :] * eye[None, :, None, :, None]
    return full.reshape(w.shape[0], LRU_W, LRU_W)


def _pick_tile(n, candidates):
    for c in candidates:
        if n % c == 0:
            return c
    raise ValueError(f"no tile for {n} tokens")


def kernel(x_prompt, x_sample, cache_k, cache_v, state_h, state_conv, meta_tokens, pre_mix_norm,
           w_in, sinks, conv_w, conv_b, w_a, b_a, w_i, b_i, lam, attn_out_norm, lru_out_norm,
           w_out, post_mix_norm, pre_ffn_norm, w_gate, w_up, w_down, post_ffn_norm):
    bp, seq_in, _ = x_prompt.shape
    seq = seq_in + N_META
    bs, n_new, _ = x_sample.shape
    assert seq % SUBLANES == 0 and (seq % WINDOW) % 16 == 0 and n_new >= CONV_W - 1

    vec = lambda a: a.reshape(DEPTH, 1, -1)
    w_in_b = w_in.astype(BF16)
    w_out_b = w_out.astype(BF16)
    w_gate_b = w_gate.astype(BF16)
    w_up_b = w_up.astype(BF16)
    w_down_b = w_down.astype(BF16)
    wab = jnp.concatenate([_block_diag(w_a), _block_diag(w_i)], axis=2).astype(BF16)
    bab = vec(jnp.concatenate([b_a, b_i], axis=1))
    g_pre, g_att, g_lru = vec(pre_mix_norm), vec(attn_out_norm), vec(lru_out_norm)
    g_pm, g_pf, g_po = vec(post_mix_norm), vec(pre_ffn_norm), vec(post_ffn_norm)
    cb, lam_v = vec(conv_b), vec(lam)

    assert DEPTH >= 2
    xp = x_prompt.reshape(bp * seq_in, D_MODEL)
    meta = meta_tokens.astype(x_prompt.dtype)
    xs = x_sample.transpose(1, 0, 2).reshape(n_new * bs, D_MODEL)

    cos_p, sin_p = _rope_tables(jnp.arange(seq, dtype=jnp.int32))
    pos_s = PAST_LEN + jnp.arange(n_new, dtype=jnp.int32)
    cos_s, sin_s = _rope_tables(jnp.repeat(pos_s, bs))

    tm_p = _pick_tile(seq, (688, 512, 256, 128, 16))
    tm_s = bs * n_new
    tps = seq // tm_p
    tm_y = _pick_tile(seq_in, (512, 256, 128, 16))
    new_pad = SUBLANES
    group = LANES // new_pad
    assert n_new <= new_pad and bs % group == 0 and WINDOW == LANES

    ck_t = cache_k.transpose(0, 1, 3, 4, 2)
    cv_t = cache_v.transpose(0, 1, 3, 4, 2)
    conv_s = state_conv.transpose(0, 2, 1, 3)

    nk_all = jnp.zeros(ck_t.shape, F32)
    nv_all = jnp.zeros(cv_t.shape, F32)
    pk, pv, ph, pc, sh, sc = [], [], [], [], [], []
    for l in range(DEPTH):
        meta_l = meta if l == 0 else None
        q, k, v, h_seq, gate, xb_tail, ht = _inproj_lru(
            l, xp, meta_l, g_pre, w_in_b, cos_p, sin_p, conv_w, cb, wab, bab, lam_v, tm_p, tps, bp)
        k3 = k.reshape(bp, seq, KV_W)
        v3 = v.reshape(bp, seq, KV_W)
        att = _attn_prompt(l, sinks, q.reshape(bp, seq, ATTN_W), k3, v3, tm_p)
        ffn_tiling = (tm_p, tps, bp, 0) if l < DEPTH - 1 else (tm_y, seq_in // tm_y, bp, N_META)
        xp = _ffn(l, xp, meta_l, att.reshape(bp * seq, ATTN_W), h_seq, gate,
                  g_att, g_lru, w_out_b, g_pm, g_pf, w_gate_b, w_up_b, w_down_b, g_po, *ffn_tiling)
        pk.append(k3[:, -WINDOW:].reshape(bp, WINDOW, N_KV_HEADS, HEAD_DIM))
        pv.append(v3[:, -WINDOW:].reshape(bp, WINDOW, N_KV_HEADS, HEAD_DIM))
        ph.append(ht.reshape(bp, LRU_W))
        pc.append(xb_tail[:, -(CONV_W - 1):])

        q, k, v, xb, gate, h_slabs, ht = _inproj_sample(
            l, xs, g_pre, w_in_b, cos_s, sin_s, conv_s, state_h, conv_w, cb, wab, bab, lam_v, n_new)
        q4 = q.reshape(n_new, bs, N_KV_HEADS, PAIRS_PER_KV, LANES).transpose(1, 2, 3, 0, 4).reshape(
            bs, N_KV_HEADS, PAIRS_PER_KV * n_new, LANES)
        seq_major = lambda a: a.reshape(n_new, bs, -1).transpose(1, 0, 2)
        padn = ((0, 0), (0, new_pad - n_new), (0, 0))
        o4, nk_all, nv_all = _attn_sample(l, sinks, q4, jnp.pad(seq_major(k), padn),
                                          jnp.pad(seq_major(v), padn), ck_t, cv_t,
                                          nk_all, nv_all, group, n_new)
        att_raw = o4.reshape(bs, N_KV_HEADS, PAIRS_PER_KV, n_new, LANES).transpose(
            3, 0, 1, 2, 4).reshape(n_new * bs, ATTN_W)
        xs = _ffn(l, xs, None, att_raw, h_slabs, gate,
                  g_att, g_lru, w_out_b, g_pm, g_pf, w_gate_b, w_up_b, w_down_b, g_po,
                  tm_s, 1, 1, 0)
        sh.append(ht)
        sc.append(seq_major(xb)[:, -(CONV_W - 1):])

    y_prompt = xp.reshape(bp, seq_in, D_MODEL)
    y_sample = xs.reshape(n_new, bs, D_MODEL).transpose(1, 0, 2)
    sample_k = nk_all.transpose(0, 1, 4, 2, 3)
    sample_v = nv_all.transpose(0, 1, 4, 2, 3)
    return (y_prompt, y_sample, jnp.stack(pk), jnp.stack(pv), jnp.stack(ph), jnp.stack(pc),
            sample_k, sample_v, jnp.stack(sh), jnp.stack(sc))
```

```python
import functools

import jax
import jax.numpy as jnp
from jax import lax
from jax.experimental import pallas as pl
from jax.experimental.pallas import tpu as pltpu

D_MODEL = 1024
DEPTH = 4
PAST_LEN = 8192
N_META = 16
HEAD_DIM = 64
N_Q_HEADS = 8
N_KV_HEADS = 2
ATTN_W = N_Q_HEADS * HEAD_DIM
KV_W = N_KV_HEADS * HEAD_DIM
LRU_W = D_MODEL - ATTN_W
LRU_BLOCKS = 8
LRU_BLOCK_W = LRU_W // LRU_BLOCKS
CONV_W = 4
LRU_C = 8.0
WINDOW = 128
ROPE_THETA = 10000.0
D_FF = 2816
IN_W = ATTN_W + 2 * KV_W + 2 * LRU_W
EPS = 1e-6

LANES = 128
SUBLANES = 8
N_SLABS = LRU_W // LANES
N_PAIRS = ATTN_W // LANES
PAIRS_PER_KV = N_PAIRS // N_KV_HEADS
VMEM_LIMIT = 56 * 1024 * 1024

BF16 = jnp.bfloat16
F32 = jnp.float32


def _rms(x, g):
    ms = jnp.mean(x * x, axis=-1, keepdims=True)
    return x * lax.rsqrt(ms + EPS) * g


def _dot(a, b):
    return jnp.dot(a, b, preferred_element_type=F32)


def _dot_nt(a, b):
    return lax.dot_general(a, b, (((1,), (1,)), ((), ())), preferred_element_type=F32)


def _const_spec(shape):
    zeros = (0,) * len(shape)
    return pl.BlockSpec(shape, lambda *_: zeros)


def _layer_spec(arr, layer, single_buffer=False):
    tail = (0,) * (arr.ndim - 1)
    mode = dict(pipeline_mode=pl.Buffered(1)) if single_buffer else {}
    return pl.BlockSpec((None,) + arr.shape[1:], lambda *_: (layer,) + tail, **mode)


def _params(n_axes=1):
    return pltpu.CompilerParams(dimension_semantics=("arbitrary",) * n_axes,
                                vmem_limit_bytes=VMEM_LIMIT)


XB_COL = ATTN_W + 2 * KV_W
GATE_COL = XB_COL + LRU_W


def _store_qkv(z, cos_ref, sin_ref, q_ref, k_ref, v_ref):
    cos = cos_ref[...]
    sin = sin_ref[...]
    lane = lax.broadcasted_iota(jnp.int32, cos.shape, 1)
    first_half = (lane & (HEAD_DIM // 2)) == 0

    def rope(t):
        swapped = jnp.where(first_half,
                            pltpu.roll(t, LANES - HEAD_DIM // 2, 1),
                            pltpu.roll(t, HEAD_DIM // 2, 1))
        return t * cos + swapped * sin

    for p in range(N_PAIRS):
        qp = rope(z[:, p * LANES:(p + 1) * LANES]) * (HEAD_DIM ** -0.5)
        q_ref[:, p * LANES:(p + 1) * LANES] = qp.astype(q_ref.dtype)
    k_ref[...] = rope(z[:, ATTN_W:ATTN_W + KV_W])
    v_ref[...] = z[:, ATTN_W + KV_W:XB_COL]


def _inproj_sample_kernel(x_ref, g_ref, w_ref, cos_ref, sin_ref, conv0_ref, h0_ref, cw_ref, cb_ref,
                          wab_ref, bab_ref, lam_ref,
                          q_ref, k_ref, v_ref, xb_ref, gate_ref, h_out_ref, ht_ref, *, n_new):
    n_seq = h0_ref.shape[0]
    hn = _rms(x_ref[...], g_ref[...]).astype(BF16)
    z = _dot(hn, w_ref[...])
    _store_qkv(z[:, :XB_COL], cos_ref, sin_ref, q_ref, k_ref, v_ref)
    xb = z[:, XB_COL:GATE_COL]
    xb_ref[...] = xb
    gate_ref[...] = z[:, GATE_COL:]

    step = lambda a, t: a[t * n_seq:(t + 1) * n_seq, :]
    xs = [conv0_ref[j] for j in range(CONV_W - 1)] + [step(xb, t) for t in range(n_new)]
    us = []
    for t in range(n_new):
        u = cb_ref[...]
        for j in range(CONV_W):
            u = u + xs[t + j] * cw_ref[j:j + 1, :]
        us.append(u)
    a, b = _lru_gates(jnp.concatenate(us, axis=0), wab_ref[...], bab_ref[...],
                      jax.nn.log_sigmoid(lam_ref[...]))
    h = h0_ref[...]
    for t in range(n_new):
        h = step(a, t) * h + step(b, t)
        for s in range(N_SLABS):
            h_out_ref[s, t * n_seq:(t + 1) * n_seq, :] = h[:, s * LANES:(s + 1) * LANES]
    ht_ref[...] = h


def _inproj_sample(layer, x, g, w, cos, sin, conv0, h0, cw, cb, wab, bab, lam, n_new):
    t = x.shape[0]
    n_seq = t // n_new
    kern = functools.partial(_inproj_sample_kernel, n_new=n_new)
    lyr = lambda a: _layer_spec(a, layer)
    whole = lambda *shape: _const_spec(shape)
    return pl.pallas_call(
        kern,
        grid=(1,),
        in_specs=[whole(t, D_MODEL), lyr(g), lyr(w), whole(t, LANES), whole(t, LANES),
                  lyr(conv0), lyr(h0), lyr(cw), lyr(cb), lyr(wab), lyr(bab), lyr(lam)],
        out_specs=[whole(t, ATTN_W), whole(t, KV_W), whole(t, KV_W), whole(t, LRU_W),
                   whole(t, LRU_W), whole(N_SLABS, t, LANES), whole(n_seq, LRU_W)],
        out_shape=[jax.ShapeDtypeStruct((t, ATTN_W), F32),
                   jax.ShapeDtypeStruct((t, KV_W), F32),
                   jax.ShapeDtypeStruct((t, KV_W), F32),
                   jax.ShapeDtypeStruct((t, LRU_W), F32),
                   jax.ShapeDtypeStruct((t, LRU_W), F32),
                   jax.ShapeDtypeStruct((N_SLABS, t, LANES), F32),
                   jax.ShapeDtypeStruct((n_seq, LRU_W), F32)],
        compiler_params=_params(),
        name="inproj_sample",
    )(x, g, w, cos, sin, conv0, h0, cw, cb, wab, bab, lam)


MXU_COLS = 256
FF_SPLIT = (D_FF // MXU_COLS + 1) // 2 * MXU_COLS


def _gelu_tanh(x):
    return 0.5 * x * (1.0 + jnp.tanh(0.7978845608028654 * (x + 0.044715 * (x * x * x))))


def _mixer_out(x, att_ref, h_ref, gate_ref, ga_ref, gl_ref, wo_ref, gpm_ref, gpf_ref):
    att = _rms(att_ref[...], ga_ref[...]).astype(BF16)
    h = jnp.concatenate([h_ref[s] for s in range(N_SLABS)], axis=1)
    lru = _rms(h * _gelu_tanh(gate_ref[...]), gl_ref[...]).astype(BF16)
    m = _dot(att, wo_ref[0:ATTN_W, :]) + _dot(lru, wo_ref[ATTN_W:, :])
    x1 = x + _rms(m, gpm_ref[...])
    return x1, _rms(x1, gpf_ref[...]).astype(BF16)


def _swiglu(hf, wg_ref, wu_ref, wd_ref):
    f = None
    for lo, hi in ((0, FF_SPLIT), (FF_SPLIT, D_FF)):
        gt = _dot(hf, wg_ref[:, lo:hi])
        up = _dot(hf, wu_ref[:, lo:hi])
        hid = (gt * jax.nn.sigmoid(gt) * up).astype(BF16)
        part = _dot(hid, wd_ref[lo:hi, :])
        f = part if f is None else f + part
    return f


def _ffn_kernel(*refs, tiles_per_seq, meta_front):
    x_ref, refs = refs[0], refs[1:]
    meta_ref, refs = (refs[0], refs[1:]) if meta_front else (None, refs)
    (att_ref, h_ref, gate_ref, ga_ref, gl_ref, wo_ref, gpm_ref, gpf_ref,
     wg_ref, wu_ref, wd_ref, gpo_ref, o_ref) = refs
    x = _meta_front_tile(x_ref, meta_ref, tiles_per_seq) if meta_front else x_ref[...]
    x1, hf = _mixer_out(x, att_ref, h_ref, gate_ref, ga_ref, gl_ref, wo_ref, gpm_ref, gpf_ref)
    o_ref[...] = x1 + _rms(_swiglu(hf, wg_ref, wu_ref, wd_ref), gpo_ref[...])


def _ffn(layer, x, meta, att, h, gate, ga, gl, wo, gpm, gpf, wg, wu, wd, gpo,
         tm, tiles_per_seq, n_seq, drop_meta):
    keep = tiles_per_seq * tm
    seq = keep + drop_meta
    meta_front = meta is not None
    assert not (meta_front and drop_meta)
    r = SUBLANES
    assert tm % r == 0 and drop_meta % r == 0
    in_row = lambda i: r * ((i // tiles_per_seq) * (seq // r) + drop_meta // r
                            + (i % tiles_per_seq) * (tm // r))

    def rows(width):
        if drop_meta:
            return pl.BlockSpec((pl.Element(tm), pl.Element(width)), lambda i: (in_row(i), 0))
        return pl.BlockSpec((tm, width), lambda i: (i, 0))

    if drop_meta:
        h_spec = pl.BlockSpec((pl.Element(N_SLABS), pl.Element(tm), pl.Element(LANES)),
                              lambda i: (0, in_row(i), 0))
    else:
        h_spec = pl.BlockSpec((N_SLABS, tm, LANES), lambda i: (0, i, 0))
    out_spec = pl.BlockSpec((tm, D_MODEL), lambda i: (i, 0))
    if meta_front:
        x_specs = [_meta_front_spec(tm, tiles_per_seq, x.shape[0] // n_seq, meta.shape[0]),
                   _const_spec(meta.shape)]
        x_args = (x, meta)
    else:
        x_specs, x_args = [rows(D_MODEL)], (x,)
    big = lambda a: _layer_spec(a, layer, single_buffer=True)
    lyr = lambda a: _layer_spec(a, layer)
    return pl.pallas_call(
        functools.partial(_ffn_kernel, tiles_per_seq=tiles_per_seq, meta_front=meta_front),
        grid=(n_seq * tiles_per_seq,),
        in_specs=x_specs + [rows(ATTN_W), h_spec, rows(LRU_W), lyr(ga), lyr(gl),
                            big(wo), lyr(gpm), lyr(gpf), big(wg), big(wu), big(wd), lyr(gpo)],
        out_specs=out_spec,
        out_shape=jax.ShapeDtypeStruct((n_seq * keep, D_MODEL), F32),
        compiler_params=_params(),
        name="outffn",
    )(*x_args, att, h, gate, ga, gl, wo, gpm, gpf, wg, wu, wd, gpo)


def _kv_variants(k, v):
    lane = lax.broadcasted_iota(jnp.int32, k.shape, 1)
    low = lane < HEAD_DIM
    kr = pltpu.roll(k, HEAD_DIM, 1)
    vr = pltpu.roll(v, HEAD_DIM, 1)
    zero = jnp.zeros_like(k)
    ka = (jnp.where(low, k, zero), jnp.where(low, kr, zero))
    kb = (jnp.where(low, zero, kr), jnp.where(low, zero, k))
    vv = (jnp.where(low, v, vr), jnp.where(low, vr, v))
    cast = lambda pair: tuple(a.astype(BF16) for a in pair)
    return cast(ka), cast(kb), cast(vv)


def _attn_prompt_kernel(sink_ref, q_ref, k_ref, v_ref, o_ref,
                        ka_ref, kb_ref, vv_ref, *, layer, seq, chunk):
    for c in range(seq // chunk):
        rows = slice(c * chunk, (c + 1) * chunk)
        ka, kb, vv = _kv_variants(k_ref[0, rows, :], v_ref[0, rows, :])
        ones = jnp.ones((chunk, LANES), BF16)
        for h in range(N_KV_HEADS):
            ka_ref[h, rows, :] = ka[h]
            kb_ref[h, rows, :] = kb[h]
            vv_ref[h, rows, :] = jnp.concatenate([vv[h], ones], axis=1)

    def tile(q0, ks, nq, nk, sink_col):
        row = lax.broadcasted_iota(jnp.int32, (nq, nk), 0)
        col = lax.broadcasted_iota(jnp.int32, (nq, nk), 1)
        rel = row - col + (q0 - ks)
        mask = (rel >= 0) & (rel < WINDOW)
        col_row = lax.broadcasted_iota(jnp.int32, (1, nk), 1)
        key_row = lax.broadcasted_iota(jnp.int32, (nk, 2 * LANES), 0)
        val_lane = lax.broadcasted_iota(jnp.int32, (nk, 2 * LANES), 1)
        zero_v = (key_row == sink_col) & (val_lane < LANES)
        vvw = [jnp.where(zero_v, jnp.zeros((), BF16), vv_ref[h, pl.ds(ks, nk), :])
               for h in range(N_KV_HEADS)]
        low = lax.broadcasted_iota(jnp.int32, (nq, LANES), 1) < HEAD_DIM
        for p in range(N_PAIRS):
            h = p // PAIRS_PER_KV
            qp = q_ref[0, pl.ds(q0, nq), p * LANES:(p + 1) * LANES]
            halves = []
            for k_ref_, head in ((ka_ref, 2 * p), (kb_ref, 2 * p + 1)):
                s = _dot_nt(qp, k_ref_[h, pl.ds(ks, nk), :])
                fill = jnp.where(col_row == sink_col, sink_ref[layer, head], -jnp.inf)
                s = jnp.where(mask, s, fill)
                e = jnp.exp(s - jnp.max(s, axis=1, keepdims=True))
                pv = _dot(e.astype(BF16), vvw[h])
                halves.append(pv[:, :LANES] / pv[:, LANES:])
            o_ref[0, pl.ds(q0, nq), p * LANES:(p + 1) * LANES] = jnp.where(low, halves[0], halves[1])

    n_full = seq // WINDOW
    group = 8

    def body(it, carry):
        for t in range(group):
            i = it * group + t
            q0 = pl.multiple_of(i * WINDOW, WINDOW)
            ks = pl.multiple_of(jnp.maximum(q0 - WINDOW, 0), WINDOW)
            sink_col = jnp.where(i == 0, 2 * WINDOW - 1, 0) if t == 0 else 0
            tile(q0, ks, WINDOW, 2 * WINDOW, sink_col)
        return carry

    lax.fori_loop(0, n_full // group, body, 0)
    for i in range(n_full // group * group, n_full):
        tile(i * WINDOW, max(i - 1, 0) * WINDOW, WINDOW, 2 * WINDOW, 2 * WINDOW - 1 if i == 0 else 0)
    if seq > n_full * WINDOW:
        tile(seq - WINDOW, seq - 2 * WINDOW, WINDOW, 2 * WINDOW, 0)


def _attn_prompt(layer, sinks, q, k, v, chunk):
    b, seq, _ = q.shape
    assert seq >= 2 * WINDOW
    kern = functools.partial(_attn_prompt_kernel, layer=layer, seq=seq, chunk=chunk)
    seq_spec = lambda w_: pl.BlockSpec((1, seq, w_), lambda i: (i, 0, 0))
    return pl.pallas_call(
        kern,
        grid=(b,),
        in_specs=[pl.BlockSpec(memory_space=pltpu.SMEM),
                  seq_spec(ATTN_W), seq_spec(KV_W), seq_spec(KV_W)],
        out_specs=seq_spec(ATTN_W),
        out_shape=jax.ShapeDtypeStruct((b, seq, ATTN_W), F32),
        scratch_shapes=[pltpu.VMEM((N_KV_HEADS, seq, LANES), BF16),
                        pltpu.VMEM((N_KV_HEADS, seq, LANES), BF16),
                        pltpu.VMEM((N_KV_HEADS, seq, 2 * LANES), BF16)],
        compiler_params=_params(),
        name="attn_prompt",
    )(sinks, q, k, v)


def _lru_gates(u, wab, bab, logsig):
    ga = _dot(u.astype(BF16), wab) + bab
    r = jax.nn.sigmoid(ga[:, :LRU_W])
    ig = jax.nn.sigmoid(ga[:, LRU_W:])
    log_a = LRU_C * r * logsig
    a = jnp.exp(log_a)
    b = jnp.sqrt(-jnp.tanh(log_a) * (a * a + 1.0)) * ig * u
    return a, b


def _scan_interleaved(a_ref, b_ref, h0, rows):
    piece = rows // SUBLANES
    at = lambda i: slice(i * SUBLANES, (i + 1) * SUBLANES)
    hs = [jnp.zeros((SUBLANES, LANES), F32) for _ in range(N_SLABS)]
    ps = [jnp.ones((SUBLANES, LANES), F32) for _ in range(N_SLABS)]
    for i in range(piece):
        for s in range(N_SLABS):
            a = a_ref[s, at(i), :]
            hs[s] = a * hs[s] + b_ref[s, at(i), :]
            ps[s] = a * ps[s]
            b_ref[s, at(i), :] = hs[s]
            a_ref[s, at(i), :] = ps[s]

    sub = lax.broadcasted_iota(jnp.int32, (SUBLANES, LANES), 0)
    carry_in, h_last = [], []
    for s in range(N_SLABS):
        cin = jnp.broadcast_to(h0[:, s * LANES:(s + 1) * LANES], (SUBLANES, LANES))
        acc = jnp.zeros((SUBLANES, LANES), F32)
        for c in range(SUBLANES):
            acc = jnp.where(sub == c, cin, acc)
            end = hs[s] + ps[s] * cin
            cin = jnp.broadcast_to(end[c:c + 1, :], (SUBLANES, LANES))
        carry_in.append(acc)
        h_last.append(cin[0:1, :])

    for i in range(piece):
        for s in range(N_SLABS):
            b_ref[s, at(i), :] = b_ref[s, at(i), :] + a_ref[s, at(i), :] * carry_in[s]
    return jnp.concatenate(h_last, axis=1)


def _meta_front_tile(x_ref, meta_ref, tiles_per_seq):
    blk = x_ref[...]
    n_meta = meta_ref.shape[0]
    with_meta = jnp.concatenate([meta_ref[...], blk[:blk.shape[0] - n_meta]], axis=0)
    return jnp.where(pl.program_id(0) % tiles_per_seq == 0, with_meta, blk)


def _inproj_lru_kernel(*refs, tm, tiles_per_seq, meta_front):
    x_ref, refs = refs[0], refs[1:]
    meta_ref, refs = (refs[0], refs[1:]) if meta_front else (None, refs)
    (g_ref, w_ref, cos_ref, sin_ref, cw_ref, cb_ref, wab_ref, bab_ref, lam_ref,
     q_ref, k_ref, v_ref, h_out_ref, gate_ref, tail_ref, ht_ref,
     xs_ref, u_ref, a_ref, b_ref, h_ref) = refs
    pad = SUBLANES
    piece = tm // SUBLANES
    slab = lambda s: slice(s * LANES, (s + 1) * LANES)

    @pl.when(pl.program_id(0) % tiles_per_seq == 0)
    def _():
        xs_ref[:, 0:pad, :] = jnp.zeros((N_SLABS, pad, LANES), F32)
        h_ref[...] = jnp.zeros((1, LRU_W), F32)

    x = _meta_front_tile(x_ref, meta_ref, tiles_per_seq) if meta_front else x_ref[...]
    hn = _rms(x, g_ref[...]).astype(BF16)
    xb = _dot(hn, w_ref[:, XB_COL:GATE_COL])
    gate_ref[...] = _dot(hn, w_ref[:, GATE_COL:])
    tail_ref[0] = xb[tm - pad:, :]
    for s in range(N_SLABS):
        xs_ref[s, pad:pad + tm, :] = xb[:, slab(s)]

    for s in range(N_SLABS):
        bias = jnp.broadcast_to(cb_ref[:, slab(s)], (SUBLANES, LANES))
        wts = [jnp.broadcast_to(cw_ref[j:j + 1, slab(s)], (SUBLANES, LANES))
               for j in range(CONV_W)]
        taps = [xs_ref[s, pl.ds(pad - (CONV_W - 1) + k, SUBLANES, stride=piece), :]
                for k in range(CONV_W - 1)]
        for i in range(piece):
            taps.append(xs_ref[s, pl.ds(pad + i, SUBLANES, stride=piece), :])
            u = bias
            for j in range(CONV_W):
                u = u + taps[j] * wts[j]
            u_ref[i * SUBLANES:(i + 1) * SUBLANES, slab(s)] = u
            taps.pop(0)
    for s in range(N_SLABS):
        xs_ref[s, 0:pad, :] = xs_ref[s, tm:tm + pad, :]

    a, b = _lru_gates(u_ref[...], wab_ref[...], bab_ref[...], jax.nn.log_sigmoid(lam_ref[...]))
    for s in range(N_SLABS):
        a_ref[s] = a[:, slab(s)]
        b_ref[s] = b[:, slab(s)]
    h_last = _scan_interleaved(a_ref, b_ref, h_ref[...], tm)
    h_ref[...] = h_last
    ht_ref[0] = h_last
    for s in range(N_SLABS):
        for i in range(piece):
            h_out_ref[s, pl.ds(i, SUBLANES, stride=piece), :] = (
                b_ref[s, i * SUBLANES:(i + 1) * SUBLANES, :])

    _store_qkv(_dot(hn, w_ref[:, :XB_COL]), cos_ref, sin_ref, q_ref, k_ref, v_ref)


def _meta_front_spec(tm, tiles_per_seq, seq_in, n_meta):
    r = SUBLANES
    assert seq_in % r == 0 and tm % r == 0 and n_meta % r == 0

    def start(i):
        j = i % tiles_per_seq
        return r * ((i // tiles_per_seq) * (seq_in // r)
                    + jnp.maximum(j * (tm // r) - n_meta // r, 0)), 0
    return pl.BlockSpec((pl.Element(tm), pl.Element(D_MODEL)), start)


def _inproj_lru(layer, x, meta, g, w, cos, sin, cw, cb, wab, bab, lam, tm, tiles_per_seq, n_seq):
    t = n_seq * tiles_per_seq * tm
    assert tm % SUBLANES == 0 and tm >= SUBLANES
    meta_front = meta is not None
    kern = functools.partial(_inproj_lru_kernel, tm=tm, tiles_per_seq=tiles_per_seq,
                             meta_front=meta_front)
    tok = lambda w_: pl.BlockSpec((tm, w_), lambda i: (i, 0))
    tab = pl.BlockSpec((tm, LANES), lambda i: (i % tiles_per_seq, 0))
    per_seq = lambda r: pl.BlockSpec((1, r, LRU_W), lambda i: (i // tiles_per_seq, 0, 0))
    lyr = lambda a: _layer_spec(a, layer)
    if meta_front:
        x_specs = [_meta_front_spec(tm, tiles_per_seq, x.shape[0] // n_seq, meta.shape[0]),
                   _const_spec(meta.shape)]
        x_args = (x, meta)
    else:
        x_specs, x_args = [tok(D_MODEL)], (x,)
    return pl.pallas_call(
        kern,
        grid=(t // tm,),
        in_specs=x_specs + [lyr(g), lyr(w), tab, tab,
                            lyr(cw), lyr(cb), lyr(wab), lyr(bab), lyr(lam)],
        out_specs=[tok(ATTN_W), tok(KV_W), tok(KV_W),
                   pl.BlockSpec((N_SLABS, tm, LANES), lambda i: (0, i, 0)), tok(LRU_W),
                   per_seq(SUBLANES), per_seq(1)],
        out_shape=[jax.ShapeDtypeStruct((t, ATTN_W), BF16),
                   jax.ShapeDtypeStruct((t, KV_W), F32),
                   jax.ShapeDtypeStruct((t, KV_W), F32),
                   jax.ShapeDtypeStruct((N_SLABS, t, LANES), F32),
                   jax.ShapeDtypeStruct((t, LRU_W), F32),
                   jax.ShapeDtypeStruct((n_seq, SUBLANES, LRU_W), F32),
                   jax.ShapeDtypeStruct((n_seq, 1, LRU_W), F32)],
        scratch_shapes=[pltpu.VMEM((N_SLABS, tm + SUBLANES, LANES), F32),
                        pltpu.VMEM((tm, LRU_W), F32),
                        pltpu.VMEM((N_SLABS, tm, LANES), F32),
                        pltpu.VMEM((N_SLABS, tm, LANES), F32),
                        pltpu.VMEM((1, LRU_W), F32)],
        compiler_params=_params(),
        name="inproj_lru",
    )(*x_args, g, w, cos, sin, cw, cb, wab, bab, lam)


def _div_pow2(x, n):
    assert n & (n - 1) == 0
    return x >> (n.bit_length() - 1)


def _mod_pow2(x, n):
    assert n & (n - 1) == 0
    return x & (n - 1)


def _attn_sample_kernel(sink_ref, q_ref, kn_ref, vn_ref, ck_ref, cv_ref, nk_all_ref, nv_all_ref,
                        o_ref, nk_ref, nv_ref, *, layer, group, n_new):
    del nk_all_ref, nv_all_ref
    rpb = PAIRS_PER_KV * n_new
    new_pad = kn_ref.shape[1]
    rows = group * rpb
    ncols = group * new_pad
    row = lax.broadcasted_iota(jnp.int32, (rows, 1), 0)
    qpos = _mod_pow2(row, n_new)
    row_seq = _div_pow2(row, rpb)
    second_pair = _mod_pow2(_div_pow2(row, n_new), PAIRS_PER_KV) == 1
    col = lax.broadcasted_iota(jnp.int32, (1, WINDOW), 1)
    mask_c = col > qpos
    if PAST_LEN < WINDOW:
        mask_c = mask_c & (col >= WINDOW - PAST_LEN)
    ncol = lax.broadcasted_iota(jnp.int32, (1, ncols), 1)
    mask_n = (_div_pow2(ncol, new_pad) == row_seq) & (_mod_pow2(ncol, new_pad) <= qpos)
    low = lax.broadcasted_iota(jnp.int32, (rpb, LANES), 1) < HEAD_DIM

    kn_all = kn_ref[...].reshape(ncols, KV_W)
    vn_all = vn_ref[...].reshape(ncols, KV_W)
    knt_all = kn_all.T
    vnt_all = vn_all.T
    vn_lane = lax.broadcasted_iota(jnp.int32, vn_all.shape, 1) < HEAD_DIM
    vn_roll = pltpu.roll(vn_all, HEAD_DIM, 1)

    keep = WINDOW - n_new
    cache_lane = lax.broadcasted_iota(jnp.int32, (HEAD_DIM, WINDOW), 1) < keep

    for h in range(N_KV_HEADS):
        hd = slice(h * HEAD_DIM, (h + 1) * HEAD_DIM)
        q_all = q_ref[:, h].reshape(rows, LANES).astype(BF16)
        q_seq = [q_ref[b, h].astype(BF16) for b in range(group)]
        knt_h = knt_all[hd, :].astype(BF16)
        vvn = (jnp.where(vn_lane, vn_all, vn_roll) if h == 0
               else jnp.where(vn_lane, vn_roll, vn_all)).astype(BF16)
        kt, vt2 = [], []
        for b in range(group):
            kt.append(ck_ref[b, h].astype(BF16))
            vt = cv_ref[b, h].astype(BF16)
            vt2.append(jnp.concatenate([vt, vt], axis=0))
        probs, dens = [], []
        for first in (True, False):
            embed = lambda t: jnp.concatenate(
                [t, jnp.zeros_like(t)] if first else [jnp.zeros_like(t), t], axis=0)
            s_new = jnp.where(mask_n, _dot(q_all, embed(knt_h)), -jnp.inf)
            s_c = jnp.concatenate(
                [_dot(q_seq[b], embed(kt[b])) for b in range(group)], axis=0)
            s_c = jnp.where(mask_c, s_c, -jnp.inf)
            pair0 = 2 * h * PAIRS_PER_KV + (0 if first else 1)
            sink = jnp.where(second_pair, sink_ref[layer,pair0 + 2], sink_ref[layer,pair0])
            mx = jnp.maximum(jnp.maximum(jnp.max(s_c, axis=1, keepdims=True),
                                         jnp.max(s_new, axis=1, keepdims=True)), sink)
            p_c = jnp.exp(s_c - mx)
            p_n = jnp.exp(s_new - mx)
            dens.append(jnp.sum(p_c, axis=1, keepdims=True) + jnp.sum(p_n, axis=1, keepdims=True)
                        + jnp.exp(sink - mx))
            probs.append((p_c, _dot(p_n.astype(BF16), vvn)))
        for b in range(group):
            rs = slice(b * rpb, (b + 1) * rpb)
            pc = jnp.concatenate([probs[0][0][rs], probs[1][0][rs]], axis=0).astype(BF16)
            oc = _dot_nt(pc, vt2[b])
            o_first = (oc[:rpb] + probs[0][1][rs]) / dens[0][rs]
            o_second = (oc[rpb:] + probs[1][1][rs]) / dens[1][rs]
            o_ref[b, h] = jnp.where(low, o_first, o_second)
            shift = (keep - b * new_pad) % ncols
            nk_ref[b, h] = jnp.where(cache_lane, pltpu.roll(ck_ref[b, h], keep, 1),
                                     pltpu.roll(knt_all[hd, :], shift, 1)[:, :WINDOW])
            nv_ref[b, h] = jnp.where(cache_lane, pltpu.roll(cv_ref[b, h], keep, 1),
                                     pltpu.roll(vnt_all[hd, :], shift, 1)[:, :WINDOW])


def _attn_sample(layer, sinks, q, kn, vn, ck, cv, nk_all, nv_all, group, n_new):
    bs = q.shape[0]
    kern = functools.partial(_attn_sample_kernel, layer=layer, group=group, n_new=n_new)
    seq4 = lambda a: pl.BlockSpec((group,) + a.shape[1:], lambda i: (i, 0, 0, 0))
    seq3 = lambda a: pl.BlockSpec((group,) + a.shape[1:], lambda i: (i, 0, 0))
    cache = pl.BlockSpec((None, group) + ck.shape[2:], lambda i: (layer, i, 0, 0, 0))
    in_hbm = pl.BlockSpec(memory_space=pl.ANY)
    return pl.pallas_call(
        kern,
        grid=(bs // group,),
        in_specs=[pl.BlockSpec(memory_space=pltpu.SMEM),
                  seq4(q), seq3(kn), seq3(vn), cache, cache, in_hbm, in_hbm],
        out_specs=[seq4(q), cache, cache],
        out_shape=[jax.ShapeDtypeStruct(q.shape, F32),
                   jax.ShapeDtypeStruct(nk_all.shape, F32),
                   jax.ShapeDtypeStruct(nv_all.shape, F32)],
        input_output_aliases={6: 1, 7: 2},
        compiler_params=_params(),
        name="attn_sample",
    )(sinks, q, kn, vn, ck, cv, nk_all, nv_all)


def _rope_tables(pos):
    half = HEAD_DIM // 2
    inv = ROPE_THETA ** (-jnp.arange(half, dtype=F32) / half)
    ang = pos.astype(F32)[:, None] * inv[None, :]
    cos, sin = jnp.cos(ang), jnp.sin(ang)
    reps = LANES // HEAD_DIM
    return (jnp.concatenate([cos, cos] * reps, axis=1),
            jnp.concatenate([-sin, sin] * reps, axis=1))


def _block_diag(w):
    rows = [jnp.pad(w[:, h], ((0, 0), (0, 0), (h * LRU_BLOCK_W, (LRU_BLOCKS - 1 - h) * LRU_BLOCK_W)))
            for h in range(LRU_BLOCKS)]
    return jnp.concatenate(rows, axis=1)


def _pick_tile(n, candidates):
    for c in candidates:
        if n % c == 0:
            return c
    raise ValueError(f"no tile for {n} tokens")


def kernel(x_prompt, x_sample, cache_k, cache_v, state_h, state_conv, meta_tokens, pre_mix_norm,
           w_in, sinks, conv_w, conv_b, w_a, b_a, w_i, b_i, lam, attn_out_norm, lru_out_norm,
           w_out, post_mix_norm, pre_ffn_norm, w_gate, w_up, w_down, post_ffn_norm):
    bp, seq_in, _ = x_prompt.shape
    seq = seq_in + N_META
    bs, n_new, _ = x_sample.shape
    assert seq % SUBLANES == 0 and (seq % WINDOW) % 16 == 0 and n_new >= CONV_W - 1

    vec = lambda a: a.reshape(DEPTH, 1, -1)
    w_in_b = w_in.astype(BF16)
    w_out_b = w_out.astype(BF16)
    w_gate_b = w_gate.astype(BF16)
    w_up_b = w_up.astype(BF16)
    w_down_b = w_down.astype(BF16)
    wab = jnp.concatenate([_block_diag(w_a), _block_diag(w_i)], axis=2).astype(BF16)
    bab = vec(jnp.concatenate([b_a, b_i], axis=1))
    g_pre, g_att, g_lru = vec(pre_mix_norm), vec(attn_out_norm), vec(lru_out_norm)
    g_pm, g_pf, g_po = vec(post_mix_norm), vec(pre_ffn_norm), vec(post_ffn_norm)
    cb, lam_v = vec(conv_b), vec(lam)

    assert DEPTH >= 2
    xp = x_prompt.reshape(bp * seq_in, D_MODEL)
    meta = meta_tokens.astype(x_prompt.dtype)
    xs = x_sample.transpose(1, 0, 2).reshape(n_new * bs, D_MODEL)

    cos_p, sin_p = _rope_tables(jnp.arange(seq, dtype=jnp.int32))
    pos_s = PAST_LEN + jnp.arange(n_new, dtype=jnp.int32)
    cos_s, sin_s = _rope_tables(jnp.repeat(pos_s, bs))

    tm_p = _pick_tile(seq, (688, 512, 256, 128, 16))
    tps = seq // tm_p
    tm_y = _pick_tile(seq_in, (512, 256, 128, 16))
    new_pad = SUBLANES
    group = 2 * LANES // new_pad
    assert n_new <= new_pad and bs % group == 0 and WINDOW == LANES

    ck_t = cache_k.transpose(0, 1, 3, 4, 2)
    cv_t = cache_v.transpose(0, 1, 3, 4, 2)
    conv_s = state_conv.transpose(0, 2, 1, 3)

    nk_all = jnp.zeros(ck_t.shape, F32)
    nv_all = jnp.zeros(cv_t.shape, F32)
    pk, pv, ph, pc, sh, sc = [], [], [], [], [], []
    for l in range(DEPTH):
        meta_l = meta if l == 0 else None
        q, k, v, h_seq, gate, xb_tail, ht = _inproj_lru(
            l, xp, meta_l, g_pre, w_in_b, cos_p, sin_p, conv_w, cb, wab, bab, lam_v, tm_p, tps, bp)
        k3 = k.reshape(bp, seq, KV_W)
        v3 = v.reshape(bp, seq, KV_W)
        att = _attn_prompt(l, sinks, q.reshape(bp, seq, ATTN_W), k3, v3, tm_p)
        ffn_tiling = (tm_p, tps, bp, 0) if l < DEPTH - 1 else (tm_y, seq_in // tm_y, bp, N_META)
        xp = _ffn(l, xp, meta_l, att.reshape(bp * seq, ATTN_W), h_seq, gate,
                  g_att, g_lru, w_out_b, g_pm, g_pf, w_gate_b, w_up_b, w_down_b, g_po, *ffn_tiling)
        pk.append(k3[:, -WINDOW:].reshape(bp, WINDOW, N_KV_HEADS, HEAD_DIM))
        pv.append(v3[:, -WINDOW:].reshape(bp, WINDOW, N_KV_HEADS, HEAD_DIM))
        ph.append(ht.reshape(bp, LRU_W))
        pc.append(xb_tail[:, -(CONV_W - 1):])

        q, k, v, xb, gate, h_slabs, ht = _inproj_sample(
            l, xs, g_pre, w_in_b, cos_s, sin_s, conv_s, state_h, conv_w, cb, wab, bab, lam_v, n_new)
        q4 = q.reshape(n_new, bs, N_KV_HEADS, PAIRS_PER_KV, LANES).transpose(1, 2, 3, 0, 4).reshape(
            bs, N_KV_HEADS, PAIRS_PER_KV * n_new, LANES)
        seq_major = lambda a: a.reshape(n_new, bs, -1).transpose(1, 0, 2)
        padn = ((0, 0), (0, new_pad - n_new), (0, 0))
        o4, nk_all, nv_all = _attn_sample(l, sinks, q4, jnp.pad(seq_major(k), padn),
                                          jnp.pad(seq_major(v), padn), ck_t, cv_t,
                                          nk_all, nv_all, group, n_new)
        att_raw = o4.reshape(bs, N_KV_HEADS, PAIRS_PER_KV, n_new, LANES).transpose(
            3, 0, 1, 2, 4).reshape(n_new * bs, ATTN_W)
        xs = _ffn(l, xs, None, att_raw, h_slabs, gate,
                  g_att, g_lru, w_out_b, g_pm, g_pf, w_gate_b, w_up_b, w_down_b, g_po,
                  bs * n_new, 1, 1, 0)
        sh.append(ht)
        sc.append(seq_major(xb)[:, -(CONV_W - 1):])

    y_prompt = xp.reshape(bp, seq_in, D_MODEL)
    y_sample = xs.reshape(n_new, bs, D_MODEL).transpose(1, 0, 2)
    sample_k = nk_all.transpose(0, 1, 4, 2, 3)
    sample_v = nv_all.transpose(0, 1, 4, 2, 3)
    return (y_prompt, y_sample, jnp.stack(pk), jnp.stack(pv), jnp.stack(ph), jnp.stack(pc),
            sample_k, sample_v, jnp.stack(sh), jnp.stack(sc))
```

```python
import functools

import jax
import jax.numpy as jnp
from jax import lax
from jax.experimental import pallas as pl
from jax.experimental.pallas import tpu as pltpu

D_MODEL = 1024
DEPTH = 4
PAST_LEN = 8192
N_META = 16
HEAD_DIM = 64
N_Q_HEADS = 8
N_KV_HEADS = 2
ATTN_W = N_Q_HEADS * HEAD_DIM
KV_W = N_KV_HEADS * HEAD_DIM
LRU_W = D_MODEL - ATTN_W
LRU_BLOCKS = 8
LRU_BLOCK_W = LRU_W // LRU_BLOCKS
CONV_W = 4
LRU_C = 8.0
WINDOW = 128
ROPE_THETA = 10000.0
D_FF = 2816
IN_W = ATTN_W + 2 * KV_W + 2 * LRU_W
EPS = 1e-6

LANES = 128
SUBLANES = 8
N_SLABS = LRU_W // LANES
N_PAIRS = ATTN_W // LANES
PAIRS_PER_KV = N_PAIRS // N_KV_HEADS
VMEM_LIMIT = 56 * 1024 * 1024

BF16 = jnp.bfloat16
F32 = jnp.float32


def _rms(x, g):
    ms = jnp.mean(x * x, axis=-1, keepdims=True)
    return x * lax.rsqrt(ms + EPS) * g


def _dot(a, b):
    return jnp.dot(a, b, preferred_element_type=F32)


def _dot_nt(a, b):
    return lax.dot_general(a, b, (((1,), (1,)), ((), ())), preferred_element_type=F32)


def _const_spec(shape):
    zeros = (0,) * len(shape)
    return pl.BlockSpec(shape, lambda *_: zeros)


def _layer_spec(arr, layer, single_buffer=False):
    tail = (0,) * (arr.ndim - 1)
    mode = dict(pipeline_mode=pl.Buffered(1)) if single_buffer else {}
    return pl.BlockSpec((None,) + arr.shape[1:], lambda *_: (layer,) + tail, **mode)


def _params(n_axes=1):
    return pltpu.CompilerParams(dimension_semantics=("arbitrary",) * n_axes,
                                vmem_limit_bytes=VMEM_LIMIT)


XB_COL = ATTN_W + 2 * KV_W
GATE_COL = XB_COL + LRU_W


def _store_qkv(z, cos_ref, sin_ref, q_ref, k_ref, v_ref):
    cos = cos_ref[...]
    sin = sin_ref[...]
    lane = lax.broadcasted_iota(jnp.int32, cos.shape, 1)
    first_half = (lane & (HEAD_DIM // 2)) == 0

    def rope(t):
        swapped = jnp.where(first_half,
                            pltpu.roll(t, LANES - HEAD_DIM // 2, 1),
                            pltpu.roll(t, HEAD_DIM // 2, 1))
        return t * cos + swapped * sin

    for p in range(N_PAIRS):
        qp = rope(z[:, p * LANES:(p + 1) * LANES]) * (HEAD_DIM ** -0.5)
        q_ref[:, p * LANES:(p + 1) * LANES] = qp.astype(q_ref.dtype)
    k_ref[...] = rope(z[:, ATTN_W:ATTN_W + KV_W])
    v_ref[...] = z[:, ATTN_W + KV_W:XB_COL]


def _inproj_sample_kernel(x_ref, g_ref, w_ref, cos_ref, sin_ref, conv0_ref, h0_ref, cw_ref, cb_ref,
                          wab_ref, bab_ref, lam_ref,
                          q_ref, kn_ref, vn_ref, xb_ref, gate_ref, h_out_ref, ht_ref,
                          k_ref, v_ref, *, n_new):
    n_seq = h0_ref.shape[0]
    new_pad = kn_ref.shape[0] // n_seq
    hn = _rms(x_ref[...], g_ref[...]).astype(BF16)
    z = _dot(hn, w_ref[...])
    _store_qkv(z[:, :XB_COL], cos_ref, sin_ref, q_ref, k_ref, v_ref)
    kn_ref[...] = jnp.zeros(kn_ref.shape, F32)
    vn_ref[...] = jnp.zeros(vn_ref.shape, F32)
    for t in range(n_new):
        kn_ref[pl.ds(t, n_seq, stride=new_pad), :] = k_ref[t * n_seq:(t + 1) * n_seq, :]
        vn_ref[pl.ds(t, n_seq, stride=new_pad), :] = v_ref[t * n_seq:(t + 1) * n_seq, :]
    xb = z[:, XB_COL:GATE_COL]
    xb_ref[...] = xb
    gate_ref[...] = z[:, GATE_COL:]

    step = lambda a, t: a[t * n_seq:(t + 1) * n_seq, :]
    xs = [conv0_ref[j] for j in range(CONV_W - 1)] + [step(xb, t) for t in range(n_new)]
    us = []
    for t in range(n_new):
        u = cb_ref[...]
        for j in range(CONV_W):
            u = u + xs[t + j] * cw_ref[j:j + 1, :]
        us.append(u)
    a, b = _lru_gates(jnp.concatenate(us, axis=0), wab_ref[...], bab_ref[...],
                      jax.nn.log_sigmoid(lam_ref[...]))
    h = h0_ref[...]
    for t in range(n_new):
        h = step(a, t) * h + step(b, t)
        for s in range(N_SLABS):
            h_out_ref[s, t * n_seq:(t + 1) * n_seq, :] = h[:, s * LANES:(s + 1) * LANES]
    ht_ref[...] = h


def _inproj_sample(layer, x, g, w, cos, sin, conv0, h0, cw, cb, wab, bab, lam, n_new, new_pad):
    t = x.shape[0]
    n_seq = t // n_new
    kern = functools.partial(_inproj_sample_kernel, n_new=n_new)
    lyr = lambda a: _layer_spec(a, layer)
    whole = lambda *shape: _const_spec(shape)
    return pl.pallas_call(
        kern,
        grid=(1,),
        in_specs=[whole(t, D_MODEL), lyr(g), lyr(w), whole(t, LANES), whole(t, LANES),
                  lyr(conv0), lyr(h0), lyr(cw), lyr(cb), lyr(wab), lyr(bab), lyr(lam)],
        out_specs=[whole(t, ATTN_W), whole(n_seq * new_pad, KV_W), whole(n_seq * new_pad, KV_W),
                   whole(t, LRU_W), whole(t, LRU_W), whole(N_SLABS, t, LANES),
                   whole(n_seq, LRU_W)],
        out_shape=[jax.ShapeDtypeStruct((t, ATTN_W), F32),
                   jax.ShapeDtypeStruct((n_seq * new_pad, KV_W), F32),
                   jax.ShapeDtypeStruct((n_seq * new_pad, KV_W), F32),
                   jax.ShapeDtypeStruct((t, LRU_W), F32),
                   jax.ShapeDtypeStruct((t, LRU_W), F32),
                   jax.ShapeDtypeStruct((N_SLABS, t, LANES), F32),
                   jax.ShapeDtypeStruct((n_seq, LRU_W), F32)],
        scratch_shapes=[pltpu.VMEM((t, KV_W), F32), pltpu.VMEM((t, KV_W), F32)],
        compiler_params=_params(),
        name="inproj_sample",
    )(x, g, w, cos, sin, conv0, h0, cw, cb, wab, bab, lam)


MXU_COLS = 256
FF_SPLIT = (D_FF // MXU_COLS + 1) // 2 * MXU_COLS


def _gelu_tanh(x):
    return 0.5 * x * (1.0 + jnp.tanh(0.7978845608028654 * (x + 0.044715 * (x * x * x))))


def _mixer_out(x, att_ref, h_ref, gate_ref, ga_ref, gl_ref, wo_ref, gpm_ref, gpf_ref):
    att = _rms(att_ref[...], ga_ref[...]).astype(BF16)
    h = jnp.concatenate([h_ref[s] for s in range(N_SLABS)], axis=1)
    lru = _rms(h * _gelu_tanh(gate_ref[...]), gl_ref[...]).astype(BF16)
    m = _dot(att, wo_ref[0:ATTN_W, :]) + _dot(lru, wo_ref[ATTN_W:, :])
    x1 = x + _rms(m, gpm_ref[...])
    return x1, _rms(x1, gpf_ref[...]).astype(BF16)


def _swiglu(hf, wg_ref, wu_ref, wd_ref):
    f = None
    for lo, hi in ((0, FF_SPLIT), (FF_SPLIT, D_FF)):
        gt = _dot(hf, wg_ref[:, lo:hi])
        up = _dot(hf, wu_ref[:, lo:hi])
        hid = (gt * jax.nn.sigmoid(gt) * up).astype(BF16)
        part = _dot(hid, wd_ref[lo:hi, :])
        f = part if f is None else f + part
    return f


def _ffn_kernel(*refs, tiles_per_seq, meta_front):
    x_ref, refs = refs[0], refs[1:]
    meta_ref, refs = (refs[0], refs[1:]) if meta_front else (None, refs)
    (att_ref, h_ref, gate_ref, ga_ref, gl_ref, wo_ref, gpm_ref, gpf_ref,
     wg_ref, wu_ref, wd_ref, gpo_ref, o_ref) = refs
    x = _meta_front_tile(x_ref, meta_ref, tiles_per_seq) if meta_front else x_ref[...]
    x1, hf = _mixer_out(x, att_ref, h_ref, gate_ref, ga_ref, gl_ref, wo_ref, gpm_ref, gpf_ref)
    o_ref[...] = x1 + _rms(_swiglu(hf, wg_ref, wu_ref, wd_ref), gpo_ref[...])


def _ffn(layer, x, meta, att, h, gate, ga, gl, wo, gpm, gpf, wg, wu, wd, gpo,
         tm, tiles_per_seq, n_seq, drop_meta):
    keep = tiles_per_seq * tm
    seq = keep + drop_meta
    meta_front = meta is not None
    assert not (meta_front and drop_meta)
    r = SUBLANES
    assert tm % r == 0 and drop_meta % r == 0
    in_row = lambda i: r * ((i // tiles_per_seq) * (seq // r) + drop_meta // r
                            + (i % tiles_per_seq) * (tm // r))

    def rows(width):
        if drop_meta:
            return pl.BlockSpec((pl.Element(tm), pl.Element(width)), lambda i: (in_row(i), 0))
        return pl.BlockSpec((tm, width), lambda i: (i, 0))

    if drop_meta:
        h_spec = pl.BlockSpec((pl.Element(N_SLABS), pl.Element(tm), pl.Element(LANES)),
                              lambda i: (0, in_row(i), 0))
    else:
        h_spec = pl.BlockSpec((N_SLABS, tm, LANES), lambda i: (0, i, 0))
    out_spec = pl.BlockSpec((tm, D_MODEL), lambda i: (i, 0))
    if meta_front:
        x_specs = [_meta_front_spec(tm, tiles_per_seq, x.shape[0] // n_seq, meta.shape[0]),
                   _const_spec(meta.shape)]
        x_args = (x, meta)
    else:
        x_specs, x_args = [rows(D_MODEL)], (x,)
    big = lambda a: _layer_spec(a, layer, single_buffer=True)
    lyr = lambda a: _layer_spec(a, layer)
    return pl.pallas_call(
        functools.partial(_ffn_kernel, tiles_per_seq=tiles_per_seq, meta_front=meta_front),
        grid=(n_seq * tiles_per_seq,),
        in_specs=x_specs + [rows(ATTN_W), h_spec, rows(LRU_W), lyr(ga), lyr(gl),
                            big(wo), lyr(gpm), lyr(gpf), big(wg), big(wu), big(wd), lyr(gpo)],
        out_specs=out_spec,
        out_shape=jax.ShapeDtypeStruct((n_seq * keep, D_MODEL), F32),
        compiler_params=_params(),
        name="outffn",
    )(*x_args, att, h, gate, ga, gl, wo, gpm, gpf, wg, wu, wd, gpo)


def _kv_variants(k, v):
    lane = lax.broadcasted_iota(jnp.int32, k.shape, 1)
    low = lane < HEAD_DIM
    kr = pltpu.roll(k, HEAD_DIM, 1)
    vr = pltpu.roll(v, HEAD_DIM, 1)
    zero = jnp.zeros_like(k)
    ka = (jnp.where(low, k, zero), jnp.where(low, kr, zero))
    kb = (jnp.where(low, zero, kr), jnp.where(low, zero, k))
    vv = (jnp.where(low, v, vr), jnp.where(low, vr, v))
    cast = lambda pair: tuple(a.astype(BF16) for a in pair)
    return cast(ka), cast(kb), cast(vv)


def _attn_prompt_kernel(sink_ref, q_ref, k_ref, v_ref, o_ref,
                        ka_ref, kb_ref, vv_ref, *, layer, seq, chunk):
    for c in range(seq // chunk):
        rows = slice(c * chunk, (c + 1) * chunk)
        ka, kb, vv = _kv_variants(k_ref[0, rows, :], v_ref[0, rows, :])
        ones = jnp.ones((chunk, LANES), BF16)
        for h in range(N_KV_HEADS):
            ka_ref[h, rows, :] = ka[h]
            kb_ref[h, rows, :] = kb[h]
            vv_ref[h, rows, :] = jnp.concatenate([vv[h], ones], axis=1)

    def tile(q0, ks, nq, nk, sink_col):
        row = lax.broadcasted_iota(jnp.int32, (nq, nk), 0)
        col = lax.broadcasted_iota(jnp.int32, (nq, nk), 1)
        rel = row - col + (q0 - ks)
        mask = (rel >= 0) & (rel < WINDOW)
        col_row = lax.broadcasted_iota(jnp.int32, (1, nk), 1)
        key_row = lax.broadcasted_iota(jnp.int32, (nk, 2 * LANES), 0)
        val_lane = lax.broadcasted_iota(jnp.int32, (nk, 2 * LANES), 1)
        zero_v = (key_row == sink_col) & (val_lane < LANES)
        vvw = [jnp.where(zero_v, jnp.zeros((), BF16), vv_ref[h, pl.ds(ks, nk), :])
               for h in range(N_KV_HEADS)]
        low = lax.broadcasted_iota(jnp.int32, (nq, LANES), 1) < HEAD_DIM
        for p in range(N_PAIRS):
            h = p // PAIRS_PER_KV
            qp = q_ref[0, pl.ds(q0, nq), p * LANES:(p + 1) * LANES]
            halves = []
            for k_ref_, head in ((ka_ref, 2 * p), (kb_ref, 2 * p + 1)):
                s = _dot_nt(qp, k_ref_[h, pl.ds(ks, nk), :])
                fill = jnp.where(col_row == sink_col, sink_ref[layer, head], -jnp.inf)
                s = jnp.where(mask, s, fill)
                e = jnp.exp(s - jnp.max(s, axis=1, keepdims=True))
                pv = _dot(e.astype(BF16), vvw[h])
                halves.append(pv[:, :LANES] / pv[:, LANES:])
            o_ref[0, pl.ds(q0, nq), p * LANES:(p + 1) * LANES] = jnp.where(low, halves[0], halves[1])

    n_full = seq // WINDOW
    group = 8

    def body(it, carry):
        for t in range(group):
            i = it * group + t
            q0 = pl.multiple_of(i * WINDOW, WINDOW)
            ks = pl.multiple_of(jnp.maximum(q0 - WINDOW, 0), WINDOW)
            sink_col = jnp.where(i == 0, 2 * WINDOW - 1, 0) if t == 0 else 0
            tile(q0, ks, WINDOW, 2 * WINDOW, sink_col)
        return carry

    lax.fori_loop(0, n_full // group, body, 0)
    for i in range(n_full // group * group, n_full):
        tile(i * WINDOW, max(i - 1, 0) * WINDOW, WINDOW, 2 * WINDOW, 2 * WINDOW - 1 if i == 0 else 0)
    if seq > n_full * WINDOW:
        tile(seq - WINDOW, seq - 2 * WINDOW, WINDOW, 2 * WINDOW, 0)


def _attn_prompt(layer, sinks, q, k, v, chunk):
    b, seq, _ = q.shape
    assert seq >= 2 * WINDOW
    kern = functools.partial(_attn_prompt_kernel, layer=layer, seq=seq, chunk=chunk)
    seq_spec = lambda w_: pl.BlockSpec((1, seq, w_), lambda i: (i, 0, 0))
    return pl.pallas_call(
        kern,
        grid=(b,),
        in_specs=[pl.BlockSpec(memory_space=pltpu.SMEM),
                  seq_spec(ATTN_W), seq_spec(KV_W), seq_spec(KV_W)],
        out_specs=seq_spec(ATTN_W),
        out_shape=jax.ShapeDtypeStruct((b, seq, ATTN_W), F32),
        scratch_shapes=[pltpu.VMEM((N_KV_HEADS, seq, LANES), BF16),
                        pltpu.VMEM((N_KV_HEADS, seq, LANES), BF16),
                        pltpu.VMEM((N_KV_HEADS, seq, 2 * LANES), BF16)],
        compiler_params=_params(),
        name="attn_prompt",
    )(sinks, q, k, v)


def _lru_gates(u, wab, bab, logsig):
    ga = _dot(u.astype(BF16), wab) + bab
    r = jax.nn.sigmoid(ga[:, :LRU_W])
    ig = jax.nn.sigmoid(ga[:, LRU_W:])
    log_a = LRU_C * r * logsig
    a = jnp.exp(log_a)
    b = jnp.sqrt(-jnp.tanh(log_a) * (a * a + 1.0)) * ig * u
    return a, b


def _scan_interleaved(a_ref, b_ref, h0, rows):
    piece = rows // SUBLANES
    at = lambda i: slice(i * SUBLANES, (i + 1) * SUBLANES)
    hs = [jnp.zeros((SUBLANES, LANES), F32) for _ in range(N_SLABS)]
    ps = [jnp.ones((SUBLANES, LANES), F32) for _ in range(N_SLABS)]
    for i in range(piece):
        for s in range(N_SLABS):
            a = a_ref[s, at(i), :]
            hs[s] = a * hs[s] + b_ref[s, at(i), :]
            ps[s] = a * ps[s]
            b_ref[s, at(i), :] = hs[s]
            a_ref[s, at(i), :] = ps[s]

    sub = lax.broadcasted_iota(jnp.int32, (SUBLANES, LANES), 0)
    carry_in, h_last = [], []
    for s in range(N_SLABS):
        cin = jnp.broadcast_to(h0[:, s * LANES:(s + 1) * LANES], (SUBLANES, LANES))
        acc = jnp.zeros((SUBLANES, LANES), F32)
        for c in range(SUBLANES):
            acc = jnp.where(sub == c, cin, acc)
            end = hs[s] + ps[s] * cin
            cin = jnp.broadcast_to(end[c:c + 1, :], (SUBLANES, LANES))
        carry_in.append(acc)
        h_last.append(cin[0:1, :])

    for i in range(piece):
        for s in range(N_SLABS):
            b_ref[s, at(i), :] = b_ref[s, at(i), :] + a_ref[s, at(i), :] * carry_in[s]
    return jnp.concatenate(h_last, axis=1)


def _meta_front_tile(x_ref, meta_ref, tiles_per_seq):
    blk = x_ref[...]
    n_meta = meta_ref.shape[0]
    with_meta = jnp.concatenate([meta_ref[...], blk[:blk.shape[0] - n_meta]], axis=0)
    return jnp.where(pl.program_id(0) % tiles_per_seq == 0, with_meta, blk)


def _inproj_lru_kernel(*refs, tm, tiles_per_seq, meta_front):
    x_ref, refs = refs[0], refs[1:]
    meta_ref, refs = (refs[0], refs[1:]) if meta_front else (None, refs)
    (g_ref, w_ref, cos_ref, sin_ref, cw_ref, cb_ref, wab_ref, bab_ref, lam_ref,
     q_ref, k_ref, v_ref, h_out_ref, gate_ref, tail_ref, ht_ref,
     xs_ref, u_ref, a_ref, b_ref, h_ref) = refs
    pad = SUBLANES
    piece = tm // SUBLANES
    slab = lambda s: slice(s * LANES, (s + 1) * LANES)

    @pl.when(pl.program_id(0) % tiles_per_seq == 0)
    def _():
        xs_ref[:, 0:pad, :] = jnp.zeros((N_SLABS, pad, LANES), F32)
        h_ref[...] = jnp.zeros((1, LRU_W), F32)

    x = _meta_front_tile(x_ref, meta_ref, tiles_per_seq) if meta_front else x_ref[...]
    hn = _rms(x, g_ref[...]).astype(BF16)
    xb = _dot(hn, w_ref[:, XB_COL:GATE_COL])
    gate_ref[...] = _dot(hn, w_ref[:, GATE_COL:])
    tail_ref[0] = xb[tm - pad:, :]
    for s in range(N_SLABS):
        xs_ref[s, pad:pad + tm, :] = xb[:, slab(s)]

    for s in range(N_SLABS):
        bias = jnp.broadcast_to(cb_ref[:, slab(s)], (SUBLANES, LANES))
        wts = [jnp.broadcast_to(cw_ref[j:j + 1, slab(s)], (SUBLANES, LANES))
               for j in range(CONV_W)]
        taps = [xs_ref[s, pl.ds(pad - (CONV_W - 1) + k, SUBLANES, stride=piece), :]
                for k in range(CONV_W - 1)]
        for i in range(piece):
            taps.append(xs_ref[s, pl.ds(pad + i, SUBLANES, stride=piece), :])
            u = bias
            for j in range(CONV_W):
                u = u + taps[j] * wts[j]
            u_ref[i * SUBLANES:(i + 1) * SUBLANES, slab(s)] = u
            taps.pop(0)
    for s in range(N_SLABS):
        xs_ref[s, 0:pad, :] = xs_ref[s, tm:tm + pad, :]

    a, b = _lru_gates(u_ref[...], wab_ref[...], bab_ref[...], jax.nn.log_sigmoid(lam_ref[...]))
    for s in range(N_SLABS):
        a_ref[s] = a[:, slab(s)]
        b_ref[s] = b[:, slab(s)]
    h_last = _scan_interleaved(a_ref, b_ref, h_ref[...], tm)
    h_ref[...] = h_last
    ht_ref[0] = h_last
    for s in range(N_SLABS):
        for i in range(piece):
            h_out_ref[s, pl.ds(i, SUBLANES, stride=piece), :] = (
                b_ref[s, i * SUBLANES:(i + 1) * SUBLANES, :])

    _store_qkv(_dot(hn, w_ref[:, :XB_COL]), cos_ref, sin_ref, q_ref, k_ref, v_ref)


def _meta_front_spec(tm, tiles_per_seq, seq_in, n_meta):
    r = SUBLANES
    assert seq_in % r == 0 and tm % r == 0 and n_meta % r == 0

    def start(i):
        j = i % tiles_per_seq
        return r * ((i // tiles_per_seq) * (seq_in // r)
                    + jnp.maximum(j * (tm // r) - n_meta // r, 0)), 0
    return pl.BlockSpec((pl.Element(tm), pl.Element(D_MODEL)), start)


def _inproj_lru(layer, x, meta, g, w, cos, sin, cw, cb, wab, bab, lam, tm, tiles_per_seq, n_seq):
    t = n_seq * tiles_per_seq * tm
    assert tm % SUBLANES == 0 and tm >= SUBLANES
    meta_front = meta is not None
    kern = functools.partial(_inproj_lru_kernel, tm=tm, tiles_per_seq=tiles_per_seq,
                             meta_front=meta_front)
    tok = lambda w_: pl.BlockSpec((tm, w_), lambda i: (i, 0))
    tab = pl.BlockSpec((tm, LANES), lambda i: (i % tiles_per_seq, 0))
    per_seq = lambda r: pl.BlockSpec((1, r, LRU_W), lambda i: (i // tiles_per_seq, 0, 0))
    lyr = lambda a: _layer_spec(a, layer)
    if meta_front:
        x_specs = [_meta_front_spec(tm, tiles_per_seq, x.shape[0] // n_seq, meta.shape[0]),
                   _const_spec(meta.shape)]
        x_args = (x, meta)
    else:
        x_specs, x_args = [tok(D_MODEL)], (x,)
    return pl.pallas_call(
        kern,
        grid=(t // tm,),
        in_specs=x_specs + [lyr(g), lyr(w), tab, tab,
                            lyr(cw), lyr(cb), lyr(wab), lyr(bab), lyr(lam)],
        out_specs=[tok(ATTN_W), tok(KV_W), tok(KV_W),
                   pl.BlockSpec((N_SLABS, tm, LANES), lambda i: (0, i, 0)), tok(LRU_W),
                   per_seq(SUBLANES), per_seq(1)],
        out_shape=[jax.ShapeDtypeStruct((t, ATTN_W), BF16),
                   jax.ShapeDtypeStruct((t, KV_W), F32),
                   jax.ShapeDtypeStruct((t, KV_W), F32),
                   jax.ShapeDtypeStruct((N_SLABS, t, LANES), F32),
                   jax.ShapeDtypeStruct((t, LRU_W), F32),
                   jax.ShapeDtypeStruct((n_seq, SUBLANES, LRU_W), F32),
                   jax.ShapeDtypeStruct((n_seq, 1, LRU_W), F32)],
        scratch_shapes=[pltpu.VMEM((N_SLABS, tm + SUBLANES, LANES), F32),
                        pltpu.VMEM((tm, LRU_W), F32),
                        pltpu.VMEM((N_SLABS, tm, LANES), F32),
                        pltpu.VMEM((N_SLABS, tm, LANES), F32),
                        pltpu.VMEM((1, LRU_W), F32)],
        compiler_params=_params(),
        name="inproj_lru",
    )(*x_args, g, w, cos, sin, cw, cb, wab, bab, lam)


def _div_pow2(x, n):
    assert n & (n - 1) == 0
    return x >> (n.bit_length() - 1)


def _mod_pow2(x, n):
    assert n & (n - 1) == 0
    return x & (n - 1)


def _attn_sample_kernel(sink_ref, q_ref, kn_ref, vn_ref, ck_ref, cv_ref, nk_all_ref, nv_all_ref,
                        o_ref, nk_ref, nv_ref, *, layer, group, n_new):
    del nk_all_ref, nv_all_ref
    rpb = PAIRS_PER_KV * n_new
    new_pad = kn_ref.shape[1]
    rows = group * rpb
    ncols = group * new_pad
    row = lax.broadcasted_iota(jnp.int32, (rows, 1), 0)
    qpos = _mod_pow2(row, n_new)
    row_seq = _div_pow2(row, rpb)
    second_pair = _mod_pow2(_div_pow2(row, n_new), PAIRS_PER_KV) == 1
    col = lax.broadcasted_iota(jnp.int32, (1, WINDOW), 1)
    mask_c = col > qpos
    if PAST_LEN < WINDOW:
        mask_c = mask_c & (col >= WINDOW - PAST_LEN)
    ncol = lax.broadcasted_iota(jnp.int32, (1, ncols), 1)
    mask_n = (_div_pow2(ncol, new_pad) == row_seq) & (_mod_pow2(ncol, new_pad) <= qpos)
    low = lax.broadcasted_iota(jnp.int32, (rpb, LANES), 1) < HEAD_DIM

    kn_all = kn_ref[...].reshape(ncols, KV_W)
    vn_all = vn_ref[...].reshape(ncols, KV_W)
    knt_all = kn_all.T
    vnt_all = vn_all.T
    vn_lane = lax.broadcasted_iota(jnp.int32, vn_all.shape, 1) < HEAD_DIM
    vn_roll = pltpu.roll(vn_all, HEAD_DIM, 1)

    keep = WINDOW - n_new
    cache_lane = lax.broadcasted_iota(jnp.int32, (HEAD_DIM, WINDOW), 1) < keep

    for h in range(N_KV_HEADS):
        hd = slice(h * HEAD_DIM, (h + 1) * HEAD_DIM)
        q_all = q_ref[:, h].reshape(rows, LANES).astype(BF16)
        q_seq = [q_ref[b, h].astype(BF16) for b in range(group)]
        knt_h = knt_all[hd, :].astype(BF16)
        vvn = (jnp.where(vn_lane, vn_all, vn_roll) if h == 0
               else jnp.where(vn_lane, vn_roll, vn_all)).astype(BF16)
        kt, vt2 = [], []
        for b in range(group):
            kt.append(ck_ref[b, h].astype(BF16))
            vt = cv_ref[b, h].astype(BF16)
            vt2.append(jnp.concatenate([vt, vt], axis=0))
        probs, dens = [], []
        for first in (True, False):
            embed = lambda t: jnp.concatenate(
                [t, jnp.zeros_like(t)] if first else [jnp.zeros_like(t), t], axis=0)
            s_new = jnp.where(mask_n, _dot(q_all, embed(knt_h)), -jnp.inf)
            s_c = jnp.concatenate(
                [_dot(q_seq[b], embed(kt[b])) for b in range(group)], axis=0)
            s_c = jnp.where(mask_c, s_c, -jnp.inf)
            pair0 = 2 * h * PAIRS_PER_KV + (0 if first else 1)
            sink = jnp.where(second_pair, sink_ref[layer,pair0 + 2], sink_ref[layer,pair0])
            mx = jnp.maximum(jnp.maximum(jnp.max(s_c, axis=1, keepdims=True),
                                         jnp.max(s_new, axis=1, keepdims=True)), sink)
            p_c = jnp.exp(s_c - mx)
            p_n = jnp.exp(s_new - mx)
            dens.append(jnp.sum(p_c, axis=1, keepdims=True) + jnp.sum(p_n, axis=1, keepdims=True)
                        + jnp.exp(sink - mx))
            probs.append((p_c, _dot(p_n.astype(BF16), vvn)))
        for b in range(group):
            rs = slice(b * rpb, (b + 1) * rpb)
            pc = jnp.concatenate([probs[0][0][rs], probs[1][0][rs]], axis=0).astype(BF16)
            oc = _dot_nt(pc, vt2[b])
            o_first = (oc[:rpb] + probs[0][1][rs]) / dens[0][rs]
            o_second = (oc[rpb:] + probs[1][1][rs]) / dens[1][rs]
            o_ref[b, h] = jnp.where(low, o_first, o_second)
            shift = (keep - b * new_pad) % ncols
            nk_ref[b, h] = jnp.where(cache_lane, pltpu.roll(ck_ref[b, h], keep, 1),
                                     pltpu.roll(knt_all[hd, :], shift, 1)[:, :WINDOW])
            nv_ref[b, h] = jnp.where(cache_lane, pltpu.roll(cv_ref[b, h], keep, 1),
                                     pltpu.roll(vnt_all[hd, :], shift, 1)[:, :WINDOW])


def _attn_sample(layer, sinks, q, kn, vn, ck, cv, nk_all, nv_all, group, n_new):
    bs = q.shape[0]
    kern = functools.partial(_attn_sample_kernel, layer=layer, group=group, n_new=n_new)
    seq4 = lambda a: pl.BlockSpec((group,) + a.shape[1:], lambda i: (i, 0, 0, 0))
    seq3 = lambda a: pl.BlockSpec((group,) + a.shape[1:], lambda i: (i, 0, 0))
    cache = pl.BlockSpec((None, group) + ck.shape[2:], lambda i: (layer, i, 0, 0, 0))
    in_hbm = pl.BlockSpec(memory_space=pl.ANY)
    return pl.pallas_call(
        kern,
        grid=(bs // group,),
        in_specs=[pl.BlockSpec(memory_space=pltpu.SMEM),
                  seq4(q), seq3(kn), seq3(vn), cache, cache, in_hbm, in_hbm],
        out_specs=[seq4(q), cache, cache],
        out_shape=[jax.ShapeDtypeStruct(q.shape, F32),
                   jax.ShapeDtypeStruct(nk_all.shape, F32),
                   jax.ShapeDtypeStruct(nv_all.shape, F32)],
        input_output_aliases={6: 1, 7: 2},
        compiler_params=_params(),
        name="attn_sample",
    )(sinks, q, kn, vn, ck, cv, nk_all, nv_all)


def _rope_tables(pos):
    half = HEAD_DIM // 2
    inv = ROPE_THETA ** (-jnp.arange(half, dtype=F32) / half)
    ang = pos.astype(F32)[:, None] * inv[None, :]
    cos, sin = jnp.cos(ang), jnp.sin(ang)
    reps = LANES // HEAD_DIM
    return (jnp.concatenate([cos, cos] * reps, axis=1),
            jnp.concatenate([-sin, sin] * reps, axis=1))


def _block_diag(w):
    rows = [jnp.pad(w[:, h], ((0, 0), (0, 0), (h * LRU_BLOCK_W, (LRU_BLOCKS - 1 - h) * LRU_BLOCK_W)))
            for h in range(LRU_BLOCKS)]
    return jnp.concatenate(rows, axis=1)


def _pick_tile(n, candidates):
    for c in candidates:
        if n % c == 0:
            return c
    raise ValueError(f"no tile for {n} tokens")


def kernel(x_prompt, x_sample, cache_k, cache_v, state_h, state_conv, meta_tokens, pre_mix_norm,
           w_in, sinks, conv_w, conv_b, w_a, b_a, w_i, b_i, lam, attn_out_norm, lru_out_norm,
           w_out, post_mix_norm, pre_ffn_norm, w_gate, w_up, w_down, post_ffn_norm):
    bp, seq_in, _ = x_prompt.shape
    seq = seq_in + N_META
    bs, n_new, _ = x_sample.shape
    assert seq % SUBLANES == 0 and (seq % WINDOW) % 16 == 0 and n_new >= CONV_W - 1

    vec = lambda a: a.reshape(DEPTH, 1, -1)
    w_in_b = w_in.astype(BF16)
    w_out_b = w_out.astype(BF16)
    w_gate_b = w_gate.astype(BF16)
    w_up_b = w_up.astype(BF16)
    w_down_b = w_down.astype(BF16)
    wab = jnp.concatenate([_block_diag(w_a), _block_diag(w_i)], axis=2).astype(BF16)
    bab = vec(jnp.concatenate([b_a, b_i], axis=1))
    g_pre, g_att, g_lru = vec(pre_mix_norm), vec(attn_out_norm), vec(lru_out_norm)
    g_pm, g_pf, g_po = vec(post_mix_norm), vec(pre_ffn_norm), vec(post_ffn_norm)
    cb, lam_v = vec(conv_b), vec(lam)

    assert DEPTH >= 2
    xp = x_prompt.reshape(bp * seq_in, D_MODEL)
    meta = meta_tokens.astype(x_prompt.dtype)
    xs = x_sample.transpose(1, 0, 2).reshape(n_new * bs, D_MODEL)

    cos_p, sin_p = _rope_tables(jnp.arange(seq, dtype=jnp.int32))
    pos_s = PAST_LEN + jnp.arange(n_new, dtype=jnp.int32)
    cos_s, sin_s = _rope_tables(jnp.repeat(pos_s, bs))

    tm_p = _pick_tile(seq, (688, 512, 256, 128, 16))
    tps = seq // tm_p
    tm_y = _pick_tile(seq_in, (512, 256, 128, 16))
    new_pad = SUBLANES
    group = 2 * LANES // new_pad
    assert n_new <= new_pad and bs % group == 0 and WINDOW == LANES

    ck_t = cache_k.transpose(0, 1, 3, 4, 2)
    cv_t = cache_v.transpose(0, 1, 3, 4, 2)
    conv_s = state_conv.transpose(0, 2, 1, 3)

    nk_all = jnp.zeros(ck_t.shape, F32)
    nv_all = jnp.zeros(cv_t.shape, F32)
    pk, pv, ph, pc, sh, sc = [], [], [], [], [], []
    for l in range(DEPTH):
        meta_l = meta if l == 0 else None
        q, k, v, h_seq, gate, xb_tail, ht = _inproj_lru(
            l, xp, meta_l, g_pre, w_in_b, cos_p, sin_p, conv_w, cb, wab, bab, lam_v, tm_p, tps, bp)
        k3 = k.reshape(bp, seq, KV_W)
        v3 = v.reshape(bp, seq, KV_W)
        att = _attn_prompt(l, sinks, q.reshape(bp, seq, ATTN_W), k3, v3, tm_p)
        ffn_tiling = (tm_p, tps, bp, 0) if l < DEPTH - 1 else (tm_y, seq_in // tm_y, bp, N_META)
        xp = _ffn(l, xp, meta_l, att.reshape(bp * seq, ATTN_W), h_seq, gate,
                  g_att, g_lru, w_out_b, g_pm, g_pf, w_gate_b, w_up_b, w_down_b, g_po, *ffn_tiling)
        pk.append(k3[:, -WINDOW:].reshape(bp, WINDOW, N_KV_HEADS, HEAD_DIM))
        pv.append(v3[:, -WINDOW:].reshape(bp, WINDOW, N_KV_HEADS, HEAD_DIM))
        ph.append(ht.reshape(bp, LRU_W))
        pc.append(xb_tail[:, -(CONV_W - 1):])

        q, kn, vn, xb, gate, h_slabs, ht = _inproj_sample(
            l, xs, g_pre, w_in_b, cos_s, sin_s, conv_s, state_h, conv_w, cb, wab, bab, lam_v,
            n_new, new_pad)
        q4 = q.reshape(n_new, bs, N_KV_HEADS, PAIRS_PER_KV, LANES).transpose(1, 2, 3, 0, 4).reshape(
            bs, N_KV_HEADS, PAIRS_PER_KV * n_new, LANES)
        seq_major = lambda a: a.reshape(n_new, bs, -1).transpose(1, 0, 2)
        o4, nk_all, nv_all = _attn_sample(l, sinks, q4, kn.reshape(bs, new_pad, KV_W),
                                          vn.reshape(bs, new_pad, KV_W), ck_t, cv_t,
                                          nk_all, nv_all, group, n_new)
        att_raw = o4.reshape(bs, N_KV_HEADS, PAIRS_PER_KV, n_new, LANES).transpose(
            3, 0, 1, 2, 4).reshape(n_new * bs, ATTN_W)
        xs = _ffn(l, xs, None, att_raw, h_slabs, gate,
                  g_att, g_lru, w_out_b, g_pm, g_pf, w_gate_b, w_up_b, w_down_b, g_po,
                  bs * n_new, 1, 1, 0)
        sh.append(ht)
        sc.append(seq_major(xb)[:, -(CONV_W - 1):])

    y_prompt = xp.reshape(bp, seq_in, D_MODEL)
    y_sample = xs.reshape(n_new, bs, D_MODEL).transpose(1, 0, 2)
    sample_k = nk_all.transpose(0, 1, 4, 2, 3)
    sample_v = nv_all.transpose(0, 1, 4, 2, 3)
    return (y_prompt, y_sample, jnp.stack(pk), jnp.stack(pv), jnp.stack(ph), jnp.stack(pc),
            sample_k, sample_v, jnp.stack(sh), jnp.stack(sc))
```

```python
import functools

import jax
import jax.numpy as jnp
from jax import lax
from jax.experimental import pallas as pl
from jax.experimental.pallas import tpu as pltpu

D_MODEL = 1024
DEPTH = 4
PAST_LEN = 8192
N_META = 16
HEAD_DIM = 64
N_Q_HEADS = 8
N_KV_HEADS = 2
ATTN_W = N_Q_HEADS * HEAD_DIM
KV_W = N_KV_HEADS * HEAD_DIM
LRU_W = D_MODEL - ATTN_W
LRU_BLOCKS = 8
LRU_BLOCK_W = LRU_W // LRU_BLOCKS
CONV_W = 4
LRU_C = 8.0
WINDOW = 128
ROPE_THETA = 10000.0
D_FF = 2816
IN_W = ATTN_W + 2 * KV_W + 2 * LRU_W
EPS = 1e-6

LANES = 128
SUBLANES = 8
N_SLABS = LRU_W // LANES
N_PAIRS = ATTN_W // LANES
PAIRS_PER_KV = N_PAIRS // N_KV_HEADS
VMEM_LIMIT = 56 * 1024 * 1024

BF16 = jnp.bfloat16
F32 = jnp.float32


def _rms(x, g):
    ms = jnp.mean(x * x, axis=-1, keepdims=True)
    return x * lax.rsqrt(ms + EPS) * g


def _dot(a, b):
    return jnp.dot(a, b, preferred_element_type=F32)


def _dot_nt(a, b):
    return lax.dot_general(a, b, (((1,), (1,)), ((), ())), preferred_element_type=F32)


def _const_spec(shape):
    zeros = (0,) * len(shape)
    return pl.BlockSpec(shape, lambda *_: zeros)


def _layer_spec(arr, layer, single_buffer=False):
    tail = (0,) * (arr.ndim - 1)
    mode = dict(pipeline_mode=pl.Buffered(1)) if single_buffer else {}
    return pl.BlockSpec((None,) + arr.shape[1:], lambda *_: (layer,) + tail, **mode)


def _params(n_axes=1):
    return pltpu.CompilerParams(dimension_semantics=("arbitrary",) * n_axes,
                                vmem_limit_bytes=VMEM_LIMIT)


XB_COL = ATTN_W + 2 * KV_W
GATE_COL = XB_COL + LRU_W


def _store_qkv(z, cos_ref, sin_ref, q_ref, k_ref, v_ref):
    cos = cos_ref[...]
    sin = sin_ref[...]
    lane = lax.broadcasted_iota(jnp.int32, cos.shape, 1)
    first_half = (lane & (HEAD_DIM // 2)) == 0

    def rope(t):
        swapped = jnp.where(first_half,
                            pltpu.roll(t, LANES - HEAD_DIM // 2, 1),
                            pltpu.roll(t, HEAD_DIM // 2, 1))
        return t * cos + swapped * sin

    for p in range(N_PAIRS):
        qp = rope(z[:, p * LANES:(p + 1) * LANES]) * (HEAD_DIM ** -0.5)
        q_ref[:, p * LANES:(p + 1) * LANES] = qp.astype(q_ref.dtype)
    k_ref[...] = rope(z[:, ATTN_W:ATTN_W + KV_W])
    v_ref[...] = z[:, ATTN_W + KV_W:XB_COL]


def _inproj_sample_kernel(x_ref, g_ref, w_ref, cos_ref, sin_ref, conv0_ref, h0_ref, cw_ref, cb_ref,
                          wab_ref, bab_ref, lam_ref,
                          q_ref, kn_ref, vn_ref, xb_ref, gate_ref, h_out_ref, ht_ref,
                          qs_ref, k_ref, v_ref, *, n_new):
    n_seq = h0_ref.shape[0]
    new_pad = kn_ref.shape[0] // n_seq
    hn = _rms(x_ref[...], g_ref[...]).astype(BF16)
    z = _dot(hn, w_ref[...])
    _store_qkv(z[:, :XB_COL], cos_ref, sin_ref, qs_ref, k_ref, v_ref)
    rpb = PAIRS_PER_KV * n_new
    for p in range(N_PAIRS):
        for t in range(n_new):
            first = (p // PAIRS_PER_KV) * rpb + (p % PAIRS_PER_KV) * n_new + t
            q_ref[pl.ds(first, n_seq, stride=N_KV_HEADS * rpb), :] = (
                qs_ref[t * n_seq:(t + 1) * n_seq, p * LANES:(p + 1) * LANES])
    kn_ref[...] = jnp.zeros(kn_ref.shape, F32)
    vn_ref[...] = jnp.zeros(vn_ref.shape, F32)
    for t in range(n_new):
        kn_ref[pl.ds(t, n_seq, stride=new_pad), :] = k_ref[t * n_seq:(t + 1) * n_seq, :]
        vn_ref[pl.ds(t, n_seq, stride=new_pad), :] = v_ref[t * n_seq:(t + 1) * n_seq, :]
    xb = z[:, XB_COL:GATE_COL]
    xb_ref[...] = xb
    gate_ref[...] = z[:, GATE_COL:]

    step = lambda a, t: a[t * n_seq:(t + 1) * n_seq, :]
    xs = [conv0_ref[j] for j in range(CONV_W - 1)] + [step(xb, t) for t in range(n_new)]
    us = []
    for t in range(n_new):
        u = cb_ref[...]
        for j in range(CONV_W):
            u = u + xs[t + j] * cw_ref[j:j + 1, :]
        us.append(u)
    a, b = _lru_gates(jnp.concatenate(us, axis=0), wab_ref[...], bab_ref[...],
                      jax.nn.log_sigmoid(lam_ref[...]))
    h = h0_ref[...]
    for t in range(n_new):
        h = step(a, t) * h + step(b, t)
        for s in range(N_SLABS):
            h_out_ref[s, t * n_seq:(t + 1) * n_seq, :] = h[:, s * LANES:(s + 1) * LANES]
    ht_ref[...] = h


def _inproj_sample(layer, x, g, w, cos, sin, conv0, h0, cw, cb, wab, bab, lam, n_new, new_pad):
    t = x.shape[0]
    n_seq = t // n_new
    kern = functools.partial(_inproj_sample_kernel, n_new=n_new)
    lyr = lambda a: _layer_spec(a, layer)
    whole = lambda *shape: _const_spec(shape)
    return pl.pallas_call(
        kern,
        grid=(1,),
        in_specs=[whole(t, D_MODEL), lyr(g), lyr(w), whole(t, LANES), whole(t, LANES),
                  lyr(conv0), lyr(h0), lyr(cw), lyr(cb), lyr(wab), lyr(bab), lyr(lam)],
        out_specs=[whole(t * N_PAIRS, LANES), whole(n_seq * new_pad, KV_W),
                   whole(n_seq * new_pad, KV_W),
                   whole(t, LRU_W), whole(t, LRU_W), whole(N_SLABS, t, LANES),
                   whole(n_seq, LRU_W)],
        out_shape=[jax.ShapeDtypeStruct((t * N_PAIRS, LANES), F32),
                   jax.ShapeDtypeStruct((n_seq * new_pad, KV_W), F32),
                   jax.ShapeDtypeStruct((n_seq * new_pad, KV_W), F32),
                   jax.ShapeDtypeStruct((t, LRU_W), F32),
                   jax.ShapeDtypeStruct((t, LRU_W), F32),
                   jax.ShapeDtypeStruct((N_SLABS, t, LANES), F32),
                   jax.ShapeDtypeStruct((n_seq, LRU_W), F32)],
        scratch_shapes=[pltpu.VMEM((t, ATTN_W), F32), pltpu.VMEM((t, KV_W), F32),
                        pltpu.VMEM((t, KV_W), F32)],
        compiler_params=_params(),
        name="inproj_sample",
    )(x, g, w, cos, sin, conv0, h0, cw, cb, wab, bab, lam)


MXU_COLS = 256
FF_SPLIT = (D_FF // MXU_COLS + 1) // 2 * MXU_COLS


def _gelu_tanh(x):
    return 0.5 * x * (1.0 + jnp.tanh(0.7978845608028654 * (x + 0.044715 * (x * x * x))))


def _mixer_out(x, att_ref, h_ref, gate_ref, ga_ref, gl_ref, wo_ref, gpm_ref, gpf_ref):
    att = _rms(att_ref[...], ga_ref[...]).astype(BF16)
    h = jnp.concatenate([h_ref[s] for s in range(N_SLABS)], axis=1)
    lru = _rms(h * _gelu_tanh(gate_ref[...]), gl_ref[...]).astype(BF16)
    m = _dot(att, wo_ref[0:ATTN_W, :]) + _dot(lru, wo_ref[ATTN_W:, :])
    x1 = x + _rms(m, gpm_ref[...])
    return x1, _rms(x1, gpf_ref[...]).astype(BF16)


def _swiglu(hf, wg_ref, wu_ref, wd_ref):
    f = None
    for lo, hi in ((0, FF_SPLIT), (FF_SPLIT, D_FF)):
        gt = _dot(hf, wg_ref[:, lo:hi])
        up = _dot(hf, wu_ref[:, lo:hi])
        hid = (gt * jax.nn.sigmoid(gt) * up).astype(BF16)
        part = _dot(hid, wd_ref[lo:hi, :])
        f = part if f is None else f + part
    return f


def _ffn_kernel(*refs, tiles_per_seq, meta_front):
    x_ref, refs = refs[0], refs[1:]
    meta_ref, refs = (refs[0], refs[1:]) if meta_front else (None, refs)
    (att_ref, h_ref, gate_ref, ga_ref, gl_ref, wo_ref, gpm_ref, gpf_ref,
     wg_ref, wu_ref, wd_ref, gpo_ref, o_ref) = refs
    x = _meta_front_tile(x_ref, meta_ref, tiles_per_seq) if meta_front else x_ref[...]
    x1, hf = _mixer_out(x, att_ref, h_ref, gate_ref, ga_ref, gl_ref, wo_ref, gpm_ref, gpf_ref)
    o_ref[...] = x1 + _rms(_swiglu(hf, wg_ref, wu_ref, wd_ref), gpo_ref[...])


def _ffn(layer, x, meta, att, h, gate, ga, gl, wo, gpm, gpf, wg, wu, wd, gpo,
         tm, tiles_per_seq, n_seq, drop_meta):
    keep = tiles_per_seq * tm
    seq = keep + drop_meta
    meta_front = meta is not None
    assert not (meta_front and drop_meta)
    r = SUBLANES
    assert tm % r == 0 and drop_meta % r == 0
    in_row = lambda i: r * ((i // tiles_per_seq) * (seq // r) + drop_meta // r
                            + (i % tiles_per_seq) * (tm // r))

    def rows(width):
        if drop_meta:
            return pl.BlockSpec((pl.Element(tm), pl.Element(width)), lambda i: (in_row(i), 0))
        return pl.BlockSpec((tm, width), lambda i: (i, 0))

    if drop_meta:
        h_spec = pl.BlockSpec((pl.Element(N_SLABS), pl.Element(tm), pl.Element(LANES)),
                              lambda i: (0, in_row(i), 0))
    else:
        h_spec = pl.BlockSpec((N_SLABS, tm, LANES), lambda i: (0, i, 0))
    out_spec = pl.BlockSpec((tm, D_MODEL), lambda i: (i, 0))
    if meta_front:
        x_specs = [_meta_front_spec(tm, tiles_per_seq, x.shape[0] // n_seq, meta.shape[0]),
                   _const_spec(meta.shape)]
        x_args = (x, meta)
    else:
        x_specs, x_args = [rows(D_MODEL)], (x,)
    big = lambda a: _layer_spec(a, layer, single_buffer=True)
    lyr = lambda a: _layer_spec(a, layer)
    return pl.pallas_call(
        functools.partial(_ffn_kernel, tiles_per_seq=tiles_per_seq, meta_front=meta_front),
        grid=(n_seq * tiles_per_seq,),
        in_specs=x_specs + [rows(ATTN_W), h_spec, rows(LRU_W), lyr(ga), lyr(gl),
                            big(wo), lyr(gpm), lyr(gpf), big(wg), big(wu), big(wd), lyr(gpo)],
        out_specs=out_spec,
        out_shape=jax.ShapeDtypeStruct((n_seq * keep, D_MODEL), F32),
        compiler_params=_params(),
        name="outffn",
    )(*x_args, att, h, gate, ga, gl, wo, gpm, gpf, wg, wu, wd, gpo)


def _kv_variants(k, v):
    lane = lax.broadcasted_iota(jnp.int32, k.shape, 1)
    low = lane < HEAD_DIM
    kr = pltpu.roll(k, HEAD_DIM, 1)
    vr = pltpu.roll(v, HEAD_DIM, 1)
    zero = jnp.zeros_like(k)
    ka = (jnp.where(low, k, zero), jnp.where(low, kr, zero))
    kb = (jnp.where(low, zero, kr), jnp.where(low, zero, k))
    vv = (jnp.where(low, v, vr), jnp.where(low, vr, v))
    cast = lambda pair: tuple(a.astype(BF16) for a in pair)
    return cast(ka), cast(kb), cast(vv)


def _attn_prompt_kernel(sink_ref, q_ref, k_ref, v_ref, o_ref,
                        ka_ref, kb_ref, vv_ref, *, layer, seq, chunk):
    for c in range(seq // chunk):
        rows = slice(c * chunk, (c + 1) * chunk)
        ka, kb, vv = _kv_variants(k_ref[0, rows, :], v_ref[0, rows, :])
        ones = jnp.ones((chunk, LANES), BF16)
        for h in range(N_KV_HEADS):
            ka_ref[h, rows, :] = ka[h]
            kb_ref[h, rows, :] = kb[h]
            vv_ref[h, rows, :] = jnp.concatenate([vv[h], ones], axis=1)

    def tile(q0, ks, nq, nk, sink_col):
        row = lax.broadcasted_iota(jnp.int32, (nq, nk), 0)
        col = lax.broadcasted_iota(jnp.int32, (nq, nk), 1)
        rel = row - col + (q0 - ks)
        mask = (rel >= 0) & (rel < WINDOW)
        col_row = lax.broadcasted_iota(jnp.int32, (1, nk), 1)
        key_row = lax.broadcasted_iota(jnp.int32, (nk, 2 * LANES), 0)
        val_lane = lax.broadcasted_iota(jnp.int32, (nk, 2 * LANES), 1)
        zero_v = (key_row == sink_col) & (val_lane < LANES)
        vvw = [jnp.where(zero_v, jnp.zeros((), BF16), vv_ref[h, pl.ds(ks, nk), :])
               for h in range(N_KV_HEADS)]
        low = lax.broadcasted_iota(jnp.int32, (nq, LANES), 1) < HEAD_DIM
        for p in range(N_PAIRS):
            h = p // PAIRS_PER_KV
            qp = q_ref[0, pl.ds(q0, nq), p * LANES:(p + 1) * LANES]
            halves = []
            for k_ref_, head in ((ka_ref, 2 * p), (kb_ref, 2 * p + 1)):
                s = _dot_nt(qp, k_ref_[h, pl.ds(ks, nk), :])
                fill = jnp.where(col_row == sink_col, sink_ref[layer, head], -jnp.inf)
                s = jnp.where(mask, s, fill)
                e = jnp.exp(s - jnp.max(s, axis=1, keepdims=True))
                pv = _dot(e.astype(BF16), vvw[h])
                halves.append(pv[:, :LANES] / pv[:, LANES:])
            o_ref[0, pl.ds(q0, nq), p * LANES:(p + 1) * LANES] = jnp.where(low, halves[0], halves[1])

    n_full = seq // WINDOW
    group = 8

    def body(it, carry):
        for t in range(group):
            i = it * group + t
            q0 = pl.multiple_of(i * WINDOW, WINDOW)
            ks = pl.multiple_of(jnp.maximum(q0 - WINDOW, 0), WINDOW)
            sink_col = jnp.where(i == 0, 2 * WINDOW - 1, 0) if t == 0 else 0
            tile(q0, ks, WINDOW, 2 * WINDOW, sink_col)
        return carry

    lax.fori_loop(0, n_full // group, body, 0)
    for i in range(n_full // group * group, n_full):
        tile(i * WINDOW, max(i - 1, 0) * WINDOW, WINDOW, 2 * WINDOW, 2 * WINDOW - 1 if i == 0 else 0)
    if seq > n_full * WINDOW:
        tile(seq - WINDOW, seq - 2 * WINDOW, WINDOW, 2 * WINDOW, 0)


def _attn_prompt(layer, sinks, q, k, v, chunk):
    b, seq, _ = q.shape
    assert seq >= 2 * WINDOW
    kern = functools.partial(_attn_prompt_kernel, layer=layer, seq=seq, chunk=chunk)
    seq_spec = lambda w_: pl.BlockSpec((1, seq, w_), lambda i: (i, 0, 0))
    return pl.pallas_call(
        kern,
        grid=(b,),
        in_specs=[pl.BlockSpec(memory_space=pltpu.SMEM),
                  seq_spec(ATTN_W), seq_spec(KV_W), seq_spec(KV_W)],
        out_specs=seq_spec(ATTN_W),
        out_shape=jax.ShapeDtypeStruct((b, seq, ATTN_W), F32),
        scratch_shapes=[pltpu.VMEM((N_KV_HEADS, seq, LANES), BF16),
                        pltpu.VMEM((N_KV_HEADS, seq, LANES), BF16),
                        pltpu.VMEM((N_KV_HEADS, seq, 2 * LANES), BF16)],
        compiler_params=_params(),
        name="attn_prompt",
    )(sinks, q, k, v)


def _lru_gates(u, wab, bab, logsig):
    ga = _dot(u.astype(BF16), wab) + bab
    r = jax.nn.sigmoid(ga[:, :LRU_W])
    ig = jax.nn.sigmoid(ga[:, LRU_W:])
    log_a = LRU_C * r * logsig
    a = jnp.exp(log_a)
    b = jnp.sqrt(-jnp.tanh(log_a) * (a * a + 1.0)) * ig * u
    return a, b


def _scan_interleaved(a_ref, b_ref, h0, rows):
    piece = rows // SUBLANES
    at = lambda i: slice(i * SUBLANES, (i + 1) * SUBLANES)
    hs = [jnp.zeros((SUBLANES, LANES), F32) for _ in range(N_SLABS)]
    ps = [jnp.ones((SUBLANES, LANES), F32) for _ in range(N_SLABS)]
    for i in range(piece):
        for s in range(N_SLABS):
            a = a_ref[s, at(i), :]
            hs[s] = a * hs[s] + b_ref[s, at(i), :]
            ps[s] = a * ps[s]
            b_ref[s, at(i), :] = hs[s]
            a_ref[s, at(i), :] = ps[s]

    sub = lax.broadcasted_iota(jnp.int32, (SUBLANES, LANES), 0)
    carry_in, h_last = [], []
    for s in range(N_SLABS):
        cin = jnp.broadcast_to(h0[:, s * LANES:(s + 1) * LANES], (SUBLANES, LANES))
        acc = jnp.zeros((SUBLANES, LANES), F32)
        for c in range(SUBLANES):
            acc = jnp.where(sub == c, cin, acc)
            end = hs[s] + ps[s] * cin
            cin = jnp.broadcast_to(end[c:c + 1, :], (SUBLANES, LANES))
        carry_in.append(acc)
        h_last.append(cin[0:1, :])

    for i in range(piece):
        for s in range(N_SLABS):
            b_ref[s, at(i), :] = b_ref[s, at(i), :] + a_ref[s, at(i), :] * carry_in[s]
    return jnp.concatenate(h_last, axis=1)


def _meta_front_tile(x_ref, meta_ref, tiles_per_seq):
    blk = x_ref[...]
    n_meta = meta_ref.shape[0]
    with_meta = jnp.concatenate([meta_ref[...], blk[:blk.shape[0] - n_meta]], axis=0)
    return jnp.where(pl.program_id(0) % tiles_per_seq == 0, with_meta, blk)


def _inproj_lru_kernel(*refs, tm, tiles_per_seq, meta_front):
    x_ref, refs = refs[0], refs[1:]
    meta_ref, refs = (refs[0], refs[1:]) if meta_front else (None, refs)
    (g_ref, w_ref, cos_ref, sin_ref, cw_ref, cb_ref, wab_ref, bab_ref, lam_ref,
     q_ref, k_ref, v_ref, h_out_ref, gate_ref, tail_ref, ht_ref,
     xs_ref, u_ref, a_ref, b_ref, h_ref) = refs
    pad = SUBLANES
    piece = tm // SUBLANES
    slab = lambda s: slice(s * LANES, (s + 1) * LANES)

    @pl.when(pl.program_id(0) % tiles_per_seq == 0)
    def _():
        xs_ref[:, 0:pad, :] = jnp.zeros((N_SLABS, pad, LANES), F32)
        h_ref[...] = jnp.zeros((1, LRU_W), F32)

    x = _meta_front_tile(x_ref, meta_ref, tiles_per_seq) if meta_front else x_ref[...]
    hn = _rms(x, g_ref[...]).astype(BF16)
    xb = _dot(hn, w_ref[:, XB_COL:GATE_COL])
    gate_ref[...] = _dot(hn, w_ref[:, GATE_COL:])
    tail_ref[0] = xb[tm - pad:, :]
    for s in range(N_SLABS):
        xs_ref[s, pad:pad + tm, :] = xb[:, slab(s)]

    for s in range(N_SLABS):
        bias = jnp.broadcast_to(cb_ref[:, slab(s)], (SUBLANES, LANES))
        wts = [jnp.broadcast_to(cw_ref[j:j + 1, slab(s)], (SUBLANES, LANES))
               for j in range(CONV_W)]
        taps = [xs_ref[s, pl.ds(pad - (CONV_W - 1) + k, SUBLANES, stride=piece), :]
                for k in range(CONV_W - 1)]
        for i in range(piece):
            taps.append(xs_ref[s, pl.ds(pad + i, SUBLANES, stride=piece), :])
            u = bias
            for j in range(CONV_W):
                u = u + taps[j] * wts[j]
            u_ref[i * SUBLANES:(i + 1) * SUBLANES, slab(s)] = u
            taps.pop(0)
    for s in range(N_SLABS):
        xs_ref[s, 0:pad, :] = xs_ref[s, tm:tm + pad, :]

    a, b = _lru_gates(u_ref[...], wab_ref[...], bab_ref[...], jax.nn.log_sigmoid(lam_ref[...]))
    for s in range(N_SLABS):
        a_ref[s] = a[:, slab(s)]
        b_ref[s] = b[:, slab(s)]
    h_last = _scan_interleaved(a_ref, b_ref, h_ref[...], tm)
    h_ref[...] = h_last
    ht_ref[0] = h_last
    for s in range(N_SLABS):
        for i in range(piece):
            h_out_ref[s, pl.ds(i, SUBLANES, stride=piece), :] = (
                b_ref[s, i * SUBLANES:(i + 1) * SUBLANES, :])

    _store_qkv(_dot(hn, w_ref[:, :XB_COL]), cos_ref, sin_ref, q_ref, k_ref, v_ref)


def _meta_front_spec(tm, tiles_per_seq, seq_in, n_meta):
    r = SUBLANES
    assert seq_in % r == 0 and tm % r == 0 and n_meta % r == 0

    def start(i):
        j = i % tiles_per_seq
        return r * ((i // tiles_per_seq) * (seq_in // r)
                    + jnp.maximum(j * (tm // r) - n_meta // r, 0)), 0
    return pl.BlockSpec((pl.Element(tm), pl.Element(D_MODEL)), start)


def _inproj_lru(layer, x, meta, g, w, cos, sin, cw, cb, wab, bab, lam, tm, tiles_per_seq, n_seq):
    t = n_seq * tiles_per_seq * tm
    assert tm % SUBLANES == 0 and tm >= SUBLANES
    meta_front = meta is not None
    kern = functools.partial(_inproj_lru_kernel, tm=tm, tiles_per_seq=tiles_per_seq,
                             meta_front=meta_front)
    tok = lambda w_: pl.BlockSpec((tm, w_), lambda i: (i, 0))
    tab = pl.BlockSpec((tm, LANES), lambda i: (i % tiles_per_seq, 0))
    per_seq = lambda r: pl.BlockSpec((1, r, LRU_W), lambda i: (i // tiles_per_seq, 0, 0))
    lyr = lambda a: _layer_spec(a, layer)
    if meta_front:
        x_specs = [_meta_front_spec(tm, tiles_per_seq, x.shape[0] // n_seq, meta.shape[0]),
                   _const_spec(meta.shape)]
        x_args = (x, meta)
    else:
        x_specs, x_args = [tok(D_MODEL)], (x,)
    return pl.pallas_call(
        kern,
        grid=(t // tm,),
        in_specs=x_specs + [lyr(g), lyr(w), tab, tab,
                            lyr(cw), lyr(cb), lyr(wab), lyr(bab), lyr(lam)],
        out_specs=[tok(ATTN_W), tok(KV_W), tok(KV_W),
                   pl.BlockSpec((N_SLABS, tm, LANES), lambda i: (0, i, 0)), tok(LRU_W),
                   per_seq(SUBLANES), per_seq(1)],
        out_shape=[jax.ShapeDtypeStruct((t, ATTN_W), BF16),
                   jax.ShapeDtypeStruct((t, KV_W), F32),
                   jax.ShapeDtypeStruct((t, KV_W), F32),
                   jax.ShapeDtypeStruct((N_SLABS, t, LANES), F32),
                   jax.ShapeDtypeStruct((t, LRU_W), F32),
                   jax.ShapeDtypeStruct((n_seq, SUBLANES, LRU_W), F32),
                   jax.ShapeDtypeStruct((n_seq, 1, LRU_W), F32)],
        scratch_shapes=[pltpu.VMEM((N_SLABS, tm + SUBLANES, LANES), F32),
                        pltpu.VMEM((tm, LRU_W), F32),
                        pltpu.VMEM((N_SLABS, tm, LANES), F32),
                        pltpu.VMEM((N_SLABS, tm, LANES), F32),
                        pltpu.VMEM((1, LRU_W), F32)],
        compiler_params=_params(),
        name="inproj_lru",
    )(*x_args, g, w, cos, sin, cw, cb, wab, bab, lam)


def _div_pow2(x, n):
    assert n & (n - 1) == 0
    return x >> (n.bit_length() - 1)


def _mod_pow2(x, n):
    assert n & (n - 1) == 0
    return x & (n - 1)


def _attn_sample_kernel(sink_ref, q_ref, kn_ref, vn_ref, ck_ref, cv_ref, nk_all_ref, nv_all_ref,
                        o_ref, nk_ref, nv_ref, os_ref, *, layer, group, n_new):
    del nk_all_ref, nv_all_ref
    rpb = PAIRS_PER_KV * n_new
    new_pad = kn_ref.shape[1]
    rows = group * rpb
    ncols = group * new_pad
    row = lax.broadcasted_iota(jnp.int32, (rows, 1), 0)
    qpos = _mod_pow2(row, n_new)
    row_seq = _div_pow2(row, rpb)
    second_pair = _mod_pow2(_div_pow2(row, n_new), PAIRS_PER_KV) == 1
    col = lax.broadcasted_iota(jnp.int32, (1, WINDOW), 1)
    mask_c = col > qpos
    if PAST_LEN < WINDOW:
        mask_c = mask_c & (col >= WINDOW - PAST_LEN)
    ncol = lax.broadcasted_iota(jnp.int32, (1, ncols), 1)
    mask_n = (_div_pow2(ncol, new_pad) == row_seq) & (_mod_pow2(ncol, new_pad) <= qpos)
    low = lax.broadcasted_iota(jnp.int32, (rpb, LANES), 1) < HEAD_DIM

    kn_all = kn_ref[...].reshape(ncols, KV_W)
    vn_all = vn_ref[...].reshape(ncols, KV_W)
    knt_all = kn_all.T
    vnt_all = vn_all.T
    vn_lane = lax.broadcasted_iota(jnp.int32, vn_all.shape, 1) < HEAD_DIM
    vn_roll = pltpu.roll(vn_all, HEAD_DIM, 1)

    keep = WINDOW - n_new
    cache_lane = lax.broadcasted_iota(jnp.int32, (HEAD_DIM, WINDOW), 1) < keep

    for h in range(N_KV_HEADS):
        hd = slice(h * HEAD_DIM, (h + 1) * HEAD_DIM)
        q_all = q_ref[:, h].reshape(rows, LANES).astype(BF16)
        q_seq = [q_ref[b, h].astype(BF16) for b in range(group)]
        knt_h = knt_all[hd, :].astype(BF16)
        vvn = (jnp.where(vn_lane, vn_all, vn_roll) if h == 0
               else jnp.where(vn_lane, vn_roll, vn_all)).astype(BF16)
        kt, vt2 = [], []
        for b in range(group):
            kt.append(ck_ref[b, h].astype(BF16))
            vt = cv_ref[b, h].astype(BF16)
            vt2.append(jnp.concatenate([vt, vt], axis=0))
        probs, dens = [], []
        for first in (True, False):
            embed = lambda t: jnp.concatenate(
                [t, jnp.zeros_like(t)] if first else [jnp.zeros_like(t), t], axis=0)
            s_new = jnp.where(mask_n, _dot(q_all, embed(knt_h)), -jnp.inf)
            s_c = jnp.concatenate(
                [_dot(q_seq[b], embed(kt[b])) for b in range(group)], axis=0)
            s_c = jnp.where(mask_c, s_c, -jnp.inf)
            pair0 = 2 * h * PAIRS_PER_KV + (0 if first else 1)
            sink = jnp.where(second_pair, sink_ref[layer,pair0 + 2], sink_ref[layer,pair0])
            mx = jnp.maximum(jnp.maximum(jnp.max(s_c, axis=1, keepdims=True),
                                         jnp.max(s_new, axis=1, keepdims=True)), sink)
            p_c = jnp.exp(s_c - mx)
            p_n = jnp.exp(s_new - mx)
            dens.append(jnp.sum(p_c, axis=1, keepdims=True) + jnp.sum(p_n, axis=1, keepdims=True)
                        + jnp.exp(sink - mx))
            probs.append((p_c, _dot(p_n.astype(BF16), vvn)))
        for b in range(group):
            rs = slice(b * rpb, (b + 1) * rpb)
            pc = jnp.concatenate([probs[0][0][rs], probs[1][0][rs]], axis=0).astype(BF16)
            oc = _dot_nt(pc, vt2[b])
            o_first = (oc[:rpb] + probs[0][1][rs]) / dens[0][rs]
            o_second = (oc[rpb:] + probs[1][1][rs]) / dens[1][rs]
            first_row = (b * N_KV_HEADS + h) * rpb
            os_ref[first_row:first_row + rpb, :] = jnp.where(low, o_first, o_second)
            shift = (keep - b * new_pad) % ncols
            nk_ref[b, h] = jnp.where(cache_lane, pltpu.roll(ck_ref[b, h], keep, 1),
                                     pltpu.roll(knt_all[hd, :], shift, 1)[:, :WINDOW])
            nv_ref[b, h] = jnp.where(cache_lane, pltpu.roll(cv_ref[b, h], keep, 1),
                                     pltpu.roll(vnt_all[hd, :], shift, 1)[:, :WINDOW])

    for p in range(N_PAIRS):
        for t in range(n_new):
            first = (p // PAIRS_PER_KV) * rpb + (p % PAIRS_PER_KV) * n_new + t
            o_ref[t, :, p * LANES:(p + 1) * LANES] = (
                os_ref[pl.ds(first, group, stride=N_KV_HEADS * rpb), :])


def _attn_sample(layer, sinks, q, kn, vn, ck, cv, nk_all, nv_all, group, n_new):
    bs = q.shape[0]
    kern = functools.partial(_attn_sample_kernel, layer=layer, group=group, n_new=n_new)
    seq4 = lambda a: pl.BlockSpec((group,) + a.shape[1:], lambda i: (i, 0, 0, 0))
    seq3 = lambda a: pl.BlockSpec((group,) + a.shape[1:], lambda i: (i, 0, 0))
    cache = pl.BlockSpec((None, group) + ck.shape[2:], lambda i: (layer, i, 0, 0, 0))
    in_hbm = pl.BlockSpec(memory_space=pl.ANY)
    return pl.pallas_call(
        kern,
        grid=(bs // group,),
        in_specs=[pl.BlockSpec(memory_space=pltpu.SMEM),
                  seq4(q), seq3(kn), seq3(vn), cache, cache, in_hbm, in_hbm],
        out_specs=[pl.BlockSpec((n_new, group, ATTN_W), lambda i: (0, i, 0)), cache, cache],
        out_shape=[jax.ShapeDtypeStruct((n_new, bs, ATTN_W), F32),
                   jax.ShapeDtypeStruct(nk_all.shape, F32),
                   jax.ShapeDtypeStruct(nv_all.shape, F32)],
        scratch_shapes=[pltpu.VMEM((group * q.shape[1] * q.shape[2], LANES), F32)],
        input_output_aliases={6: 1, 7: 2},
        compiler_params=_params(),
        name="attn_sample",
    )(sinks, q, kn, vn, ck, cv, nk_all, nv_all)


def _rope_tables(pos):
    half = HEAD_DIM // 2
    inv = ROPE_THETA ** (-jnp.arange(half, dtype=F32) / half)
    ang = pos.astype(F32)[:, None] * inv[None, :]
    cos, sin = jnp.cos(ang), jnp.sin(ang)
    reps = LANES // HEAD_DIM
    return (jnp.concatenate([cos, cos] * reps, axis=1),
            jnp.concatenate([-sin, sin] * reps, axis=1))


def _block_diag(w):
    rows = [jnp.pad(w[:, h], ((0, 0), (0, 0), (h * LRU_BLOCK_W, (LRU_BLOCKS - 1 - h) * LRU_BLOCK_W)))
            for h in range(LRU_BLOCKS)]
    return jnp.concatenate(rows, axis=1)


def _pick_tile(n, candidates):
    for c in candidates:
        if n % c == 0:
            return c
    raise ValueError(f"no tile for {n} tokens")


def kernel(x_prompt, x_sample, cache_k, cache_v, state_h, state_conv, meta_tokens, pre_mix_norm,
           w_in, sinks, conv_w, conv_b, w_a, b_a, w_i, b_i, lam, attn_out_norm, lru_out_norm,
           w_out, post_mix_norm, pre_ffn_norm, w_gate, w_up, w_down, post_ffn_norm):
    bp, seq_in, _ = x_prompt.shape
    seq = seq_in + N_META
    bs, n_new, _ = x_sample.shape
    assert seq % SUBLANES == 0 and (seq % WINDOW) % 16 == 0 and n_new >= CONV_W - 1

    vec = lambda a: a.reshape(DEPTH, 1, -1)
    w_in_b = w_in.astype(BF16)
    w_out_b = w_out.astype(BF16)
    w_gate_b = w_gate.astype(BF16)
    w_up_b = w_up.astype(BF16)
    w_down_b = w_down.astype(BF16)
    wab = jnp.concatenate([_block_diag(w_a), _block_diag(w_i)], axis=2).astype(BF16)
    bab = vec(jnp.concatenate([b_a, b_i], axis=1))
    g_pre, g_att, g_lru = vec(pre_mix_norm), vec(attn_out_norm), vec(lru_out_norm)
    g_pm, g_pf, g_po = vec(post_mix_norm), vec(pre_ffn_norm), vec(post_ffn_norm)
    cb, lam_v = vec(conv_b), vec(lam)

    assert DEPTH >= 2
    xp = x_prompt.reshape(bp * seq_in, D_MODEL)
    meta = meta_tokens.astype(x_prompt.dtype)
    xs = x_sample.transpose(1, 0, 2).reshape(n_new * bs, D_MODEL)

    cos_p, sin_p = _rope_tables(jnp.arange(seq, dtype=jnp.int32))
    pos_s = PAST_LEN + jnp.arange(n_new, dtype=jnp.int32)
    cos_s, sin_s = _rope_tables(jnp.repeat(pos_s, bs))

    tm_p = _pick_tile(seq, (688, 512, 256, 128, 16))
    tps = seq // tm_p
    tm_y = _pick_tile(seq_in, (512, 256, 128, 16))
    new_pad = SUBLANES
    group = 2 * LANES // new_pad
    assert n_new <= new_pad and bs % group == 0 and WINDOW == LANES

    ck_t = cache_k.transpose(0, 1, 3, 4, 2)
    cv_t = cache_v.transpose(0, 1, 3, 4, 2)
    conv_s = state_conv.transpose(0, 2, 1, 3)

    nk_all = jnp.zeros(ck_t.shape, F32)
    nv_all = jnp.zeros(cv_t.shape, F32)
    pk, pv, ph, pc, sh, sc = [], [], [], [], [], []
    for l in range(DEPTH):
        meta_l = meta if l == 0 else None
        q, k, v, h_seq, gate, xb_tail, ht = _inproj_lru(
            l, xp, meta_l, g_pre, w_in_b, cos_p, sin_p, conv_w, cb, wab, bab, lam_v, tm_p, tps, bp)
        k3 = k.reshape(bp, seq, KV_W)
        v3 = v.reshape(bp, seq, KV_W)
        att = _attn_prompt(l, sinks, q.reshape(bp, seq, ATTN_W), k3, v3, tm_p)
        ffn_tiling = (tm_p, tps, bp, 0) if l < DEPTH - 1 else (tm_y, seq_in // tm_y, bp, N_META)
        xp = _ffn(l, xp, meta_l, att.reshape(bp * seq, ATTN_W), h_seq, gate,
                  g_att, g_lru, w_out_b, g_pm, g_pf, w_gate_b, w_up_b, w_down_b, g_po, *ffn_tiling)
        pk.append(k3[:, -WINDOW:].reshape(bp, WINDOW, N_KV_HEADS, HEAD_DIM))
        pv.append(v3[:, -WINDOW:].reshape(bp, WINDOW, N_KV_HEADS, HEAD_DIM))
        ph.append(ht.reshape(bp, LRU_W))
        pc.append(xb_tail[:, -(CONV_W - 1):])

        q, kn, vn, xb, gate, h_slabs, ht = _inproj_sample(
            l, xs, g_pre, w_in_b, cos_s, sin_s, conv_s, state_h, conv_w, cb, wab, bab, lam_v,
            n_new, new_pad)
        q4 = q.reshape(bs, N_KV_HEADS, PAIRS_PER_KV * n_new, LANES)
        seq_major = lambda a: a.reshape(n_new, bs, -1).transpose(1, 0, 2)
        att, nk_all, nv_all = _attn_sample(l, sinks, q4, kn.reshape(bs, new_pad, KV_W),
                                           vn.reshape(bs, new_pad, KV_W), ck_t, cv_t,
                                           nk_all, nv_all, group, n_new)
        xs = _ffn(l, xs, None, att.reshape(n_new * bs, ATTN_W), h_slabs, gate,
                  g_att, g_lru, w_out_b, g_pm, g_pf, w_gate_b, w_up_b, w_down_b, g_po,
                  bs * n_new, 1, 1, 0)
        sh.append(ht)
        sc.append(seq_major(xb)[:, -(CONV_W - 1):])

    y_prompt = xp.reshape(bp, seq_in, D_MODEL)
    y_sample = xs.reshape(n_new, bs, D_MODEL).transpose(1, 0, 2)
    sample_k = nk_all.transpose(0, 1, 4, 2, 3)
    sample_v = nv_all.transpose(0, 1, 4, 2, 3)
    return (y_prompt, y_sample, jnp.stack(pk), jnp.stack(pv), jnp.stack(ph), jnp.stack(pc),
            sample_k, sample_v, jnp.stack(sh), jnp.stack(sc))
```

```python
import functools

import jax
import jax.numpy as jnp
from jax import lax
from jax.experimental import pallas as pl
from jax.experimental.pallas import tpu as pltpu

D_MODEL = 1024
DEPTH = 4
PAST_LEN = 8192
N_META = 16
HEAD_DIM = 64
N_Q_HEADS = 8
N_KV_HEADS = 2
ATTN_W = N_Q_HEADS * HEAD_DIM
KV_W = N_KV_HEADS * HEAD_DIM
LRU_W = D_MODEL - ATTN_W
LRU_BLOCKS = 8
LRU_BLOCK_W = LRU_W // LRU_BLOCKS
CONV_W = 4
LRU_C = 8.0
WINDOW = 128
ROPE_THETA = 10000.0
D_FF = 2816
IN_W = ATTN_W + 2 * KV_W + 2 * LRU_W
EPS = 1e-6

LANES = 128
SUBLANES = 8
N_SLABS = LRU_W // LANES
N_PAIRS = ATTN_W // LANES
PAIRS_PER_KV = N_PAIRS // N_KV_HEADS
VMEM_LIMIT = 56 * 1024 * 1024

BF16 = jnp.bfloat16
F32 = jnp.float32


def _rms(x, g):
    ms = jnp.mean(x * x, axis=-1, keepdims=True)
    return x * lax.rsqrt(ms + EPS) * g


def _dot(a, b):
    return jnp.dot(a, b, preferred_element_type=F32)


def _dot_nt(a, b):
    return lax.dot_general(a, b, (((1,), (1,)), ((), ())), preferred_element_type=F32)


def _const_spec(shape):
    zeros = (0,) * len(shape)
    return pl.BlockSpec(shape, lambda *_: zeros)


def _layer_spec(arr, layer, single_buffer=False):
    tail = (0,) * (arr.ndim - 1)
    mode = dict(pipeline_mode=pl.Buffered(1)) if single_buffer else {}
    return pl.BlockSpec((None,) + arr.shape[1:], lambda *_: (layer,) + tail, **mode)


def _params(n_axes=1):
    return pltpu.CompilerParams(dimension_semantics=("arbitrary",) * n_axes,
                                vmem_limit_bytes=VMEM_LIMIT)


XB_COL = ATTN_W + 2 * KV_W
GATE_COL = XB_COL + LRU_W


def _store_qkv(z, cos_ref, sin_ref, q_ref, k_ref, v_ref):
    cos = cos_ref[...]
    sin = sin_ref[...]
    lane = lax.broadcasted_iota(jnp.int32, cos.shape, 1)
    first_half = (lane & (HEAD_DIM // 2)) == 0

    def rope(t):
        swapped = jnp.where(first_half,
                            pltpu.roll(t, LANES - HEAD_DIM // 2, 1),
                            pltpu.roll(t, HEAD_DIM // 2, 1))
        return t * cos + swapped * sin

    for p in range(N_PAIRS):
        qp = rope(z[:, p * LANES:(p + 1) * LANES]) * (HEAD_DIM ** -0.5)
        q_ref[:, p * LANES:(p + 1) * LANES] = qp.astype(q_ref.dtype)
    k_ref[...] = rope(z[:, ATTN_W:ATTN_W + KV_W])
    v_ref[...] = z[:, ATTN_W + KV_W:XB_COL]


def _inproj_sample_kernel(x_ref, g_ref, w_ref, cos_ref, sin_ref, conv0_ref, h0_ref, cw_ref, cb_ref,
                          wab_ref, bab_ref, lam_ref,
                          q_ref, kn_ref, vn_ref, xb_ref, gate_ref, h_out_ref, ht_ref,
                          qs_ref, k_ref, v_ref, *, n_new):
    n_seq = h0_ref.shape[0]
    new_pad = kn_ref.shape[0] // n_seq
    hn = _rms(x_ref[...], g_ref[...]).astype(BF16)
    z = _dot(hn, w_ref[...].astype(BF16))
    _store_qkv(z[:, :XB_COL], cos_ref, sin_ref, qs_ref, k_ref, v_ref)
    rpb = PAIRS_PER_KV * n_new
    for p in range(N_PAIRS):
        for t in range(n_new):
            first = (p // PAIRS_PER_KV) * rpb + (p % PAIRS_PER_KV) * n_new + t
            q_ref[pl.ds(first, n_seq, stride=N_KV_HEADS * rpb), :] = (
                qs_ref[t * n_seq:(t + 1) * n_seq, p * LANES:(p + 1) * LANES])
    kn_ref[...] = jnp.zeros(kn_ref.shape, F32)
    vn_ref[...] = jnp.zeros(vn_ref.shape, F32)
    for t in range(n_new):
        kn_ref[pl.ds(t, n_seq, stride=new_pad), :] = k_ref[t * n_seq:(t + 1) * n_seq, :]
        vn_ref[pl.ds(t, n_seq, stride=new_pad), :] = v_ref[t * n_seq:(t + 1) * n_seq, :]
    xb = z[:, XB_COL:GATE_COL]
    xb_ref[...] = xb
    gate_ref[...] = z[:, GATE_COL:]

    step = lambda a, t: a[t * n_seq:(t + 1) * n_seq, :]
    xs = [conv0_ref[j] for j in range(CONV_W - 1)] + [step(xb, t) for t in range(n_new)]
    us = []
    for t in range(n_new):
        u = cb_ref[...]
        for j in range(CONV_W):
            u = u + xs[t + j] * cw_ref[j:j + 1, :]
        us.append(u)
    a, b = _lru_gates(jnp.concatenate(us, axis=0), wab_ref[...], bab_ref[...],
                      jax.nn.log_sigmoid(lam_ref[...]))
    h = h0_ref[...]
    for t in range(n_new):
        h = step(a, t) * h + step(b, t)
        for s in range(N_SLABS):
            h_out_ref[s, t * n_seq:(t + 1) * n_seq, :] = h[:, s * LANES:(s + 1) * LANES]
    ht_ref[...] = h


def _inproj_sample(layer, x, g, w, cos, sin, conv0, h0, cw, cb, wab, bab, lam, n_new, new_pad):
    t = x.shape[0]
    n_seq = t // n_new
    kern = functools.partial(_inproj_sample_kernel, n_new=n_new)
    lyr = lambda a: _layer_spec(a, layer)
    whole = lambda *shape: _const_spec(shape)
    return pl.pallas_call(
        kern,
        grid=(1,),
        in_specs=[whole(t, D_MODEL), lyr(g), lyr(w), whole(t, LANES), whole(t, LANES),
                  lyr(conv0), lyr(h0), lyr(cw), lyr(cb), lyr(wab), lyr(bab), lyr(lam)],
        out_specs=[whole(t * N_PAIRS, LANES), whole(n_seq * new_pad, KV_W),
                   whole(n_seq * new_pad, KV_W),
                   whole(t, LRU_W), whole(t, LRU_W), whole(N_SLABS, t, LANES),
                   whole(n_seq, LRU_W)],
        out_shape=[jax.ShapeDtypeStruct((t * N_PAIRS, LANES), F32),
                   jax.ShapeDtypeStruct((n_seq * new_pad, KV_W), F32),
                   jax.ShapeDtypeStruct((n_seq * new_pad, KV_W), F32),
                   jax.ShapeDtypeStruct((t, LRU_W), F32),
                   jax.ShapeDtypeStruct((t, LRU_W), F32),
                   jax.ShapeDtypeStruct((N_SLABS, t, LANES), F32),
                   jax.ShapeDtypeStruct((n_seq, LRU_W), F32)],
        scratch_shapes=[pltpu.VMEM((t, ATTN_W), F32), pltpu.VMEM((t, KV_W), F32),
                        pltpu.VMEM((t, KV_W), F32)],
        compiler_params=_params(),
        name="inproj_sample",
    )(x, g, w, cos, sin, conv0, h0, cw, cb, wab, bab, lam)


MXU_COLS = 256
FF_SPLIT = (D_FF // MXU_COLS + 1) // 2 * MXU_COLS


def _gelu_tanh(x):
    return 0.5 * x * (1.0 + jnp.tanh(0.7978845608028654 * (x + 0.044715 * (x * x * x))))


def _mixer_out(x, att_ref, h_ref, gate_ref, ga_ref, gl_ref, wo_ref, gpm_ref, gpf_ref):
    att = _rms(att_ref[...], ga_ref[...]).astype(BF16)
    h = jnp.concatenate([h_ref[s] for s in range(N_SLABS)], axis=1)
    lru = _rms(h * _gelu_tanh(gate_ref[...]), gl_ref[...]).astype(BF16)
    m = _dot(att, wo_ref[0:ATTN_W, :]) + _dot(lru, wo_ref[ATTN_W:, :])
    x1 = x + _rms(m, gpm_ref[...])
    return x1, _rms(x1, gpf_ref[...]).astype(BF16)


def _swiglu(hf, wg_ref, wu_ref, wd_ref):
    f = None
    for lo, hi in ((0, FF_SPLIT), (FF_SPLIT, D_FF)):
        gt = _dot(hf, wg_ref[:, lo:hi])
        up = _dot(hf, wu_ref[:, lo:hi])
        hid = (gt * jax.nn.sigmoid(gt) * up).astype(BF16)
        part = _dot(hid, wd_ref[lo:hi, :])
        f = part if f is None else f + part
    return f


def _ffn_kernel(*refs, tiles_per_seq, meta_front):
    x_ref, refs = refs[0], refs[1:]
    meta_ref, refs = (refs[0], refs[1:]) if meta_front else (None, refs)
    (att_ref, h_ref, gate_ref, ga_ref, gl_ref, wo_ref, gpm_ref, gpf_ref,
     wg_ref, wu_ref, wd_ref, gpo_ref, o_ref) = refs
    x = _meta_front_tile(x_ref, meta_ref, tiles_per_seq) if meta_front else x_ref[...]
    x1, hf = _mixer_out(x, att_ref, h_ref, gate_ref, ga_ref, gl_ref, wo_ref, gpm_ref, gpf_ref)
    o_ref[...] = x1 + _rms(_swiglu(hf, wg_ref, wu_ref, wd_ref), gpo_ref[...])


def _ffn(layer, x, meta, att, h, gate, ga, gl, wo, gpm, gpf, wg, wu, wd, gpo,
         tm, tiles_per_seq, n_seq, drop_meta):
    keep = tiles_per_seq * tm
    seq = keep + drop_meta
    meta_front = meta is not None
    assert not (meta_front and drop_meta)
    r = SUBLANES
    assert tm % r == 0 and drop_meta % r == 0
    in_row = lambda i: r * ((i // tiles_per_seq) * (seq // r) + drop_meta // r
                            + (i % tiles_per_seq) * (tm // r))

    def rows(width):
        if drop_meta:
            return pl.BlockSpec((pl.Element(tm), pl.Element(width)), lambda i: (in_row(i), 0))
        return pl.BlockSpec((tm, width), lambda i: (i, 0))

    if drop_meta:
        h_spec = pl.BlockSpec((pl.Element(N_SLABS), pl.Element(tm), pl.Element(LANES)),
                              lambda i: (0, in_row(i), 0))
    else:
        h_spec = pl.BlockSpec((N_SLABS, tm, LANES), lambda i: (0, i, 0))
    out_spec = pl.BlockSpec((tm, D_MODEL), lambda i: (i, 0))
    if meta_front:
        x_specs = [_meta_front_spec(tm, tiles_per_seq, x.shape[0] // n_seq, meta.shape[0]),
                   _const_spec(meta.shape)]
        x_args = (x, meta)
    else:
        x_specs, x_args = [rows(D_MODEL)], (x,)
    big = lambda a: _layer_spec(a, layer, single_buffer=True)
    lyr = lambda a: _layer_spec(a, layer)
    return pl.pallas_call(
        functools.partial(_ffn_kernel, tiles_per_seq=tiles_per_seq, meta_front=meta_front),
        grid=(n_seq * tiles_per_seq,),
        in_specs=x_specs + [rows(ATTN_W), h_spec, rows(LRU_W), lyr(ga), lyr(gl),
                            big(wo), lyr(gpm), lyr(gpf), big(wg), big(wu), big(wd), lyr(gpo)],
        out_specs=out_spec,
        out_shape=jax.ShapeDtypeStruct((n_seq * keep, D_MODEL), F32),
        compiler_params=_params(),
        name="outffn",
    )(*x_args, att, h, gate, ga, gl, wo, gpm, gpf, wg, wu, wd, gpo)


def _kv_variants(k, v):
    lane = lax.broadcasted_iota(jnp.int32, k.shape, 1)
    low = lane < HEAD_DIM
    kr = pltpu.roll(k, HEAD_DIM, 1)
    vr = pltpu.roll(v, HEAD_DIM, 1)
    zero = jnp.zeros_like(k)
    ka = (jnp.where(low, k, zero), jnp.where(low, kr, zero))
    kb = (jnp.where(low, zero, kr), jnp.where(low, zero, k))
    vv = (jnp.where(low, v, vr), jnp.where(low, vr, v))
    cast = lambda pair: tuple(a.astype(BF16) for a in pair)
    return cast(ka), cast(kb), cast(vv)


def _attn_prompt_kernel(sink_ref, q_ref, k_ref, v_ref, o_ref,
                        ka_ref, kb_ref, vv_ref, *, layer, seq, chunk):
    for c in range(seq // chunk):
        rows = slice(c * chunk, (c + 1) * chunk)
        ka, kb, vv = _kv_variants(k_ref[0, rows, :], v_ref[0, rows, :])
        ones = jnp.ones((chunk, LANES), BF16)
        for h in range(N_KV_HEADS):
            ka_ref[h, rows, :] = ka[h]
            kb_ref[h, rows, :] = kb[h]
            vv_ref[h, rows, :] = jnp.concatenate([vv[h], ones], axis=1)

    def tile(q0, ks, nq, nk, sink_col):
        row = lax.broadcasted_iota(jnp.int32, (nq, nk), 0)
        col = lax.broadcasted_iota(jnp.int32, (nq, nk), 1)
        rel = row - col + (q0 - ks)
        mask = (rel >= 0) & (rel < WINDOW)
        col_row = lax.broadcasted_iota(jnp.int32, (1, nk), 1)
        key_row = lax.broadcasted_iota(jnp.int32, (nk, 2 * LANES), 0)
        val_lane = lax.broadcasted_iota(jnp.int32, (nk, 2 * LANES), 1)
        zero_v = (key_row == sink_col) & (val_lane < LANES)
        vvw = [jnp.where(zero_v, jnp.zeros((), BF16), vv_ref[h, pl.ds(ks, nk), :])
               for h in range(N_KV_HEADS)]
        low = lax.broadcasted_iota(jnp.int32, (nq, LANES), 1) < HEAD_DIM
        for p in range(N_PAIRS):
            h = p // PAIRS_PER_KV
            qp = q_ref[0, pl.ds(q0, nq), p * LANES:(p + 1) * LANES]
            halves = []
            for k_ref_, head in ((ka_ref, 2 * p), (kb_ref, 2 * p + 1)):
                s = _dot_nt(qp, k_ref_[h, pl.ds(ks, nk), :])
                fill = jnp.where(col_row == sink_col, sink_ref[layer, head], -jnp.inf)
                s = jnp.where(mask, s, fill)
                e = jnp.exp(s - jnp.max(s, axis=1, keepdims=True))
                pv = _dot(e.astype(BF16), vvw[h])
                halves.append(pv[:, :LANES] / pv[:, LANES:])
            o_ref[0, pl.ds(q0, nq), p * LANES:(p + 1) * LANES] = jnp.where(low, halves[0], halves[1])

    n_full = seq // WINDOW
    group = 8

    def body(it, carry):
        for t in range(group):
            i = it * group + t
            q0 = pl.multiple_of(i * WINDOW, WINDOW)
            ks = pl.multiple_of(jnp.maximum(q0 - WINDOW, 0), WINDOW)
            sink_col = jnp.where(i == 0, 2 * WINDOW - 1, 0) if t == 0 else 0
            tile(q0, ks, WINDOW, 2 * WINDOW, sink_col)
        return carry

    lax.fori_loop(0, n_full // group, body, 0)
    for i in range(n_full // group * group, n_full):
        tile(i * WINDOW, max(i - 1, 0) * WINDOW, WINDOW, 2 * WINDOW, 2 * WINDOW - 1 if i == 0 else 0)
    if seq > n_full * WINDOW:
        tile(seq - WINDOW, seq - 2 * WINDOW, WINDOW, 2 * WINDOW, 0)


def _attn_prompt(layer, sinks, q, k, v, chunk):
    b, seq, _ = q.shape
    assert seq >= 2 * WINDOW
    kern = functools.partial(_attn_prompt_kernel, layer=layer, seq=seq, chunk=chunk)
    seq_spec = lambda w_: pl.BlockSpec((1, seq, w_), lambda i: (i, 0, 0))
    return pl.pallas_call(
        kern,
        grid=(b,),
        in_specs=[pl.BlockSpec(memory_space=pltpu.SMEM),
                  seq_spec(ATTN_W), seq_spec(KV_W), seq_spec(KV_W)],
        out_specs=seq_spec(ATTN_W),
        out_shape=jax.ShapeDtypeStruct((b, seq, ATTN_W), F32),
        scratch_shapes=[pltpu.VMEM((N_KV_HEADS, seq, LANES), BF16),
                        pltpu.VMEM((N_KV_HEADS, seq, LANES), BF16),
                        pltpu.VMEM((N_KV_HEADS, seq, 2 * LANES), BF16)],
        compiler_params=_params(),
        name="attn_prompt",
    )(sinks, q, k, v)


def _lru_gates(u, wab, bab, logsig):
    ga = _dot(u.astype(BF16), wab) + bab
    r = jax.nn.sigmoid(ga[:, :LRU_W])
    ig = jax.nn.sigmoid(ga[:, LRU_W:])
    log_a = LRU_C * r * logsig
    a = jnp.exp(log_a)
    b = jnp.sqrt(-jnp.tanh(log_a) * (a * a + 1.0)) * ig * u
    return a, b


def _scan_interleaved(a_ref, b_ref, h0, rows):
    piece = rows // SUBLANES
    at = lambda i: slice(i * SUBLANES, (i + 1) * SUBLANES)
    hs = [jnp.zeros((SUBLANES, LANES), F32) for _ in range(N_SLABS)]
    ps = [jnp.ones((SUBLANES, LANES), F32) for _ in range(N_SLABS)]
    for i in range(piece):
        for s in range(N_SLABS):
            a = a_ref[s, at(i), :]
            hs[s] = a * hs[s] + b_ref[s, at(i), :]
            ps[s] = a * ps[s]
            b_ref[s, at(i), :] = hs[s]
            a_ref[s, at(i), :] = ps[s]

    sub = lax.broadcasted_iota(jnp.int32, (SUBLANES, LANES), 0)
    carry_in, h_last = [], []
    for s in range(N_SLABS):
        cin = jnp.broadcast_to(h0[:, s * LANES:(s + 1) * LANES], (SUBLANES, LANES))
        acc = jnp.zeros((SUBLANES, LANES), F32)
        for c in range(SUBLANES):
            acc = jnp.where(sub == c, cin, acc)
            end = hs[s] + ps[s] * cin
            cin = jnp.broadcast_to(end[c:c + 1, :], (SUBLANES, LANES))
        carry_in.append(acc)
        h_last.append(cin[0:1, :])

    for i in range(piece):
        for s in range(N_SLABS):
            b_ref[s, at(i), :] = b_ref[s, at(i), :] + a_ref[s, at(i), :] * carry_in[s]
    return jnp.concatenate(h_last, axis=1)


def _meta_front_tile(x_ref, meta_ref, tiles_per_seq):
    blk = x_ref[...]
    n_meta = meta_ref.shape[0]
    with_meta = jnp.concatenate([meta_ref[...], blk[:blk.shape[0] - n_meta]], axis=0)
    return jnp.where(pl.program_id(0) % tiles_per_seq == 0, with_meta, blk)


def _inproj_lru_kernel(*refs, tm, tiles_per_seq, meta_front):
    x_ref, refs = refs[0], refs[1:]
    meta_ref, refs = (refs[0], refs[1:]) if meta_front else (None, refs)
    (g_ref, w_ref, cos_ref, sin_ref, cw_ref, cb_ref, wab_ref, bab_ref, lam_ref,
     q_ref, k_ref, v_ref, h_out_ref, gate_ref, tail_ref, ht_ref,
     xs_ref, u_ref, a_ref, b_ref, h_ref) = refs
    pad = SUBLANES
    piece = tm // SUBLANES
    slab = lambda s: slice(s * LANES, (s + 1) * LANES)

    @pl.when(pl.program_id(0) % tiles_per_seq == 0)
    def _():
        xs_ref[:, 0:pad, :] = jnp.zeros((N_SLABS, pad, LANES), F32)
        h_ref[...] = jnp.zeros((1, LRU_W), F32)

    x = _meta_front_tile(x_ref, meta_ref, tiles_per_seq) if meta_front else x_ref[...]
    hn = _rms(x, g_ref[...]).astype(BF16)
    w_cols = lambda lo, hi: w_ref[:, lo:hi].astype(BF16)
    xb = _dot(hn, w_cols(XB_COL, GATE_COL))
    gate_ref[...] = _dot(hn, w_cols(GATE_COL, IN_W))
    tail_ref[0] = xb[tm - pad:, :]
    for s in range(N_SLABS):
        xs_ref[s, pad:pad + tm, :] = xb[:, slab(s)]

    for s in range(N_SLABS):
        bias = jnp.broadcast_to(cb_ref[:, slab(s)], (SUBLANES, LANES))
        wts = [jnp.broadcast_to(cw_ref[j:j + 1, slab(s)], (SUBLANES, LANES))
               for j in range(CONV_W)]
        taps = [xs_ref[s, pl.ds(pad - (CONV_W - 1) + k, SUBLANES, stride=piece), :]
                for k in range(CONV_W - 1)]
        for i in range(piece):
            taps.append(xs_ref[s, pl.ds(pad + i, SUBLANES, stride=piece), :])
            u = bias
            for j in range(CONV_W):
                u = u + taps[j] * wts[j]
            u_ref[i * SUBLANES:(i + 1) * SUBLANES, slab(s)] = u
            taps.pop(0)
    for s in range(N_SLABS):
        xs_ref[s, 0:pad, :] = xs_ref[s, tm:tm + pad, :]

    a, b = _lru_gates(u_ref[...], wab_ref[...], bab_ref[...], jax.nn.log_sigmoid(lam_ref[...]))
    for s in range(N_SLABS):
        a_ref[s] = a[:, slab(s)]
        b_ref[s] = b[:, slab(s)]
    h_last = _scan_interleaved(a_ref, b_ref, h_ref[...], tm)
    h_ref[...] = h_last
    ht_ref[0] = h_last
    for s in range(N_SLABS):
        for i in range(piece):
            h_out_ref[s, pl.ds(i, SUBLANES, stride=piece), :] = (
                b_ref[s, i * SUBLANES:(i + 1) * SUBLANES, :])

    _store_qkv(_dot(hn, w_cols(0, XB_COL)), cos_ref, sin_ref, q_ref, k_ref, v_ref)


def _meta_front_spec(tm, tiles_per_seq, seq_in, n_meta):
    r = SUBLANES
    assert seq_in % r == 0 and tm % r == 0 and n_meta % r == 0

    def start(i):
        j = i % tiles_per_seq
        return r * ((i // tiles_per_seq) * (seq_in // r)
                    + jnp.maximum(j * (tm // r) - n_meta // r, 0)), 0
    return pl.BlockSpec((pl.Element(tm), pl.Element(D_MODEL)), start)


def _inproj_lru(layer, x, meta, g, w, cos, sin, cw, cb, wab, bab, lam, tm, tiles_per_seq, n_seq):
    t = n_seq * tiles_per_seq * tm
    assert tm % SUBLANES == 0 and tm >= SUBLANES
    meta_front = meta is not None
    kern = functools.partial(_inproj_lru_kernel, tm=tm, tiles_per_seq=tiles_per_seq,
                             meta_front=meta_front)
    tok = lambda w_: pl.BlockSpec((tm, w_), lambda i: (i, 0))
    tab = pl.BlockSpec((tm, LANES), lambda i: (i % tiles_per_seq, 0))
    per_seq = lambda r: pl.BlockSpec((1, r, LRU_W), lambda i: (i // tiles_per_seq, 0, 0))
    lyr = lambda a: _layer_spec(a, layer)
    if meta_front:
        x_specs = [_meta_front_spec(tm, tiles_per_seq, x.shape[0] // n_seq, meta.shape[0]),
                   _const_spec(meta.shape)]
        x_args = (x, meta)
    else:
        x_specs, x_args = [tok(D_MODEL)], (x,)
    return pl.pallas_call(
        kern,
        grid=(t // tm,),
        in_specs=x_specs + [lyr(g), lyr(w), tab, tab,
                            lyr(cw), lyr(cb), lyr(wab), lyr(bab), lyr(lam)],
        out_specs=[tok(ATTN_W), tok(KV_W), tok(KV_W),
                   pl.BlockSpec((N_SLABS, tm, LANES), lambda i: (0, i, 0)), tok(LRU_W),
                   per_seq(SUBLANES), per_seq(1)],
        out_shape=[jax.ShapeDtypeStruct((t, ATTN_W), BF16),
                   jax.ShapeDtypeStruct((t, KV_W), F32),
                   jax.ShapeDtypeStruct((t, KV_W), F32),
                   jax.ShapeDtypeStruct((N_SLABS, t, LANES), F32),
                   jax.ShapeDtypeStruct((t, LRU_W), F32),
                   jax.ShapeDtypeStruct((n_seq, SUBLANES, LRU_W), F32),
                   jax.ShapeDtypeStruct((n_seq, 1, LRU_W), F32)],
        scratch_shapes=[pltpu.VMEM((N_SLABS, tm + SUBLANES, LANES), F32),
                        pltpu.VMEM((tm, LRU_W), F32),
                        pltpu.VMEM((N_SLABS, tm, LANES), F32),
                        pltpu.VMEM((N_SLABS, tm, LANES), F32),
                        pltpu.VMEM((1, LRU_W), F32)],
        compiler_params=_params(),
        name="inproj_lru",
    )(*x_args, g, w, cos, sin, cw, cb, wab, bab, lam)


def _div_pow2(x, n):
    assert n & (n - 1) == 0
    return x >> (n.bit_length() - 1)


def _mod_pow2(x, n):
    assert n & (n - 1) == 0
    return x & (n - 1)


def _attn_sample_kernel(sink_ref, q_ref, kn_ref, vn_ref, ck_ref, cv_ref, nk_all_ref, nv_all_ref,
                        o_ref, nk_ref, nv_ref, os_ref, *, layer, group, n_new):
    del nk_all_ref, nv_all_ref
    rpb = PAIRS_PER_KV * n_new
    new_pad = kn_ref.shape[1]
    rows = group * rpb
    ncols = group * new_pad
    row = lax.broadcasted_iota(jnp.int32, (rows, 1), 0)
    qpos = _mod_pow2(row, n_new)
    row_seq = _div_pow2(row, rpb)
    second_pair = _mod_pow2(_div_pow2(row, n_new), PAIRS_PER_KV) == 1
    col = lax.broadcasted_iota(jnp.int32, (1, WINDOW), 1)
    mask_c = col > qpos
    if PAST_LEN < WINDOW:
        mask_c = mask_c & (col >= WINDOW - PAST_LEN)
    ncol = lax.broadcasted_iota(jnp.int32, (1, ncols), 1)
    mask_n = (_div_pow2(ncol, new_pad) == row_seq) & (_mod_pow2(ncol, new_pad) <= qpos)
    low = lax.broadcasted_iota(jnp.int32, (rpb, LANES), 1) < HEAD_DIM

    kn_all = kn_ref[...].reshape(ncols, KV_W)
    vn_all = vn_ref[...].reshape(ncols, KV_W)
    knt_all = kn_all.T
    vnt_all = vn_all.T
    vn_lane = lax.broadcasted_iota(jnp.int32, vn_all.shape, 1) < HEAD_DIM
    vn_roll = pltpu.roll(vn_all, HEAD_DIM, 1)

    keep = WINDOW - n_new
    cache_lane = lax.broadcasted_iota(jnp.int32, (HEAD_DIM, WINDOW), 1) < keep

    for h in range(N_KV_HEADS):
        hd = slice(h * HEAD_DIM, (h + 1) * HEAD_DIM)
        q_all = q_ref[:, h].reshape(rows, LANES).astype(BF16)
        q_seq = [q_ref[b, h].astype(BF16) for b in range(group)]
        knt_h = knt_all[hd, :].astype(BF16)
        vvn = (jnp.where(vn_lane, vn_all, vn_roll) if h == 0
               else jnp.where(vn_lane, vn_roll, vn_all)).astype(BF16)
        kt, vt2 = [], []
        for b in range(group):
            kt.append(ck_ref[b, h].astype(BF16))
            vt = cv_ref[b, h].astype(BF16)
            vt2.append(jnp.concatenate([vt, vt], axis=0))
        probs, dens = [], []
        for first in (True, False):
            embed = lambda t: jnp.concatenate(
                [t, jnp.zeros_like(t)] if first else [jnp.zeros_like(t), t], axis=0)
            s_new = jnp.where(mask_n, _dot(q_all, embed(knt_h)), -jnp.inf)
            s_c = jnp.concatenate(
                [_dot(q_seq[b], embed(kt[b])) for b in range(group)], axis=0)
            s_c = jnp.where(mask_c, s_c, -jnp.inf)
            pair0 = 2 * h * PAIRS_PER_KV + (0 if first else 1)
            sink = jnp.where(second_pair, sink_ref[layer,pair0 + 2], sink_ref[layer,pair0])
            mx = jnp.maximum(jnp.maximum(jnp.max(s_c, axis=1, keepdims=True),
                                         jnp.max(s_new, axis=1, keepdims=True)), sink)
            p_c = jnp.exp(s_c - mx)
            p_n = jnp.exp(s_new - mx)
            dens.append(jnp.sum(p_c, axis=1, keepdims=True) + jnp.sum(p_n, axis=1, keepdims=True)
                        + jnp.exp(sink - mx))
            probs.append((p_c, _dot(p_n.astype(BF16), vvn)))
        for b in range(group):
            rs = slice(b * rpb, (b + 1) * rpb)
            pc = jnp.concatenate([probs[0][0][rs], probs[1][0][rs]], axis=0).astype(BF16)
            oc = _dot_nt(pc, vt2[b])
            o_first = (oc[:rpb] + probs[0][1][rs]) / dens[0][rs]
            o_second = (oc[rpb:] + probs[1][1][rs]) / dens[1][rs]
            first_row = (b * N_KV_HEADS + h) * rpb
            os_ref[first_row:first_row + rpb, :] = jnp.where(low, o_first, o_second)
            shift = (keep - b * new_pad) % ncols
            nk_ref[b, h] = jnp.where(cache_lane, pltpu.roll(ck_ref[b, h], keep, 1),
                                     pltpu.roll(knt_all[hd, :], shift, 1)[:, :WINDOW])
            nv_ref[b, h] = jnp.where(cache_lane, pltpu.roll(cv_ref[b, h], keep, 1),
                                     pltpu.roll(vnt_all[hd, :], shift, 1)[:, :WINDOW])

    for p in range(N_PAIRS):
        for t in range(n_new):
            first = (p // PAIRS_PER_KV) * rpb + (p % PAIRS_PER_KV) * n_new + t
            o_ref[t, :, p * LANES:(p + 1) * LANES] = (
                os_ref[pl.ds(first, group, stride=N_KV_HEADS * rpb), :])


def _attn_sample(layer, sinks, q, kn, vn, ck, cv, nk_all, nv_all, group, n_new):
    bs = q.shape[0]
    kern = functools.partial(_attn_sample_kernel, layer=layer, group=group, n_new=n_new)
    seq4 = lambda a: pl.BlockSpec((group,) + a.shape[1:], lambda i: (i, 0, 0, 0))
    seq3 = lambda a: pl.BlockSpec((group,) + a.shape[1:], lambda i: (i, 0, 0))
    cache = pl.BlockSpec((None, group) + ck.shape[2:], lambda i: (layer, i, 0, 0, 0))
    in_hbm = pl.BlockSpec(memory_space=pl.ANY)
    return pl.pallas_call(
        kern,
        grid=(bs // group,),
        in_specs=[pl.BlockSpec(memory_space=pltpu.SMEM),
                  seq4(q), seq3(kn), seq3(vn), cache, cache, in_hbm, in_hbm],
        out_specs=[pl.BlockSpec((n_new, group, ATTN_W), lambda i: (0, i, 0)), cache, cache],
        out_shape=[jax.ShapeDtypeStruct((n_new, bs, ATTN_W), F32),
                   jax.ShapeDtypeStruct(nk_all.shape, F32),
                   jax.ShapeDtypeStruct(nv_all.shape, F32)],
        scratch_shapes=[pltpu.VMEM((group * q.shape[1] * q.shape[2], LANES), F32)],
        input_output_aliases={6: 1, 7: 2},
        compiler_params=_params(),
        name="attn_sample",
    )(sinks, q, kn, vn, ck, cv, nk_all, nv_all)


def _rope_tables(pos):
    half = HEAD_DIM // 2
    inv = ROPE_THETA ** (-jnp.arange(half, dtype=F32) / half)
    ang = pos.astype(F32)[:, None] * inv[None, :]
    cos, sin = jnp.cos(ang), jnp.sin(ang)
    reps = LANES // HEAD_DIM
    return (jnp.concatenate([cos, cos] * reps, axis=1),
            jnp.concatenate([-sin, sin] * reps, axis=1))


def _block_diag(w):
    rows = [jnp.pad(w[:, h], ((0, 0), (0, 0), (h * LRU_BLOCK_W, (LRU_BLOCKS - 1 - h) * LRU_BLOCK_W)))
            for h in range(LRU_BLOCKS)]
    return jnp.concatenate(rows, axis=1)


def _pick_tile(n, candidates):
    for c in candidates:
        if n % c == 0:
            return c
    raise ValueError(f"no tile for {n} tokens")


def kernel(x_prompt, x_sample, cache_k, cache_v, state_h, state_conv, meta_tokens, pre_mix_norm,
           w_in, sinks, conv_w, conv_b, w_a, b_a, w_i, b_i, lam, attn_out_norm, lru_out_norm,
           w_out, post_mix_norm, pre_ffn_norm, w_gate, w_up, w_down, post_ffn_norm):
    bp, seq_in, _ = x_prompt.shape
    seq = seq_in + N_META
    bs, n_new, _ = x_sample.shape
    assert seq % SUBLANES == 0 and (seq % WINDOW) % 16 == 0 and n_new >= CONV_W - 1

    vec = lambda a: a.reshape(DEPTH, 1, -1)
    w_out_b = w_out.astype(BF16)
    w_gate_b = w_gate.astype(BF16)
    w_up_b = w_up.astype(BF16)
    w_down_b = w_down.astype(BF16)
    wab = jnp.concatenate([_block_diag(w_a), _block_diag(w_i)], axis=2).astype(BF16)
    bab = vec(jnp.concatenate([b_a, b_i], axis=1))
    g_pre, g_att, g_lru = vec(pre_mix_norm), vec(attn_out_norm), vec(lru_out_norm)
    g_pm, g_pf, g_po = vec(post_mix_norm), vec(pre_ffn_norm), vec(post_ffn_norm)
    cb, lam_v = vec(conv_b), vec(lam)

    assert DEPTH >= 2
    xp = x_prompt.reshape(bp * seq_in, D_MODEL)
    meta = meta_tokens.astype(x_prompt.dtype)
    xs = x_sample.transpose(1, 0, 2).reshape(n_new * bs, D_MODEL)

    cos_p, sin_p = _rope_tables(jnp.arange(seq, dtype=jnp.int32))
    pos_s = PAST_LEN + jnp.arange(n_new, dtype=jnp.int32)
    cos_s, sin_s = _rope_tables(jnp.repeat(pos_s, bs))

    tm_p = _pick_tile(seq, (688, 512, 256, 128, 16))
    tps = seq // tm_p
    tm_y = _pick_tile(seq_in, (512, 256, 128, 16))
    new_pad = SUBLANES
    group = 2 * LANES // new_pad
    assert n_new <= new_pad and bs % group == 0 and WINDOW == LANES

    ck_t = cache_k.transpose(0, 1, 3, 4, 2)
    cv_t = cache_v.transpose(0, 1, 3, 4, 2)
    conv_s = state_conv.transpose(0, 2, 1, 3)

    nk_all = jnp.zeros(ck_t.shape, F32)
    nv_all = jnp.zeros(cv_t.shape, F32)
    pk, pv, ph, pc, sh, sc = [], [], [], [], [], []
    for l in range(DEPTH):
        meta_l = meta if l == 0 else None
        q, k, v, h_seq, gate, xb_tail, ht = _inproj_lru(
            l, xp, meta_l, g_pre, w_in, cos_p, sin_p, conv_w, cb, wab, bab, lam_v, tm_p, tps, bp)
        k3 = k.reshape(bp, seq, KV_W)
        v3 = v.reshape(bp, seq, KV_W)
        att = _attn_prompt(l, sinks, q.reshape(bp, seq, ATTN_W), k3, v3, tm_p)
        ffn_tiling = (tm_p, tps, bp, 0) if l < DEPTH - 1 else (tm_y, seq_in // tm_y, bp, N_META)
        xp = _ffn(l, xp, meta_l, att.reshape(bp * seq, ATTN_W), h_seq, gate,
                  g_att, g_lru, w_out_b, g_pm, g_pf, w_gate_b, w_up_b, w_down_b, g_po, *ffn_tiling)
        pk.append(k3[:, -WINDOW:].reshape(bp, WINDOW, N_KV_HEADS, HEAD_DIM))
        pv.append(v3[:, -WINDOW:].reshape(bp, WINDOW, N_KV_HEADS, HEAD_DIM))
        ph.append(ht.reshape(bp, LRU_W))
        pc.append(xb_tail[:, -(CONV_W - 1):])

        q, kn, vn, xb, gate, h_slabs, ht = _inproj_sample(
            l, xs, g_pre, w_in, cos_s, sin_s, conv_s, state_h, conv_w, cb, wab, bab, lam_v,
            n_new, new_pad)
        q4 = q.reshape(bs, N_KV_HEADS, PAIRS_PER_KV * n_new, LANES)
        seq_major = lambda a: a.reshape(n_new, bs, -1).transpose(1, 0, 2)
        att, nk_all, nv_all = _attn_sample(l, sinks, q4, kn.reshape(bs, new_pad, KV_W),
                                           vn.reshape(bs, new_pad, KV_W), ck_t, cv_t,
                                           nk_all, nv_all, group, n_new)
        xs = _ffn(l, xs, None, att.reshape(n_new * bs, ATTN_W), h_slabs, gate,
                  g_att, g_lru, w_out_b, g_pm, g_pf, w_gate_b, w_up_b, w_down_b, g_po,
                  bs * n_new, 1, 1, 0)
        sh.append(ht)
        sc.append(seq_major(xb)[:, -(CONV_W - 1):])

    y_prompt = xp.reshape(bp, seq_in, D_MODEL)
    y_sample = xs.reshape(n_new, bs, D_MODEL).transpose(1, 0, 2)
    sample_k = nk_all.transpose(0, 1, 4, 2, 3)
    sample_v = nv_all.transpose(0, 1, 4, 2, 3)
    return (y_prompt, y_sample, jnp.stack(pk), jnp.stack(pv), jnp.stack(ph), jnp.stack(pc),
            sample_k, sample_v, jnp.stack(sh), jnp.stack(sc))
```

```python
import functools

import jax
import jax.numpy as jnp
from jax import lax
from jax.experimental import pallas as pl
from jax.experimental.pallas import tpu as pltpu

D_MODEL = 1024
DEPTH = 4
PAST_LEN = 8192
N_META = 16
HEAD_DIM = 64
N_Q_HEADS = 8
N_KV_HEADS = 2
ATTN_W = N_Q_HEADS * HEAD_DIM
KV_W = N_KV_HEADS * HEAD_DIM
LRU_W = D_MODEL - ATTN_W
LRU_BLOCKS = 8
LRU_BLOCK_W = LRU_W // LRU_BLOCKS
CONV_W = 4
LRU_C = 8.0
WINDOW = 128
ROPE_THETA = 10000.0
D_FF = 2816
IN_W = ATTN_W + 2 * KV_W + 2 * LRU_W
EPS = 1e-6

LANES = 128
SUBLANES = 8
N_SLABS = LRU_W // LANES
N_PAIRS = ATTN_W // LANES
PAIRS_PER_KV = N_PAIRS // N_KV_HEADS
VMEM_LIMIT = 56 * 1024 * 1024

BF16 = jnp.bfloat16
F32 = jnp.float32


def _rms(x, g):
    ms = jnp.mean(x * x, axis=-1, keepdims=True)
    return x * lax.rsqrt(ms + EPS) * g


def _dot(a, b):
    return jnp.dot(a, b, preferred_element_type=F32)


def _dot_nt(a, b):
    return lax.dot_general(a, b, (((1,), (1,)), ((), ())), preferred_element_type=F32)


def _const_spec(shape):
    zeros = (0,) * len(shape)
    return pl.BlockSpec(shape, lambda *_: zeros)


def _layer_spec(arr, layer, single_buffer=False):
    tail = (0,) * (arr.ndim - 1)
    mode = dict(pipeline_mode=pl.Buffered(1)) if single_buffer else {}
    return pl.BlockSpec((None,) + arr.shape[1:], lambda *_: (layer,) + tail, **mode)


def _params(n_axes=1):
    return pltpu.CompilerParams(dimension_semantics=("arbitrary",) * n_axes,
                                vmem_limit_bytes=VMEM_LIMIT)


XB_COL = ATTN_W + 2 * KV_W
GATE_COL = XB_COL + LRU_W


def _store_qkv(z, cos_ref, sin_ref, q_ref, k_ref, v_ref):
    cos = cos_ref[...]
    sin = sin_ref[...]
    lane = lax.broadcasted_iota(jnp.int32, cos.shape, 1)
    first_half = (lane & (HEAD_DIM // 2)) == 0

    def rope(t):
        swapped = jnp.where(first_half,
                            pltpu.roll(t, LANES - HEAD_DIM // 2, 1),
                            pltpu.roll(t, HEAD_DIM // 2, 1))
        return t * cos + swapped * sin

    for p in range(N_PAIRS):
        qp = rope(z[:, p * LANES:(p + 1) * LANES]) * (HEAD_DIM ** -0.5)
        q_ref[:, p * LANES:(p + 1) * LANES] = qp.astype(q_ref.dtype)
    k_ref[...] = rope(z[:, ATTN_W:ATTN_W + KV_W])
    v_ref[...] = z[:, ATTN_W + KV_W:XB_COL]


def _inproj_sample_kernel(x_ref, g_ref, w_ref, cos_ref, sin_ref, conv0_ref, h0_ref, cw_ref, cb_ref,
                          wab_ref, bab_ref, lam_ref,
                          q_ref, kn_ref, vn_ref, xb_ref, gate_ref, h_out_ref, ht_ref,
                          qs_ref, k_ref, v_ref, *, n_new):
    n_seq = h0_ref.shape[0]
    new_pad = kn_ref.shape[0] // n_seq
    hn = _rms(x_ref[...], g_ref[...]).astype(BF16)
    z = _dot(hn, w_ref[...].astype(BF16))
    _store_qkv(z[:, :XB_COL], cos_ref, sin_ref, qs_ref, k_ref, v_ref)
    rpb = PAIRS_PER_KV * n_new
    for p in range(N_PAIRS):
        for t in range(n_new):
            first = (p // PAIRS_PER_KV) * rpb + (p % PAIRS_PER_KV) * n_new + t
            q_ref[pl.ds(first, n_seq, stride=N_KV_HEADS * rpb), :] = (
                qs_ref[t * n_seq:(t + 1) * n_seq, p * LANES:(p + 1) * LANES])
    kn_ref[...] = jnp.zeros(kn_ref.shape, F32)
    vn_ref[...] = jnp.zeros(vn_ref.shape, F32)
    for t in range(n_new):
        kn_ref[pl.ds(t, n_seq, stride=new_pad), :] = k_ref[t * n_seq:(t + 1) * n_seq, :]
        vn_ref[pl.ds(t, n_seq, stride=new_pad), :] = v_ref[t * n_seq:(t + 1) * n_seq, :]
    xb = z[:, XB_COL:GATE_COL]
    xb_ref[...] = xb
    gate_ref[...] = z[:, GATE_COL:]

    step = lambda a, t: a[t * n_seq:(t + 1) * n_seq, :]
    xs = [conv0_ref[j] for j in range(CONV_W - 1)] + [step(xb, t) for t in range(n_new)]
    us = []
    for t in range(n_new):
        u = cb_ref[...]
        for j in range(CONV_W):
            u = u + xs[t + j] * cw_ref[j:j + 1, :]
        us.append(u)
    a, b = _lru_gates(jnp.concatenate(us, axis=0), wab_ref[...], bab_ref[...],
                      jax.nn.log_sigmoid(lam_ref[...]))
    h = h0_ref[...]
    for t in range(n_new):
        h = step(a, t) * h + step(b, t)
        for s in range(N_SLABS):
            h_out_ref[s, t * n_seq:(t + 1) * n_seq, :] = h[:, s * LANES:(s + 1) * LANES]
    ht_ref[...] = h


def _inproj_sample(layer, x, g, w, cos, sin, conv0, h0, cw, cb, wab, bab, lam, n_new, new_pad):
    t = x.shape[0]
    n_seq = t // n_new
    kern = functools.partial(_inproj_sample_kernel, n_new=n_new)
    lyr = lambda a: _layer_spec(a, layer)
    whole = lambda *shape: _const_spec(shape)
    return pl.pallas_call(
        kern,
        grid=(1,),
        in_specs=[whole(t, D_MODEL), lyr(g), lyr(w), whole(t, LANES), whole(t, LANES),
                  lyr(conv0), lyr(h0), lyr(cw), lyr(cb), lyr(wab), lyr(bab), lyr(lam)],
        out_specs=[whole(t * N_PAIRS, LANES), whole(n_seq * new_pad, KV_W),
                   whole(n_seq * new_pad, KV_W),
                   whole(t, LRU_W), whole(t, LRU_W), whole(N_SLABS, t, LANES),
                   whole(n_seq, LRU_W)],
        out_shape=[jax.ShapeDtypeStruct((t * N_PAIRS, LANES), F32),
                   jax.ShapeDtypeStruct((n_seq * new_pad, KV_W), F32),
                   jax.ShapeDtypeStruct((n_seq * new_pad, KV_W), F32),
                   jax.ShapeDtypeStruct((t, LRU_W), F32),
                   jax.ShapeDtypeStruct((t, LRU_W), F32),
                   jax.ShapeDtypeStruct((N_SLABS, t, LANES), F32),
                   jax.ShapeDtypeStruct((n_seq, LRU_W), F32)],
        scratch_shapes=[pltpu.VMEM((t, ATTN_W), F32), pltpu.VMEM((t, KV_W), F32),
                        pltpu.VMEM((t, KV_W), F32)],
        compiler_params=_params(),
        name="inproj_sample",
    )(x, g, w, cos, sin, conv0, h0, cw, cb, wab, bab, lam)


MXU_COLS = 256
FF_SPLIT = (D_FF // MXU_COLS + 1) // 2 * MXU_COLS


def _gelu_tanh(x):
    return 0.5 * x * (1.0 + jnp.tanh(0.7978845608028654 * (x + 0.044715 * (x * x * x))))


def _mixer_out(x, att_ref, h_ref, gate_ref, ga_ref, gl_ref, wo_ref, gpm_ref, gpf_ref):
    att = _rms(att_ref[...], ga_ref[...]).astype(BF16)
    h = jnp.concatenate([h_ref[s] for s in range(N_SLABS)], axis=1)
    lru = _rms(h * _gelu_tanh(gate_ref[...]), gl_ref[...]).astype(BF16)
    m = _dot(att, wo_ref[0:ATTN_W, :]) + _dot(lru, wo_ref[ATTN_W:, :])
    x1 = x + _rms(m, gpm_ref[...])
    return x1, _rms(x1, gpf_ref[...]).astype(BF16)


def _swiglu(hf, wg_ref, wu_ref, wd_ref):
    f = None
    for lo, hi in ((0, FF_SPLIT), (FF_SPLIT, D_FF)):
        gt = _dot(hf, wg_ref[:, lo:hi])
        up = _dot(hf, wu_ref[:, lo:hi])
        hid = (gt * jax.nn.sigmoid(gt) * up).astype(BF16)
        part = _dot(hid, wd_ref[lo:hi, :])
        f = part if f is None else f + part
    return f


def _ffn_kernel(*refs, tiles_per_seq, meta_front):
    x_ref, refs = refs[0], refs[1:]
    meta_ref, refs = (refs[0], refs[1:]) if meta_front else (None, refs)
    (att_ref, h_ref, gate_ref, ga_ref, gl_ref, wo_ref, gpm_ref, gpf_ref,
     wg_ref, wu_ref, wd_ref, gpo_ref, o_ref) = refs
    x = _meta_front_tile(x_ref, meta_ref, tiles_per_seq) if meta_front else x_ref[...]
    x1, hf = _mixer_out(x, att_ref, h_ref, gate_ref, ga_ref, gl_ref, wo_ref, gpm_ref, gpf_ref)
    o_ref[...] = x1 + _rms(_swiglu(hf, wg_ref, wu_ref, wd_ref), gpo_ref[...])


def _ffn(layer, x, meta, att, h, gate, ga, gl, wo, gpm, gpf, wg, wu, wd, gpo,
         tm, tiles_per_seq, n_seq, drop_meta):
    keep = tiles_per_seq * tm
    seq = keep + drop_meta
    meta_front = meta is not None
    assert not (meta_front and drop_meta)
    r = SUBLANES
    assert tm % r == 0 and drop_meta % r == 0
    in_row = lambda i: r * ((i // tiles_per_seq) * (seq // r) + drop_meta // r
                            + (i % tiles_per_seq) * (tm // r))

    def rows(width):
        if drop_meta:
            return pl.BlockSpec((pl.Element(tm), pl.Element(width)), lambda i: (in_row(i), 0))
        return pl.BlockSpec((tm, width), lambda i: (i, 0))

    if drop_meta:
        h_spec = pl.BlockSpec((pl.Element(N_SLABS), pl.Element(tm), pl.Element(LANES)),
                              lambda i: (0, in_row(i), 0))
    else:
        h_spec = pl.BlockSpec((N_SLABS, tm, LANES), lambda i: (0, i, 0))
    out_spec = pl.BlockSpec((tm, D_MODEL), lambda i: (i, 0))
    if meta_front:
        x_specs = [_meta_front_spec(tm, tiles_per_seq, x.shape[0] // n_seq, meta.shape[0]),
                   _const_spec(meta.shape)]
        x_args = (x, meta)
    else:
        x_specs, x_args = [rows(D_MODEL)], (x,)
    big = lambda a: _layer_spec(a, layer, single_buffer=True)
    lyr = lambda a: _layer_spec(a, layer)
    return pl.pallas_call(
        functools.partial(_ffn_kernel, tiles_per_seq=tiles_per_seq, meta_front=meta_front),
        grid=(n_seq * tiles_per_seq,),
        in_specs=x_specs + [rows(ATTN_W), h_spec, rows(LRU_W), lyr(ga), lyr(gl),
                            big(wo), lyr(gpm), lyr(gpf), big(wg), big(wu), big(wd), lyr(gpo)],
        out_specs=out_spec,
        out_shape=jax.ShapeDtypeStruct((n_seq * keep, D_MODEL), F32),
        compiler_params=_params(),
        name="outffn",
    )(*x_args, att, h, gate, ga, gl, wo, gpm, gpf, wg, wu, wd, gpo)


def _kv_variants(k, v):
    lane = lax.broadcasted_iota(jnp.int32, k.shape, 1)
    low = lane < HEAD_DIM
    kr = pltpu.roll(k, HEAD_DIM, 1)
    vr = pltpu.roll(v, HEAD_DIM, 1)
    zero = jnp.zeros_like(k)
    ka = (jnp.where(low, k, zero), jnp.where(low, kr, zero))
    kb = (jnp.where(low, zero, kr), jnp.where(low, zero, k))
    vv = (jnp.where(low, v, vr), jnp.where(low, vr, v))
    cast = lambda pair: tuple(a.astype(BF16) for a in pair)
    return cast(ka), cast(kb), cast(vv)


def _attn_prompt_kernel(sink_ref, q_ref, k_ref, v_ref, o_ref,
                        ka_ref, kb_ref, vv_ref, *, layer, seq, chunk):
    for c in range(seq // chunk):
        rows = slice(c * chunk, (c + 1) * chunk)
        ka, kb, vv = _kv_variants(k_ref[0, rows, :], v_ref[0, rows, :])
        ones = jnp.ones((chunk, LANES), BF16)
        for h in range(N_KV_HEADS):
            ka_ref[h, rows, :] = ka[h]
            kb_ref[h, rows, :] = kb[h]
            vv_ref[h, rows, :] = jnp.concatenate([vv[h], ones], axis=1)

    def tile(q0, ks, nq, nk, sink_col):
        row = lax.broadcasted_iota(jnp.int32, (nq, nk), 0)
        col = lax.broadcasted_iota(jnp.int32, (nq, nk), 1)
        rel = row - col + (q0 - ks)
        mask = (rel >= 0) & (rel < WINDOW)
        col_row = lax.broadcasted_iota(jnp.int32, (1, nk), 1)
        key_row = lax.broadcasted_iota(jnp.int32, (nk, 2 * LANES), 0)
        val_lane = lax.broadcasted_iota(jnp.int32, (nk, 2 * LANES), 1)
        zero_v = (key_row == sink_col) & (val_lane < LANES)
        vvw = [jnp.where(zero_v, jnp.zeros((), BF16), vv_ref[h, pl.ds(ks, nk), :])
               for h in range(N_KV_HEADS)]
        low = lax.broadcasted_iota(jnp.int32, (nq, LANES), 1) < HEAD_DIM
        for p in range(N_PAIRS):
            h = p // PAIRS_PER_KV
            qp = q_ref[0, pl.ds(q0, nq), p * LANES:(p + 1) * LANES]
            halves = []
            for k_ref_, head in ((ka_ref, 2 * p), (kb_ref, 2 * p + 1)):
                s = _dot_nt(qp, k_ref_[h, pl.ds(ks, nk), :])
                fill = jnp.where(col_row == sink_col, sink_ref[layer, head], -jnp.inf)
                s = jnp.where(mask, s, fill)
                e = jnp.exp(s - jnp.max(s, axis=1, keepdims=True))
                pv = _dot(e.astype(BF16), vvw[h])
                halves.append(pv[:, :LANES] / pv[:, LANES:])
            o_ref[0, pl.ds(q0, nq), p * LANES:(p + 1) * LANES] = jnp.where(low, halves[0], halves[1])

    n_full = seq // WINDOW
    group = 16

    def body(it, carry):
        for t in range(group):
            i = it * group + t
            q0 = pl.multiple_of(i * WINDOW, WINDOW)
            ks = pl.multiple_of(jnp.maximum(q0 - WINDOW, 0), WINDOW)
            sink_col = jnp.where(i == 0, 2 * WINDOW - 1, 0) if t == 0 else 0
            tile(q0, ks, WINDOW, 2 * WINDOW, sink_col)
        return carry

    lax.fori_loop(0, n_full // group, body, 0)
    for i in range(n_full // group * group, n_full):
        tile(i * WINDOW, max(i - 1, 0) * WINDOW, WINDOW, 2 * WINDOW, 2 * WINDOW - 1 if i == 0 else 0)
    if seq > n_full * WINDOW:
        tile(seq - WINDOW, seq - 2 * WINDOW, WINDOW, 2 * WINDOW, 0)


def _attn_prompt(layer, sinks, q, k, v, chunk):
    b, seq, _ = q.shape
    assert seq >= 2 * WINDOW
    kern = functools.partial(_attn_prompt_kernel, layer=layer, seq=seq, chunk=chunk)
    seq_spec = lambda w_: pl.BlockSpec((1, seq, w_), lambda i: (i, 0, 0))
    return pl.pallas_call(
        kern,
        grid=(b,),
        in_specs=[pl.BlockSpec(memory_space=pltpu.SMEM),
                  seq_spec(ATTN_W), seq_spec(KV_W), seq_spec(KV_W)],
        out_specs=seq_spec(ATTN_W),
        out_shape=jax.ShapeDtypeStruct((b, seq, ATTN_W), F32),
        scratch_shapes=[pltpu.VMEM((N_KV_HEADS, seq, LANES), BF16),
                        pltpu.VMEM((N_KV_HEADS, seq, LANES), BF16),
                        pltpu.VMEM((N_KV_HEADS, seq, 2 * LANES), BF16)],
        compiler_params=_params(),
        name="attn_prompt",
    )(sinks, q, k, v)


def _lru_gates(u, wab, bab, logsig):
    ga = _dot(u.astype(BF16), wab) + bab
    r = jax.nn.sigmoid(ga[:, :LRU_W])
    ig = jax.nn.sigmoid(ga[:, LRU_W:])
    log_a = LRU_C * r * logsig
    a = jnp.exp(log_a)
    b = jnp.sqrt(-jnp.tanh(log_a) * (a * a + 1.0)) * ig * u
    return a, b


def _scan_interleaved(a_ref, b_ref, h0, rows):
    piece = rows // SUBLANES
    at = lambda i: slice(i * SUBLANES, (i + 1) * SUBLANES)
    hs = [jnp.zeros((SUBLANES, LANES), F32) for _ in range(N_SLABS)]
    ps = [jnp.ones((SUBLANES, LANES), F32) for _ in range(N_SLABS)]
    for i in range(piece):
        for s in range(N_SLABS):
            a = a_ref[s, at(i), :]
            hs[s] = a * hs[s] + b_ref[s, at(i), :]
            ps[s] = a * ps[s]
            b_ref[s, at(i), :] = hs[s]
            a_ref[s, at(i), :] = ps[s]

    sub = lax.broadcasted_iota(jnp.int32, (SUBLANES, LANES), 0)
    carry_in, h_last = [], []
    for s in range(N_SLABS):
        cin = jnp.broadcast_to(h0[:, s * LANES:(s + 1) * LANES], (SUBLANES, LANES))
        acc = jnp.zeros((SUBLANES, LANES), F32)
        for c in range(SUBLANES):
            acc = jnp.where(sub == c, cin, acc)
            end = hs[s] + ps[s] * cin
            cin = jnp.broadcast_to(end[c:c + 1, :], (SUBLANES, LANES))
        carry_in.append(acc)
        h_last.append(cin[0:1, :])

    for i in range(piece):
        for s in range(N_SLABS):
            b_ref[s, at(i), :] = b_ref[s, at(i), :] + a_ref[s, at(i), :] * carry_in[s]
    return jnp.concatenate(h_last, axis=1)


def _meta_front_tile(x_ref, meta_ref, tiles_per_seq):
    blk = x_ref[...]
    n_meta = meta_ref.shape[0]
    with_meta = jnp.concatenate([meta_ref[...], blk[:blk.shape[0] - n_meta]], axis=0)
    return jnp.where(pl.program_id(0) % tiles_per_seq == 0, with_meta, blk)


def _inproj_lru_kernel(*refs, tm, tiles_per_seq, meta_front):
    x_ref, refs = refs[0], refs[1:]
    meta_ref, refs = (refs[0], refs[1:]) if meta_front else (None, refs)
    (g_ref, w_ref, cos_ref, sin_ref, cw_ref, cb_ref, wab_ref, bab_ref, lam_ref,
     q_ref, k_ref, v_ref, h_out_ref, gate_ref, tail_ref, ht_ref, k_last_ref, v_last_ref,
     xs_ref, u_ref, a_ref, b_ref, h_ref) = refs
    pad = SUBLANES
    piece = tm // SUBLANES
    slab = lambda s: slice(s * LANES, (s + 1) * LANES)

    @pl.when(pl.program_id(0) % tiles_per_seq == 0)
    def _():
        xs_ref[:, 0:pad, :] = jnp.zeros((N_SLABS, pad, LANES), F32)
        h_ref[...] = jnp.zeros((1, LRU_W), F32)

    x = _meta_front_tile(x_ref, meta_ref, tiles_per_seq) if meta_front else x_ref[...]
    hn = _rms(x, g_ref[...]).astype(BF16)
    w_cols = lambda lo, hi: w_ref[:, lo:hi].astype(BF16)
    xb = _dot(hn, w_cols(XB_COL, GATE_COL))
    gate_ref[...] = _dot(hn, w_cols(GATE_COL, IN_W))
    tail_ref[0] = xb[tm - pad:, :]
    for s in range(N_SLABS):
        xs_ref[s, pad:pad + tm, :] = xb[:, slab(s)]

    for s in range(N_SLABS):
        bias = jnp.broadcast_to(cb_ref[:, slab(s)], (SUBLANES, LANES))
        wts = [jnp.broadcast_to(cw_ref[j:j + 1, slab(s)], (SUBLANES, LANES))
               for j in range(CONV_W)]
        taps = [xs_ref[s, pl.ds(pad - (CONV_W - 1) + k, SUBLANES, stride=piece), :]
                for k in range(CONV_W - 1)]
        for i in range(piece):
            taps.append(xs_ref[s, pl.ds(pad + i, SUBLANES, stride=piece), :])
            u = bias
            for j in range(CONV_W):
                u = u + taps[j] * wts[j]
            u_ref[i * SUBLANES:(i + 1) * SUBLANES, slab(s)] = u
            taps.pop(0)
    for s in range(N_SLABS):
        xs_ref[s, 0:pad, :] = xs_ref[s, tm:tm + pad, :]

    a, b = _lru_gates(u_ref[...], wab_ref[...], bab_ref[...], jax.nn.log_sigmoid(lam_ref[...]))
    for s in range(N_SLABS):
        a_ref[s] = a[:, slab(s)]
        b_ref[s] = b[:, slab(s)]
    h_last = _scan_interleaved(a_ref, b_ref, h_ref[...], tm)
    h_ref[...] = h_last
    ht_ref[0] = h_last
    for s in range(N_SLABS):
        for i in range(piece):
            h_out_ref[s, pl.ds(i, SUBLANES, stride=piece), :] = (
                b_ref[s, i * SUBLANES:(i + 1) * SUBLANES, :])

    _store_qkv(_dot(hn, w_cols(0, XB_COL)), cos_ref, sin_ref, q_ref, k_ref, v_ref)
    k_last_ref[0] = k_ref[tm - WINDOW:, :]
    v_last_ref[0] = v_ref[tm - WINDOW:, :]


def _meta_front_spec(tm, tiles_per_seq, seq_in, n_meta):
    r = SUBLANES
    assert seq_in % r == 0 and tm % r == 0 and n_meta % r == 0

    def start(i):
        j = i % tiles_per_seq
        return r * ((i // tiles_per_seq) * (seq_in // r)
                    + jnp.maximum(j * (tm // r) - n_meta // r, 0)), 0
    return pl.BlockSpec((pl.Element(tm), pl.Element(D_MODEL)), start)


def _inproj_lru(layer, x, meta, g, w, cos, sin, cw, cb, wab, bab, lam, tm, tiles_per_seq, n_seq):
    t = n_seq * tiles_per_seq * tm
    assert tm % SUBLANES == 0 and tm >= WINDOW
    meta_front = meta is not None
    kern = functools.partial(_inproj_lru_kernel, tm=tm, tiles_per_seq=tiles_per_seq,
                             meta_front=meta_front)
    tok = lambda w_: pl.BlockSpec((tm, w_), lambda i: (i, 0))
    tab = pl.BlockSpec((tm, LANES), lambda i: (i % tiles_per_seq, 0))
    per_seq = lambda r: pl.BlockSpec((1, r, LRU_W), lambda i: (i // tiles_per_seq, 0, 0))
    last_kv = pl.BlockSpec((1, WINDOW, KV_W), lambda i: (i // tiles_per_seq, 0, 0))
    lyr = lambda a: _layer_spec(a, layer)
    if meta_front:
        x_specs = [_meta_front_spec(tm, tiles_per_seq, x.shape[0] // n_seq, meta.shape[0]),
                   _const_spec(meta.shape)]
        x_args = (x, meta)
    else:
        x_specs, x_args = [tok(D_MODEL)], (x,)
    return pl.pallas_call(
        kern,
        grid=(t // tm,),
        in_specs=x_specs + [lyr(g), lyr(w), tab, tab,
                            lyr(cw), lyr(cb), lyr(wab), lyr(bab), lyr(lam)],
        out_specs=[tok(ATTN_W), tok(KV_W), tok(KV_W),
                   pl.BlockSpec((N_SLABS, tm, LANES), lambda i: (0, i, 0)), tok(LRU_W),
                   per_seq(SUBLANES), per_seq(1), last_kv, last_kv],
        out_shape=[jax.ShapeDtypeStruct((t, ATTN_W), BF16),
                   jax.ShapeDtypeStruct((t, KV_W), F32),
                   jax.ShapeDtypeStruct((t, KV_W), F32),
                   jax.ShapeDtypeStruct((N_SLABS, t, LANES), F32),
                   jax.ShapeDtypeStruct((t, LRU_W), F32),
                   jax.ShapeDtypeStruct((n_seq, SUBLANES, LRU_W), F32),
                   jax.ShapeDtypeStruct((n_seq, 1, LRU_W), F32),
                   jax.ShapeDtypeStruct((n_seq, WINDOW, KV_W), F32),
                   jax.ShapeDtypeStruct((n_seq, WINDOW, KV_W), F32)],
        scratch_shapes=[pltpu.VMEM((N_SLABS, tm + SUBLANES, LANES), F32),
                        pltpu.VMEM((tm, LRU_W), F32),
                        pltpu.VMEM((N_SLABS, tm, LANES), F32),
                        pltpu.VMEM((N_SLABS, tm, LANES), F32),
                        pltpu.VMEM((1, LRU_W), F32)],
        compiler_params=_params(),
        name="inproj_lru",
    )(*x_args, g, w, cos, sin, cw, cb, wab, bab, lam)


def _div_pow2(x, n):
    assert n & (n - 1) == 0
    return x >> (n.bit_length() - 1)


def _mod_pow2(x, n):
    assert n & (n - 1) == 0
    return x & (n - 1)


def _attn_sample_kernel(sink_ref, q_ref, kn_ref, vn_ref, ck_ref, cv_ref, nk_all_ref, nv_all_ref,
                        o_ref, nk_ref, nv_ref, os_ref, *, layer, group, n_new):
    del nk_all_ref, nv_all_ref
    rpb = PAIRS_PER_KV * n_new
    new_pad = kn_ref.shape[1]
    rows = group * rpb
    ncols = group * new_pad
    row = lax.broadcasted_iota(jnp.int32, (rows, 1), 0)
    qpos = _mod_pow2(row, n_new)
    row_seq = _div_pow2(row, rpb)
    second_pair = _mod_pow2(_div_pow2(row, n_new), PAIRS_PER_KV) == 1
    col = lax.broadcasted_iota(jnp.int32, (1, WINDOW), 1)
    mask_c = col > qpos
    if PAST_LEN < WINDOW:
        mask_c = mask_c & (col >= WINDOW - PAST_LEN)
    ncol = lax.broadcasted_iota(jnp.int32, (1, ncols), 1)
    mask_n = (_div_pow2(ncol, new_pad) == row_seq) & (_mod_pow2(ncol, new_pad) <= qpos)
    low = lax.broadcasted_iota(jnp.int32, (rpb, LANES), 1) < HEAD_DIM

    kn_all = kn_ref[...].reshape(ncols, KV_W)
    vn_all = vn_ref[...].reshape(ncols, KV_W)
    knt_all = kn_all.T
    vnt_all = vn_all.T
    vn_lane = lax.broadcasted_iota(jnp.int32, vn_all.shape, 1) < HEAD_DIM
    vn_roll = pltpu.roll(vn_all, HEAD_DIM, 1)

    keep = WINDOW - n_new
    cache_lane = lax.broadcasted_iota(jnp.int32, (HEAD_DIM, WINDOW), 1) < keep

    for h in range(N_KV_HEADS):
        hd = slice(h * HEAD_DIM, (h + 1) * HEAD_DIM)
        q_all = q_ref[:, h].reshape(rows, LANES).astype(BF16)
        q_seq = [q_ref[b, h].astype(BF16) for b in range(group)]
        knt_h = knt_all[hd, :].astype(BF16)
        vvn = (jnp.where(vn_lane, vn_all, vn_roll) if h == 0
               else jnp.where(vn_lane, vn_roll, vn_all)).astype(BF16)
        kt, vt2 = [], []
        for b in range(group):
            kt.append(ck_ref[b, h].astype(BF16))
            vt = cv_ref[b, h].astype(BF16)
            vt2.append(jnp.concatenate([vt, vt], axis=0))
        probs, dens = [], []
        for first in (True, False):
            embed = lambda t: jnp.concatenate(
                [t, jnp.zeros_like(t)] if first else [jnp.zeros_like(t), t], axis=0)
            s_new = jnp.where(mask_n, _dot(q_all, embed(knt_h)), -jnp.inf)
            s_c = jnp.concatenate(
                [_dot(q_seq[b], embed(kt[b])) for b in range(group)], axis=0)
            s_c = jnp.where(mask_c, s_c, -jnp.inf)
            pair0 = 2 * h * PAIRS_PER_KV + (0 if first else 1)
            sink = jnp.where(second_pair, sink_ref[layer,pair0 + 2], sink_ref[layer,pair0])
            mx = jnp.maximum(jnp.maximum(jnp.max(s_c, axis=1, keepdims=True),
                                         jnp.max(s_new, axis=1, keepdims=True)), sink)
            p_c = jnp.exp(s_c - mx)
            p_n = jnp.exp(s_new - mx)
            dens.append(jnp.sum(p_c, axis=1, keepdims=True) + jnp.sum(p_n, axis=1, keepdims=True)
                        + jnp.exp(sink - mx))
            probs.append((p_c, _dot(p_n.astype(BF16), vvn)))
        for b in range(group):
            rs = slice(b * rpb, (b + 1) * rpb)
            pc = jnp.concatenate([probs[0][0][rs], probs[1][0][rs]], axis=0).astype(BF16)
            oc = _dot_nt(pc, vt2[b])
            o_first = (oc[:rpb] + probs[0][1][rs]) / dens[0][rs]
            o_second = (oc[rpb:] + probs[1][1][rs]) / dens[1][rs]
            first_row = (b * N_KV_HEADS + h) * rpb
            os_ref[first_row:first_row + rpb, :] = jnp.where(low, o_first, o_second)
            shift = (keep - b * new_pad) % ncols
            nk_ref[b, h] = jnp.where(cache_lane, pltpu.roll(ck_ref[b, h], keep, 1),
                                     pltpu.roll(knt_all[hd, :], shift, 1)[:, :WINDOW])
            nv_ref[b, h] = jnp.where(cache_lane, pltpu.roll(cv_ref[b, h], keep, 1),
                                     pltpu.roll(vnt_all[hd, :], shift, 1)[:, :WINDOW])

    for p in range(N_PAIRS):
        for t in range(n_new):
            first = (p // PAIRS_PER_KV) * rpb + (p % PAIRS_PER_KV) * n_new + t
            o_ref[t, :, p * LANES:(p + 1) * LANES] = (
                os_ref[pl.ds(first, group, stride=N_KV_HEADS * rpb), :])


def _attn_sample(layer, sinks, q, kn, vn, ck, cv, nk_all, nv_all, group, n_new):
    bs = q.shape[0]
    kern = functools.partial(_attn_sample_kernel, layer=layer, group=group, n_new=n_new)
    seq4 = lambda a: pl.BlockSpec((group,) + a.shape[1:], lambda i: (i, 0, 0, 0))
    seq3 = lambda a: pl.BlockSpec((group,) + a.shape[1:], lambda i: (i, 0, 0))
    cache = pl.BlockSpec((None, group) + ck.shape[2:], lambda i: (layer, i, 0, 0, 0))
    in_hbm = pl.BlockSpec(memory_space=pl.ANY)
    return pl.pallas_call(
        kern,
        grid=(bs // group,),
        in_specs=[pl.BlockSpec(memory_space=pltpu.SMEM),
                  seq4(q), seq3(kn), seq3(vn), cache, cache, in_hbm, in_hbm],
        out_specs=[pl.BlockSpec((n_new, group, ATTN_W), lambda i: (0, i, 0)), cache, cache],
        out_shape=[jax.ShapeDtypeStruct((n_new, bs, ATTN_W), F32),
                   jax.ShapeDtypeStruct(nk_all.shape, F32),
                   jax.ShapeDtypeStruct(nv_all.shape, F32)],
        scratch_shapes=[pltpu.VMEM((group * q.shape[1] * q.shape[2], LANES), F32)],
        input_output_aliases={6: 1, 7: 2},
        compiler_params=_params(),
        name="attn_sample",
    )(sinks, q, kn, vn, ck, cv, nk_all, nv_all)


def _rope_tables(pos):
    half = HEAD_DIM // 2
    inv = ROPE_THETA ** (-jnp.arange(half, dtype=F32) / half)
    ang = pos.astype(F32)[:, None] * inv[None, :]
    cos, sin = jnp.cos(ang), jnp.sin(ang)
    reps = LANES // HEAD_DIM
    return (jnp.concatenate([cos, cos] * reps, axis=1),
            jnp.concatenate([-sin, sin] * reps, axis=1))


def _block_diag(w):
    rows = [jnp.pad(w[:, h], ((0, 0), (0, 0), (h * LRU_BLOCK_W, (LRU_BLOCKS - 1 - h) * LRU_BLOCK_W)))
            for h in range(LRU_BLOCKS)]
    return jnp.concatenate(rows, axis=1)


def _pick_tile(n, candidates):
    for c in candidates:
        if n % c == 0:
            return c
    raise ValueError(f"no tile for {n} tokens")


def kernel(x_prompt, x_sample, cache_k, cache_v, state_h, state_conv, meta_tokens, pre_mix_norm,
           w_in, sinks, conv_w, conv_b, w_a, b_a, w_i, b_i, lam, attn_out_norm, lru_out_norm,
           w_out, post_mix_norm, pre_ffn_norm, w_gate, w_up, w_down, post_ffn_norm):
    bp, seq_in, _ = x_prompt.shape
    seq = seq_in + N_META
    bs, n_new, _ = x_sample.shape
    assert seq % SUBLANES == 0 and (seq % WINDOW) % 16 == 0 and n_new >= CONV_W - 1

    vec = lambda a: a.reshape(DEPTH, 1, -1)
    w_out_b = w_out.astype(BF16)
    w_gate_b = w_gate.astype(BF16)
    w_up_b = w_up.astype(BF16)
    w_down_b = w_down.astype(BF16)
    wab = jnp.concatenate([_block_diag(w_a), _block_diag(w_i)], axis=2).astype(BF16)
    bab = vec(jnp.concatenate([b_a, b_i], axis=1))
    g_pre, g_att, g_lru = vec(pre_mix_norm), vec(attn_out_norm), vec(lru_out_norm)
    g_pm, g_pf, g_po = vec(post_mix_norm), vec(pre_ffn_norm), vec(post_ffn_norm)
    cb, lam_v = vec(conv_b), vec(lam)

    assert DEPTH >= 2
    xp = x_prompt.reshape(bp * seq_in, D_MODEL)
    meta = meta_tokens.astype(x_prompt.dtype)
    xs = x_sample.transpose(1, 0, 2).reshape(n_new * bs, D_MODEL)

    cos_p, sin_p = _rope_tables(jnp.arange(seq, dtype=jnp.int32))
    pos_s = PAST_LEN + jnp.arange(n_new, dtype=jnp.int32)
    cos_s, sin_s = _rope_tables(jnp.repeat(pos_s, bs))

    tm_p = _pick_tile(seq, (688, 512, 256, 128, 16))
    tps = seq // tm_p
    tm_y = _pick_tile(seq_in, (512, 256, 128, 16))
    new_pad = SUBLANES
    group = 2 * LANES // new_pad
    assert n_new <= new_pad and bs % group == 0 and WINDOW == LANES

    ck_t = cache_k.transpose(0, 1, 3, 4, 2)
    cv_t = cache_v.transpose(0, 1, 3, 4, 2)
    conv_s = state_conv.transpose(0, 2, 1, 3)

    nk_all = jnp.zeros(ck_t.shape, F32)
    nv_all = jnp.zeros(cv_t.shape, F32)
    pk, pv, ph, pc, sh, sc = [], [], [], [], [], []
    for l in range(DEPTH):
        meta_l = meta if l == 0 else None
        q, k, v, h_seq, gate, xb_tail, ht, k_last, v_last = _inproj_lru(
            l, xp, meta_l, g_pre, w_in, cos_p, sin_p, conv_w, cb, wab, bab, lam_v, tm_p, tps, bp)
        k3 = k.reshape(bp, seq, KV_W)
        v3 = v.reshape(bp, seq, KV_W)
        att = _attn_prompt(l, sinks, q.reshape(bp, seq, ATTN_W), k3, v3, tm_p)
        ffn_tiling = (tm_p, tps, bp, 0) if l < DEPTH - 1 else (tm_y, seq_in // tm_y, bp, N_META)
        xp = _ffn(l, xp, meta_l, att.reshape(bp * seq, ATTN_W), h_seq, gate,
                  g_att, g_lru, w_out_b, g_pm, g_pf, w_gate_b, w_up_b, w_down_b, g_po, *ffn_tiling)
        pk.append(k_last.reshape(bp, WINDOW, N_KV_HEADS, HEAD_DIM))
        pv.append(v_last.reshape(bp, WINDOW, N_KV_HEADS, HEAD_DIM))
        ph.append(ht.reshape(bp, LRU_W))
        pc.append(xb_tail[:, -(CONV_W - 1):])

        q, kn, vn, xb, gate, h_slabs, ht = _inproj_sample(
            l, xs, g_pre, w_in, cos_s, sin_s, conv_s, state_h, conv_w, cb, wab, bab, lam_v,
            n_new, new_pad)
        q4 = q.reshape(bs, N_KV_HEADS, PAIRS_PER_KV * n_new, LANES)
        seq_major = lambda a: a.reshape(n_new, bs, -1).transpose(1, 0, 2)
        att, nk_all, nv_all = _attn_sample(l, sinks, q4, kn.reshape(bs, new_pad, KV_W),
                                           vn.reshape(bs, new_pad, KV_W), ck_t, cv_t,
                                           nk_all, nv_all, group, n_new)
        xs = _ffn(l, xs, None, att.reshape(n_new * bs, ATTN_W), h_slabs, gate,
                  g_att, g_lru, w_out_b, g_pm, g_pf, w_gate_b, w_up_b, w_down_b, g_po,
                  bs * n_new, 1, 1, 0)
        sh.append(ht)
        sc.append(seq_major(xb)[:, -(CONV_W - 1):])

    y_prompt = xp.reshape(bp, seq_in, D_MODEL)
    y_sample = xs.reshape(n_new, bs, D_MODEL).transpose(1, 0, 2)
    sample_k = nk_all.transpose(0, 1, 4, 2, 3)
    sample_v = nv_all.transpose(0, 1, 4, 2, 3)
    return (y_prompt, y_sample, jnp.stack(pk), jnp.stack(pv), jnp.stack(ph), jnp.stack(pc),
            sample_k, sample_v, jnp.stack(sh), jnp.stack(sc))
```
